```python
import math
import jax, jax.numpy as jnp
from jax import lax
import numpy as np

D_MODEL = 1024
BATCH = 32
SEQ = 2048
DEPTH = 4
DEC_BATCH = 16
DEC_SEQ = 2048
PAST_LEN = 128

N_EVEN = (DEPTH + 1) // 2
N_ODD = DEPTH // 2
MIX_WIDTH = D_MODEL
HALF = MIX_WIDTH // 2
HEAD_DIM = 64
EPS = 1e-6
NEG = -1e30

HY_WIDTH = HALF
HY_EMB = 33
HY_BANDS = (HY_EMB - 1) // 2
HY_FILT_HIDDEN = 64
HY_DECAY_TARGET = 1e-2
HY_FAST = 0.3
HY_SLOW = 1.5
HY_MIN_DECAY = math.log(HY_DECAY_TARGET) / HY_SLOW
HY_MAX_DECAY = math.log(HY_DECAY_TARGET) / HY_FAST
HY_SHIFT = 0.05

WA_HEADS = HALF // HEAD_DIM
WA_KV_HEADS = 2
WA_REP = WA_HEADS // WA_KV_HEADS
WA_WINDOW = 128
WA_BLOCK = 128
REL_BUCKETS = 32
REL_MAX_DIST = 128

S5_WIDTH = HALF
S5_GROUP = 16
S5_GROUPS = S5_WIDTH // S5_GROUP
S5_STATE = 64
DT_MIN = 1e-3
DT_MAX = 1e-1

MLA_HEADS = 8
MLA_NOPE = 64
MLA_ROPE = 32
MLA_V = 64
MLA_Q_RANK = 256
MLA_KV_RANK = 128
ROPE_THETA = 10000.0
Q_BLOCK = 128

MEM_TOKENS = 256
CA_HEADS = 4
CA_HEAD_DIM = 128

N_EXPERTS = 16
EC_CAPACITY_FACTOR = 2
D_EXPERT = 2048

AB_IN = 3 * HY_WIDTH + (WA_HEADS + 2 * WA_KV_HEADS) * HEAD_DIM
CD_IN = S5_WIDTH + MLA_Q_RANK + MLA_KV_RANK + MLA_ROPE

kernel_name = "hybrid_hyena_swa_s5_mla_ec_encoder"


def rmsnorm(x, g):
    xf = x.astype(jnp.float32)
    y = xf * lax.rsqrt(jnp.mean(xf * xf, axis=-1, keepdims=True) + EPS)
    return (y * g.astype(jnp.float32)).astype(x.dtype)


def short_conv(u, w, b):
    L = u.shape[1]
    up = jnp.pad(u, ((0, 0), (1, 1), (0, 0)))
    return up[:, :L] * w[0] + up[:, 1:L + 1] * w[1] + up[:, 2:] * w[2] + b


def hyena_filters(L, w1, b1, w2, b2, w3, freq):
    f32 = jnp.float32
    t = jnp.linspace(0.0, 1.0, L, dtype=f32)[:, None]
    ang = 2.0 * math.pi * jnp.arange(L, dtype=f32)[:, None] / L
    bands = jnp.linspace(1e-4, HY_BANDS - 1, HY_BANDS, dtype=f32)[None, :]
    z = jnp.concatenate([t, jnp.cos(bands * ang), -jnp.sin(bands * ang)], axis=-1)
    fr = freq.astype(f32)
    h = jnp.sin(fr * (z @ w1.astype(f32) + b1.astype(f32)))
    h = jnp.sin(fr * (h @ w2.astype(f32) + b2.astype(f32)))
    h = h @ w3.astype(f32)
    deltas = jnp.abs(jnp.linspace(HY_MIN_DECAY, HY_MAX_DECAY, HY_WIDTH, dtype=f32))
    window = jnp.exp(-t * deltas[None, :]) + HY_SHIFT
    h = h.reshape(L, 2, HY_WIDTH) * window[:, None, :]
    return h[:, 0], h[:, 1]


def bidir_fftconv(v, h_fwd, h_bwd, d):
    L = v.shape[1]
    k = jnp.concatenate([h_fwd, jnp.zeros_like(h_fwd[:1]), h_bwd[:0:-1]], axis=0)
    kf = jnp.fft.rfft(k, n=2 * L, axis=0)
    vf32 = v.astype(jnp.float32)
    vf = jnp.fft.rfft(vf32, n=2 * L, axis=1)
    y = jnp.fft.irfft(vf * kf[None], n=2 * L, axis=1)[:, :L]
    return y + vf32 * d.astype(jnp.float32)


def rel_bucket(rel):
    nb = REL_BUCKETS // 2
    max_exact = nb // 2
    ret = (rel > 0).astype(jnp.int32) * nb
    n = jnp.abs(rel)
    nf = jnp.maximum(n, 1).astype(jnp.float32)
    large = max_exact + (jnp.log(nf / max_exact) / math.log(REL_MAX_DIST / max_exact)
                         * (nb - max_exact)).astype(jnp.int32)
    large = jnp.minimum(large, nb - 1)
    return ret + jnp.where(n < max_exact, n, large)


def window_bias_and_band(rel_bias):
    j = jnp.arange(WA_BLOCK, dtype=jnp.int32)[:, None]
    s = jnp.arange(3 * WA_BLOCK, dtype=jnp.int32)[None, :]
    rel = (s - WA_BLOCK) - j
    bias = jnp.transpose(rel_bias.astype(jnp.float32)[rel_bucket(rel)], (2, 0, 1))
    band = jnp.abs(rel) <= WA_WINDOW
    return bias, band


def window_attn(q, k, v, sink, bias, band):
    Bsz, L = q.shape[0], q.shape[1]
    nb = L // WA_BLOCK
    kp = jnp.pad(k, ((0, 0), (WA_BLOCK, WA_BLOCK), (0, 0), (0, 0)))
    vp = jnp.pad(v, ((0, 0), (WA_BLOCK, WA_BLOCK), (0, 0), (0, 0)))
    scale = HEAD_DIM ** -0.5
    sk = sink.astype(jnp.float32)[None, :, None, None]

    def block(i):
        qb = lax.dynamic_slice_in_dim(q, i * WA_BLOCK, WA_BLOCK, axis=1)
        kb = lax.dynamic_slice_in_dim(kp, i * WA_BLOCK, 3 * WA_BLOCK, axis=1)
        vb = lax.dynamic_slice_in_dim(vp, i * WA_BLOCK, 3 * WA_BLOCK, axis=1)
        qb = qb.reshape(Bsz, WA_BLOCK, WA_KV_HEADS, WA_REP, HEAD_DIM)
        s = jnp.einsum('bqgrd,bkgd->bgrqk', qb, kb).astype(jnp.float32) * scale
        s = s.reshape(Bsz, WA_HEADS, WA_BLOCK, 3 * WA_BLOCK) + bias[None]
        kpos = i * WA_BLOCK - WA_BLOCK + jnp.arange(3 * WA_BLOCK)
        valid = band & ((kpos >= 0) & (kpos < L))[None, :]
        s = jnp.where(valid[None, None], s, NEG)
        m = jnp.maximum(jnp.max(s, axis=-1, keepdims=True), sk)
        p = jnp.exp(s - m)
        p = p / (jnp.sum(p, axis=-1, keepdims=True) + jnp.exp(sk - m))
        p = p.astype(v.dtype).reshape(Bsz, WA_KV_HEADS, WA_REP, WA_BLOCK, 3 * WA_BLOCK)
        o = jnp.einsum('bgrqk,bkgd->bqgrd', p, vb)
        return o.reshape(Bsz, WA_BLOCK, WA_HEADS * HEAD_DIM)

    out = lax.map(block, jnp.arange(nb))
    return jnp.transpose(out, (1, 0, 2, 3)).reshape(Bsz, L, WA_HEADS * HEAD_DIM)


def _ssm_combine(e_i, e_j):
    a_i, b_i = e_i
    a_j, b_j = e_j
    return a_j * a_i, a_j * b_i + b_j


def s5_bidir(u, a_re, a_im, log_dt, b_re, b_im, c_re, c_im, d):
    f32 = jnp.float32
    Bsz, L, _ = u.shape
    uf = u.astype(f32)
    ug = uf.reshape(Bsz, L, S5_GROUPS, S5_GROUP).astype(jnp.complex64)
    y = uf * d.astype(f32)
    for direction in range(2):
        lam = lax.complex(a_re[direction].astype(f32), a_im[direction].astype(f32))
        dt = jnp.exp(log_dt[direction].astype(f32))[:, None]
        abar = jnp.exp(lam * dt)
        bmat = lax.complex(b_re[direction].astype(f32), b_im[direction].astype(f32))
        bbar = ((abar - 1.0) / lam)[..., None] * bmat
        bu = jnp.einsum('gpc,blgc->blgp', bbar, ug)
        a_seq = jnp.broadcast_to(abar, bu.shape)
        _, hstate = lax.associative_scan(_ssm_combine, (a_seq, bu), axis=1, reverse=(direction == 1))
        cmat = lax.complex(c_re[direction].astype(f32), c_im[direction].astype(f32))
        y = y + jnp.einsum('gcp,blgp->blgc', cmat, hstate).real.reshape(Bsz, L, S5_WIDTH)
    return y.astype(u.dtype)


def rope_tables(L):
    inv = 1.0 / (ROPE_THETA ** (jnp.arange(0, MLA_ROPE, 2, dtype=jnp.float32) / MLA_ROPE))
    ang = jnp.arange(L, dtype=jnp.float32)[:, None] * inv[None, :]
    return jnp.cos(ang), jnp.sin(ang)


def apply_rope(x, cos, sin):
    half = MLA_ROPE // 2
    x1, x2 = x[..., :half].astype(jnp.float32), x[..., half:].astype(jnp.float32)
    c, s = cos[None, :, None, :], sin[None, :, None, :]
    return jnp.concatenate([x1 * c - x2 * s, x1 * s + x2 * c], axis=-1).astype(x.dtype)


def mla_attn(c_q, c_kv, k_rope_in, q_norm, w_uq, kv_norm, w_ukv, cos, sin):
    Bsz, L = c_q.shape[0], c_q.shape[1]
    q = (rmsnorm(c_q, q_norm) @ w_uq).reshape(Bsz, L, MLA_HEADS, MLA_NOPE + MLA_ROPE)
    q_nope, q_rope = q[..., :MLA_NOPE], apply_rope(q[..., MLA_NOPE:], cos, sin)
    kv = (rmsnorm(c_kv, kv_norm) @ w_ukv).reshape(Bsz, L, MLA_HEADS, MLA_NOPE + MLA_V)
    k_nope, v = kv[..., :MLA_NOPE], kv[..., MLA_NOPE:]
    k_rope = apply_rope(k_rope_in[:, :, None, :], cos, sin)[:, :, 0]
    scale = (MLA_NOPE + MLA_ROPE) ** -0.5

    def qblock(i):
        qn = lax.dynamic_slice_in_dim(q_nope, i * Q_BLOCK, Q_BLOCK, axis=1)
        qr = lax.dynamic_slice_in_dim(q_rope, i * Q_BLOCK, Q_BLOCK, axis=1)
        s = (jnp.einsum('bqhd,bkhd->bhqk', qn, k_nope)
             + jnp.einsum('bqhd,bkd->bhqk', qr, k_rope)).astype(jnp.float32) * scale
        p = jax.nn.softmax(s, axis=-1).astype(v.dtype)
        return jnp.einsum('bhqk,bkhd->bqhd', p, v).reshape(Bsz, Q_BLOCK, MLA_HEADS * MLA_V)

    out = lax.map(qblock, jnp.arange(L // Q_BLOCK))
    return jnp.transpose(out, (1, 0, 2, 3)).reshape(Bsz, L, MLA_HEADS * MLA_V)


def mem_attn(h, mem_n, w_q, w_kv, w_o):
    Bsz, L = h.shape[0], h.shape[1]
    M = mem_n.shape[1]
    q = (h @ w_q).reshape(Bsz, L, CA_HEADS, CA_HEAD_DIM)
    kv = (mem_n @ w_kv).reshape(Bsz, M, 2, CA_HEADS, CA_HEAD_DIM)
    k, v = kv[:, :, 0], kv[:, :, 1]
    s = jnp.einsum('bqhd,bkhd->bhqk', q, k).astype(jnp.float32) * CA_HEAD_DIM ** -0.5
    p = jax.nn.softmax(s, axis=-1).astype(v.dtype)
    o = jnp.einsum('bhqk,bkhd->bqhd', p, v).reshape(Bsz, L, CA_HEADS * CA_HEAD_DIM)
    return o @ w_o


def ec_moe(h, w_router, w_gate, w_up, w_down):
    Bsz, L, D = h.shape
    T = Bsz * L
    cap = EC_CAPACITY_FACTOR * T // N_EXPERTS
    tok = h.reshape(T, D)
    aff = jax.nn.softmax((tok @ w_router).astype(jnp.float32), axis=-1)
    gate, idx = lax.top_k(aff.T, cap)
    xe = tok[idx]
    hid = jax.nn.silu(jnp.einsum('ecd,edf->ecf', xe, w_gate)) * jnp.einsum('ecd,edf->ecf', xe, w_up)
    ye = jnp.einsum('ecf,efd->ecd', hid, w_down) * gate[..., None].astype(h.dtype)
    out = jnp.zeros_like(tok).at[idx.reshape(-1)].add(ye.reshape(-1, D))
    return out.reshape(Bsz, L, D)


def mixer_ab(h, e, p, wa_bias, wa_band):
    Bsz, L = h.shape[0], h.shape[1]
    proj = h @ p['w_in_ab'][e]
    o0 = 3 * HY_WIDTH
    o1 = o0 + WA_HEADS * HEAD_DIM
    o2 = o1 + WA_KV_HEADS * HEAD_DIM
    u = short_conv(proj[..., :o0], p['hy_conv_w'][e], p['hy_conv_b'][e])
    x0, x1, v = u[..., :HY_WIDTH], u[..., HY_WIDTH:2 * HY_WIDTH], u[..., 2 * HY_WIDTH:]
    h_fwd, h_bwd = hyena_filters(L, p['hy_filt_w1'][e], p['hy_filt_b1'][e], p['hy_filt_w2'][e],
                                 p['hy_filt_b2'][e], p['hy_filt_w3'][e], p['hy_filt_freq'][e])
    z = bidir_fftconv(v * x1, h_fwd, h_bwd, p['hy_d'][e]).astype(h.dtype)
    y_hy = z * x0
    q = proj[..., o0:o1].reshape(Bsz, L, WA_HEADS, HEAD_DIM)
    k = proj[..., o1:o2].reshape(Bsz, L, WA_KV_HEADS, HEAD_DIM)
    vv = proj[..., o2:].reshape(Bsz, L, WA_KV_HEADS, HEAD_DIM)
    y_wa = window_attn(q, k, vv, p['attn_sink'][e], wa_bias, wa_band)
    return jnp.concatenate([y_hy, y_wa], axis=-1) @ p['w_out_ab'][e]


def mixer_cd(h, o, p, cos, sin):
    proj = h @ p['w_in_cd'][o]
    o0 = S5_WIDTH
    o1 = o0 + MLA_Q_RANK
    o2 = o1 + MLA_KV_RANK
    y = s5_bidir(proj[..., :o0], p['s5_a_re'][o], p['s5_a_im'][o], p['s5_log_dt'][o],
                 p['s5_b_re'][o], p['s5_b_im'][o], p['s5_c_re'][o], p['s5_c_im'][o], p['s5_d'][o])
    g = jax.nn.gelu(y)
    y_s5 = g * jax.nn.sigmoid(g @ p['s5_glu_w'][o] + p['s5_glu_b'][o])
    y_mla = mla_attn(proj[..., o0:o1], proj[..., o1:o2], proj[..., o2:], p['mla_q_norm'][o],
                     p['mla_w_uq'][o], p['mla_kv_norm'][o], p['mla_w_ukv'][o], cos, sin)
    return jnp.concatenate([y_s5, y_mla], axis=-1) @ p['w_out_cd'][o]


def run_trunk(x, mem, p):
    L = x.shape[1]
    cos, sin = rope_tables(L)
    wa_bias, wa_band = window_bias_and_band(p['rel_bias'])
    for layer in range(DEPTH):
        h = rmsnorm(x, p['ln_mix'][layer])
        if layer % 2 == 0:
            x = x + mixer_ab(h, layer // 2, p, wa_bias, wa_band)
        else:
            x = x + mixer_cd(h, layer // 2, p, cos, sin)
        x = x + mem_attn(rmsnorm(x, p['ln_cross'][layer]), rmsnorm(mem, p['ln_mem'][layer]),
                         p['ca_w_q'][layer], p['ca_w_kv'][layer], p['ca_w_o'][layer])
        x = x + ec_moe(rmsnorm(x, p['ln_ffn'][layer]), p['moe_w_router'][layer],
                       p['moe_w_gate'][layer], p['moe_w_up'][layer], p['moe_w_down'][layer])
    return rmsnorm(x, p['ln_final'])


def setup_inputs(seed: int = 0) -> dict:
    key = jax.random.key(seed)
    ks = iter(jax.random.split(key, 64))

    def nrm(shape, scale):
        return jax.random.normal(next(ks), shape, jnp.float32) * scale

    def gain(shape):
        return 1.0 + nrm(shape, 0.02)

    NE, NO, D = N_EVEN, N_ODD, D_MODEL
    n_idx = jnp.arange(S5_STATE, dtype=jnp.float32)
    return {
        'x_prompt': nrm((BATCH, SEQ, D), 1.0),
        'x_sample': nrm((DEC_BATCH, DEC_SEQ, D), 1.0),
        'mem_prompt': nrm((BATCH, MEM_TOKENS, D), 1.0),
        'mem_sample': nrm((DEC_BATCH, MEM_TOKENS, D), 1.0),
        'ln_mix': gain((DEPTH, D)),
        'ln_cross': gain((DEPTH, D)),
        'ln_mem': gain((DEPTH, D)),
        'ln_ffn': gain((DEPTH, D)),
        'ln_final': gain((D,)),
        'rel_bias': nrm((REL_BUCKETS, WA_HEADS), 0.1),
        'w_in_ab': nrm((NE, D, AB_IN), D ** -0.5),
        'w_out_ab': nrm((NE, MIX_WIDTH, D), MIX_WIDTH ** -0.5),
        'hy_conv_w': nrm((NE, 3, 3 * HY_WIDTH), 0.5),
        'hy_conv_b': nrm((NE, 3 * HY_WIDTH), 0.02),
        'hy_filt_w1': nrm((NE, HY_EMB, HY_FILT_HIDDEN), HY_EMB ** -0.5),
        'hy_filt_b1': nrm((NE, HY_FILT_HIDDEN), 0.02),
        'hy_filt_w2': nrm((NE, HY_FILT_HIDDEN, HY_FILT_HIDDEN), HY_FILT_HIDDEN ** -0.5),
        'hy_filt_b2': nrm((NE, HY_FILT_HIDDEN), 0.02),
        'hy_filt_w3': nrm((NE, HY_FILT_HIDDEN, 2 * HY_WIDTH), 0.1 * HY_FILT_HIDDEN ** -0.5),
        'hy_filt_freq': gain((NE, HY_FILT_HIDDEN)),
        'hy_d': nrm((NE, HY_WIDTH), 1.0),
        'attn_sink': nrm((NE, WA_HEADS), 0.5),
        'w_in_cd': nrm((NO, D, CD_IN), D ** -0.5),
        'w_out_cd': nrm((NO, MIX_WIDTH, D), MIX_WIDTH ** -0.5),
        's5_a_re': -0.5 + nrm((NO, 2, S5_GROUPS, S5_STATE), 0.01),
        's5_a_im': math.pi * n_idx + nrm((NO, 2, S5_GROUPS, S5_STATE), 0.01),
        's5_log_dt': jax.random.uniform(next(ks), (NO, 2, S5_GROUPS), jnp.float32,
                                        math.log(DT_MIN), math.log(DT_MAX)),
        's5_b_re': nrm((NO, 2, S5_GROUPS, S5_STATE, S5_GROUP), (2 * S5_GROUP) ** -0.5),
        's5_b_im': nrm((NO, 2, S5_GROUPS, S5_STATE, S5_GROUP), (2 * S5_GROUP) ** -0.5),
        's5_c_re': nrm((NO, 2, S5_GROUPS, S5_GROUP, S5_STATE), (2 * S5_STATE) ** -0.5),
        's5_c_im': nrm((NO, 2, S5_GROUPS, S5_GROUP, S5_STATE), (2 * S5_STATE) ** -0.5),
        's5_d': nrm((NO, S5_WIDTH), 1.0),
        's5_glu_w': nrm((NO, S5_WIDTH, S5_WIDTH), S5_WIDTH ** -0.5),
        's5_glu_b': nrm((NO, S5_WIDTH), 0.02),
        'mla_q_norm': gain((NO, MLA_Q_RANK)),
        'mla_w_uq': nrm((NO, MLA_Q_RANK, MLA_HEADS * (MLA_NOPE + MLA_ROPE)), MLA_Q_RANK ** -0.5),
        'mla_kv_norm': gain((NO, MLA_KV_RANK)),
        'mla_w_ukv': nrm((NO, MLA_KV_RANK, MLA_HEADS * (MLA_NOPE + MLA_V)), MLA_KV_RANK ** -0.5),
        'ca_w_q': nrm((DEPTH, D, CA_HEADS * CA_HEAD_DIM), D ** -0.5),
        'ca_w_kv': nrm((DEPTH, D, 2 * CA_HEADS * CA_HEAD_DIM), D ** -0.5),
        'ca_w_o': nrm((DEPTH, CA_HEADS * CA_HEAD_DIM, D), (CA_HEADS * CA_HEAD_DIM) ** -0.5),
        'moe_w_router': nrm((DEPTH, D, N_EXPERTS), D ** -0.5),
        'moe_w_gate': nrm((DEPTH, N_EXPERTS, D, D_EXPERT), D ** -0.5),
        'moe_w_up': nrm((DEPTH, N_EXPERTS, D, D_EXPERT), D ** -0.5),
        'moe_w_down': nrm((DEPTH, N_EXPERTS, D_EXPERT, D), D_EXPERT ** -0.5),
    }


def reference(x_prompt, x_sample, mem_prompt, mem_sample, ln_mix, ln_cross, ln_mem, ln_ffn,
              ln_final, rel_bias, w_in_ab, w_out_ab, hy_conv_w, hy_conv_b, hy_filt_w1, hy_filt_b1,
              hy_filt_w2, hy_filt_b2, hy_filt_w3, hy_filt_freq, hy_d, attn_sink, w_in_cd, w_out_cd,
              s5_a_re, s5_a_im, s5_log_dt, s5_b_re, s5_b_im, s5_c_re, s5_c_im, s5_d, s5_glu_w,
              s5_glu_b, mla_q_norm, mla_w_uq, mla_kv_norm, mla_w_ukv, ca_w_q, ca_w_kv, ca_w_o,
              moe_w_router, moe_w_gate, moe_w_up, moe_w_down):
    p = dict(ln_mix=ln_mix, ln_cross=ln_cross, ln_mem=ln_mem, ln_ffn=ln_ffn, ln_final=ln_final,
             rel_bias=rel_bias, w_in_ab=w_in_ab, w_out_ab=w_out_ab, hy_conv_w=hy_conv_w,
             hy_conv_b=hy_conv_b, hy_filt_w1=hy_filt_w1, hy_filt_b1=hy_filt_b1,
             hy_filt_w2=hy_filt_w2, hy_filt_b2=hy_filt_b2, hy_filt_w3=hy_filt_w3,
             hy_filt_freq=hy_filt_freq, hy_d=hy_d, attn_sink=attn_sink, w_in_cd=w_in_cd,
             w_out_cd=w_out_cd, s5_a_re=s5_a_re, s5_a_im=s5_a_im, s5_log_dt=s5_log_dt,
             s5_b_re=s5_b_re, s5_b_im=s5_b_im, s5_c_re=s5_c_re, s5_c_im=s5_c_im, s5_d=s5_d,
             s5_glu_w=s5_glu_w, s5_glu_b=s5_glu_b, mla_q_norm=mla_q_norm, mla_w_uq=mla_w_uq,
             mla_kv_norm=mla_kv_norm, mla_w_ukv=mla_w_ukv, ca_w_q=ca_w_q, ca_w_kv=ca_w_kv,
             ca_w_o=ca_w_o, moe_w_router=moe_w_router, moe_w_gate=moe_w_gate,
             moe_w_up=moe_w_up, moe_w_down=moe_w_down)
    y_prompt = run_trunk(x_prompt, mem_prompt, p)
    y_sample = run_trunk(x_sample, mem_sample, p)
    return (y_prompt, y_sample)
```

```python
import functools
import math

import jax
import jax.numpy as jnp
import numpy as np
from jax import lax
from jax.experimental import pallas as pl
from jax.experimental.pallas import tpu as pltpu

D_MODEL = 1024
DEPTH = 4
HALF = 512
HEAD_DIM = 64
EPS = 1e-6
NEG = -1e30

HY_WIDTH = HALF
HY_EMB = 33
HY_BANDS = (HY_EMB - 1) // 2
HY_FILT_HIDDEN = 64
HY_DECAY_TARGET = 1e-2
HY_FAST = 0.3
HY_SLOW = 1.5
HY_MIN_DECAY = math.log(HY_DECAY_TARGET) / HY_SLOW
HY_MAX_DECAY = math.log(HY_DECAY_TARGET) / HY_FAST
HY_SHIFT = 0.05

WA_HEADS = 8
WA_KV_HEADS = 2
WA_REP = 4
WA_WINDOW = 128
WA_BLOCK = 128
REL_BUCKETS = 32
REL_MAX_DIST = 128

S5_GROUP = 16
S5_GROUPS = 32
S5_STATE = 64

MLA_HEADS = 8
MLA_NOPE = 64
MLA_ROPE = 32
MLA_V = 64
MLA_Q_RANK = 256
MLA_KV_RANK = 128
ROPE_THETA = 10000.0

CA_HEADS = 4
CA_HEAD_DIM = 128

N_EXPERTS = 16
EC_CAPACITY_FACTOR = 2
D_EXPERT = 2048

AB_IN = 3 * HY_WIDTH + (WA_HEADS + 2 * WA_KV_HEADS) * HEAD_DIM
CD_PAD = 1152

V7X_VMEM_LIMIT_BYTES = 56 * 1024 * 1024
LANES = 128
SUBLANES = 8

BF16 = jnp.bfloat16
F32 = jnp.float32


def _cparams(*sem):
    return pltpu.CompilerParams(dimension_semantics=sem, vmem_limit_bytes=V7X_VMEM_LIMIT_BYTES)


def _dot(a, b):
    return jnp.dot(a, b, preferred_element_type=F32)


def _dot_nt(a, b):
    return lax.dot_general(a, b, (((1,), (1,)), ((), ())), preferred_element_type=F32)


def _rms(xf, g):
    return xf * lax.rsqrt(jnp.mean(xf * xf, axis=-1, keepdims=True) + EPS) * g


def _norm_proj_kernel(x_ref, g_ref, w_ref, *out_refs, splits):
    hn = _rms(x_ref[...].astype(F32), g_ref[...]).astype(BF16)
    for o_ref, (start, width) in zip(out_refs, splits):
        for c0 in range(0, width, 512):
            cw = min(512, width - c0)
            o_ref[:, c0:c0 + cw] = _dot(hn, w_ref[:, start + c0:start + c0 + cw]).astype(o_ref.dtype)


def norm_proj(x2d, gain, w, bsz, seq, outs, tm=512):
    tm = min(tm, seq)
    nl = seq // tm
    n = w.shape[1]
    out_shapes, out_specs, splits = [], [], []
    for start, width, dtype, time_major in outs:
        splits.append((start, width))
        if time_major:
            out_shapes.append(jax.ShapeDtypeStruct((seq, bsz * width), dtype))
            out_specs.append(pl.BlockSpec((tm, width), lambda b, i: (i, b)))
        else:
            out_shapes.append(jax.ShapeDtypeStruct((bsz * seq, width), dtype))
            out_specs.append(pl.BlockSpec((tm, width), lambda b, i, nl=nl: (b * nl + i, 0)))
    return pl.pallas_call(
        functools.partial(_norm_proj_kernel, splits=tuple(splits)),
        grid=(bsz, nl),
        in_specs=[pl.BlockSpec((tm, D_MODEL), lambda b, i, nl=nl: (b * nl + i, 0)),
                  pl.BlockSpec((1, D_MODEL), lambda b, i: (0, 0)),
                  pl.BlockSpec((D_MODEL, n), lambda b, i: (0, 0))],
        out_specs=out_specs,
        out_shape=out_shapes,
        compiler_params=_cparams("parallel", "parallel"),
        name="norm_proj",
    )(x2d, gain.reshape(1, D_MODEL), w)


def _hyena_filter_kernel(z_ref, w1_ref, b1_ref, w2_ref, b2_ref, w3_ref, fr_ref, win_ref, o_ref):
    hp = lax.Precision.HIGHEST
    fr = fr_ref[...]
    h = jnp.sin(fr * (jnp.dot(z_ref[...], w1_ref[...], precision=hp, preferred_element_type=F32) + b1_ref[...]))
    h = jnp.sin(fr * (jnp.dot(h, w2_ref[...], precision=hp, preferred_element_type=F32) + b2_ref[...]))
    h = jnp.dot(h, w3_ref[...], precision=hp, preferred_element_type=F32)
    o_ref[...] = h * win_ref[...]


def hyena_filters(seq, w1, b1, w2, b2, w3, freq):
    t = jnp.linspace(0.0, 1.0, seq, dtype=F32)[:, None]
    ang = 2.0 * math.pi * jnp.arange(seq, dtype=F32)[:, None] / seq
    bands = jnp.linspace(1e-4, HY_BANDS - 1, HY_BANDS, dtype=F32)[None, :]
    z = jnp.concatenate([t, jnp.cos(bands * ang), -jnp.sin(bands * ang)], axis=-1)
    zp = jnp.pad(z, ((0, 0), (0, HY_FILT_HIDDEN - HY_EMB)))
    w1p = jnp.pad(w1.astype(F32), ((0, HY_FILT_HIDDEN - HY_EMB), (0, 0)))
    deltas = jnp.abs(jnp.linspace(HY_MIN_DECAY, HY_MAX_DECAY, HY_WIDTH, dtype=F32))
    window = jnp.exp(-t * deltas[None, :]) + HY_SHIFT
    win2 = jnp.concatenate([window, window], axis=-1)
    tl = min(512, seq)
    hh = HY_FILT_HIDDEN
    full = lambda r, c: pl.BlockSpec((r, c), lambda i: (0, 0))
    return pl.pallas_call(
        _hyena_filter_kernel,
        grid=(seq // tl,),
        in_specs=[pl.BlockSpec((tl, hh), lambda i: (i, 0)), full(hh, hh), full(1, hh), full(hh, hh), full(1, hh),
                  full(hh, 2 * HY_WIDTH), full(1, hh), pl.BlockSpec((tl, 2 * HY_WIDTH), lambda i: (i, 0))],
        out_specs=pl.BlockSpec((tl, 2 * HY_WIDTH), lambda i: (i, 0)),
        out_shape=jax.ShapeDtypeStruct((seq, 2 * HY_WIDTH), F32),
        compiler_params=_cparams("parallel"),
        name="hyena_filter",
    )(zp, w1p, b1.reshape(1, hh), w2, b2.reshape(1, hh), w3, freq.reshape(1, hh), win2)


def dft_matrices(seq):
    n = 2 * seq
    r = jnp.arange(seq, dtype=jnp.int32)[:, None]
    t = jnp.arange(n, dtype=jnp.int32)[None, :]
    ang = ((r * t) % n).astype(F32) * (2.0 * math.pi / n)
    c, s = jnp.cos(ang), jnp.sin(ang)
    nyq = jnp.where(t % 2 == 0, 1.0, -1.0).astype(F32)
    fwd = jnp.concatenate([c, jnp.where(r == 0, nyq, -s)], axis=0)
    ct = c[:, :seq].T
    st = s[:, :seq].T
    r_row = r.T
    inv_r = jnp.where(r_row == 0, 1.0, 2.0 * ct) / n
    inv_s = jnp.where(r_row == 0, nyq[:, :seq].T, -2.0 * st) / n
    inv = jnp.concatenate([inv_r, inv_s], axis=1)
    return fwd.astype(BF16), inv.astype(BF16)


def _kernel_dft_kernel(a_ref, khi_ref, klo_ref, o_ref):
    o_ref[...] = _dot(a_ref[...], khi_ref[...]) + _dot(a_ref[...], klo_ref[...])


def kernel_spectrum(fwd, k):
    n = fwd.shape[0]
    khi = k.astype(BF16)
    klo = (k - khi.astype(F32)).astype(BF16)
    tf = min(256, n)
    return pl.pallas_call(
        _kernel_dft_kernel,
        grid=(n // tf,),
        in_specs=[pl.BlockSpec((tf, n), lambda i: (i, 0)),
                  pl.BlockSpec((n, HY_WIDTH), lambda i: (0, 0)),
                  pl.BlockSpec((n, HY_WIDTH), lambda i: (0, 0))],
        out_specs=pl.BlockSpec((tf, HY_WIDTH), lambda i: (i, 0)),
        out_shape=jax.ShapeDtypeStruct((n, HY_WIDTH), F32),
        compiler_params=_cparams("parallel"),
        name="hyena_kernel_dft",
    )(fwd, khi, klo)


def _shift_down(u):
    rows = lax.broadcasted_iota(jnp.int32, u.shape, 0)
    return jnp.where(rows == 0, 0.0, pltpu.roll(u, 1, 0))


def _shift_up(u):
    n = u.shape[0]
    rows = lax.broadcasted_iota(jnp.int32, u.shape, 0)
    return jnp.where(rows == n - 1, 0.0, pltpu.roll(u, n - 1, 0))


def _hyena_pre_kernel(u_ref, w_ref, b_ref, vx_ref, x0_ref):
    def conv(c0):
        u = u_ref[:, c0:c0 + LANES].astype(F32)
        w = w_ref[:, c0:c0 + LANES]
        return _shift_down(u) * w[0:1] + u * w[1:2] + _shift_up(u) * w[2:3] + b_ref[:, c0:c0 + LANES]

    for c in range(0, HY_WIDTH, LANES):
        x0_ref[:, c:c + LANES] = conv(c).astype(x0_ref.dtype)
        vx_ref[:, c:c + LANES] = (conv(2 * HY_WIDTH + c) * conv(HY_WIDTH + c)).astype(vx_ref.dtype)


def hyena_pre(proj, conv_w, conv_b, bsz, seq):
    w3 = 3 * HY_WIDTH
    out = jax.ShapeDtypeStruct((bsz * seq, HY_WIDTH), BF16)
    return pl.pallas_call(
        _hyena_pre_kernel,
        grid=(bsz,),
        in_specs=[pl.BlockSpec((seq, w3), lambda b: (b, 0)),
                  pl.BlockSpec((3, w3), lambda b: (0, 0)),
                  pl.BlockSpec((1, w3), lambda b: (0, 0))],
        out_specs=[pl.BlockSpec((seq, HY_WIDTH), lambda b: (b, 0))] * 2,
        out_shape=[out, out],
        compiler_params=_cparams("parallel"),
        name="hyena_pre",
    )(proj, conv_w, conv_b.reshape(1, w3))


def _hyena_conv_kernel(vx_ref, x0_ref, ar_ref, as_ref, br_ref, bs_ref, kp_ref, kq_ref, kp2_ref, d_ref,
                       o_ref, acc_ref):
    f = pl.program_id(1)
    vx = vx_ref[...]
    r = _dot(ar_ref[...], vx)
    s = _dot(as_ref[...], vx)
    kq = kq_ref[...]
    zr = (r * kp_ref[...] - s * kq).astype(BF16)
    zs = (r * kq + s * kp2_ref[...]).astype(BF16)
    part = _dot(br_ref[...], zr) + _dot(bs_ref[...], zs)

    @pl.when(f == 0)
    def _():
        acc_ref[...] = part

    @pl.when(f > 0)
    def _():
        acc_ref[...] += part

    @pl.when(f == pl.num_programs(1) - 1)
    def _():
        y = acc_ref[...] + vx.astype(F32) * d_ref[...]
        o_ref[...] = (y * x0_ref[...].astype(F32)).astype(o_ref.dtype)


def hyena_conv(vx, x0, fwd, inv, kp, kq, kp2, d, bsz, seq):
    tf = min(256, seq)
    nf = seq // tf
    w = HY_WIDTH
    return pl.pallas_call(
        _hyena_conv_kernel,
        grid=(bsz, nf),
        in_specs=[pl.BlockSpec((seq, w), lambda b, f: (b, 0)),
                  pl.BlockSpec((seq, w), lambda b, f: (b, 0)),
                  pl.BlockSpec((tf, seq), lambda b, f: (f, 0)),
                  pl.BlockSpec((tf, seq), lambda b, f, nf=nf: (nf + f, 0)),
                  pl.BlockSpec((seq, tf), lambda b, f: (0, f)),
                  pl.BlockSpec((seq, tf), lambda b, f, nf=nf: (0, nf + f)),
                  pl.BlockSpec((tf, w), lambda b, f: (f, 0)),
                  pl.BlockSpec((tf, w), lambda b, f: (f, 0)),
                  pl.BlockSpec((tf, w), lambda b, f: (f, 0)),
                  pl.BlockSpec((1, w), lambda b, f: (0, 0))],
        out_specs=pl.BlockSpec((seq, w), lambda b, f: (b, 0)),
        out_shape=jax.ShapeDtypeStruct((bsz * seq, w), BF16),
        scratch_shapes=[pltpu.VMEM((seq, w), F32)],
        compiler_params=_cparams("parallel", "arbitrary"),
        name="hyena_conv",
    )(vx, x0, fwd, fwd, inv, inv, kp, kq, kp2, d.reshape(1, w))


def _rel_bucket(rel):
    nb = REL_BUCKETS // 2
    max_exact = nb // 2
    ret = (rel > 0).astype(jnp.int32) * nb
    n = jnp.abs(rel)
    nf = jnp.maximum(n, 1).astype(F32)
    large = max_exact + (jnp.log(nf / max_exact) / math.log(REL_MAX_DIST / max_exact)
                         * (nb - max_exact)).astype(jnp.int32)
    large = jnp.minimum(large, nb - 1)
    return ret + jnp.where(n < max_exact, n, large)


def window_bias_mask(rel_bias):
    j = jnp.arange(WA_BLOCK, dtype=jnp.int32)[:, None]
    s = jnp.arange(3 * WA_BLOCK, dtype=jnp.int32)[None, :]
    rel = (s - WA_BLOCK) - j
    bias = jnp.transpose(rel_bias.astype(F32)[_rel_bucket(rel)], (2, 0, 1))
    band = jnp.abs(rel) <= WA_WINDOW
    return jnp.where(band[None], bias, NEG)


def _window_attn_kernel(sink_ref, q_ref, k_ref, v_ref, bias_ref, o_ref):
    nb = q_ref.shape[0] // WA_BLOCK
    scale = HEAD_DIM ** -0.5
    col = lax.broadcasted_iota(jnp.int32, (WA_BLOCK, 3 * WA_BLOCK), 1)

    def block(i, carry):
        ip = jnp.maximum(i - 1, 0)
        inx = jnp.minimum(i + 1, nb - 1)
        rows = lambda j: pl.ds(pl.multiple_of(j * WA_BLOCK, WA_BLOCK), WA_BLOCK)
        qb = q_ref[rows(i), :]
        kslab = jnp.concatenate([k_ref[rows(ip), :], k_ref[rows(i), :], k_ref[rows(inx), :]], axis=0)
        vslab = jnp.concatenate([v_ref[rows(ip), :], v_ref[rows(i), :], v_ref[rows(inx), :]], axis=0)
        lo = jnp.where(i > 0, 0, WA_BLOCK)
        hi = jnp.where(i < nb - 1, 3 * WA_BLOCK, 2 * WA_BLOCK)
        valid = jnp.logical_and(col >= lo, col < hi)
        outs = []
        for h in range(WA_HEADS):
            g = h // WA_REP
            kg = kslab[:, g * HEAD_DIM:(g + 1) * HEAD_DIM]
            vg = vslab[:, g * HEAD_DIM:(g + 1) * HEAD_DIM]
            s = _dot_nt(qb[:, h * HEAD_DIM:(h + 1) * HEAD_DIM], kg) * scale + bias_ref[h]
            s = jnp.where(valid, s, NEG)
            sk = sink_ref[h]
            m = jnp.maximum(jnp.max(s, axis=-1, keepdims=True), sk)
            p = jnp.exp(s - m)
            den = jnp.sum(p, axis=-1, keepdims=True) + jnp.exp(sk - m)
            outs.append(_dot(p.astype(BF16), vg) / den)
        o_ref[rows(i), :] = jnp.concatenate(outs, axis=-1).astype(o_ref.dtype)
        return carry

    lax.fori_loop(0, nb, block, 0)


def window_attn(proj, sink, bias_mask, bsz, seq):
    hq = WA_HEADS * HEAD_DIM
    hkv = WA_KV_HEADS * HEAD_DIM
    q_blk = (3 * HY_WIDTH) // hq
    k_blk = (3 * HY_WIDTH + hq) // hkv
    return pl.pallas_call(
        _window_attn_kernel,
        grid=(bsz,),
        in_specs=[pl.BlockSpec(memory_space=pltpu.SMEM),
                  pl.BlockSpec((seq, hq), lambda b: (b, q_blk)),
                  pl.BlockSpec((seq, hkv), lambda b: (b, k_blk)),
                  pl.BlockSpec((seq, hkv), lambda b: (b, k_blk + 1)),
                  pl.BlockSpec((WA_HEADS, WA_BLOCK, 3 * WA_BLOCK), lambda b: (0, 0, 0))],
        out_specs=pl.BlockSpec((seq, hq), lambda b: (b, 0)),
        out_shape=jax.ShapeDtypeStruct((bsz * seq, hq), BF16),
        compiler_params=_cparams("parallel"),
        name="window_attn",
    )(sink.astype(F32), proj, proj, proj, bias_mask)


def _out_proj_kernel(x_ref, a_ref, b_ref, w_ref, o_ref):
    acc = _dot(a_ref[...], w_ref[0:HALF, :]) + _dot(b_ref[...], w_ref[HALF:, :])
    o_ref[...] = x_ref[...] + acc


def out_proj_ab(x2d, y_a, y_b, w, tm=512):
    t = x2d.shape[0]
    tm = min(tm, t)
    return pl.pallas_call(
        _out_proj_kernel,
        grid=(t // tm,),
        in_specs=[pl.BlockSpec((tm, D_MODEL), lambda i: (i, 0)),
                  pl.BlockSpec((tm, HALF), lambda i: (i, 0)),
                  pl.BlockSpec((tm, HALF), lambda i: (i, 0)),
                  pl.BlockSpec((D_MODEL, D_MODEL), lambda i: (0, 0))],
        out_specs=pl.BlockSpec((tm, D_MODEL), lambda i: (i, 0)),
        out_shape=jax.ShapeDtypeStruct(x2d.shape, F32),
        compiler_params=_cparams("parallel"),
        name="out_proj_ab",
    )(x2d, y_a, y_b, w)


def s5_discretise(a_re, a_im, log_dt, b_re, b_im, c_re, c_im):
    lam = lax.complex(a_re.astype(F32), a_im.astype(F32))
    dt = jnp.exp(log_dt.astype(F32))[..., None]
    abar = jnp.exp(lam * dt)
    bmat = lax.complex(b_re.astype(F32), b_im.astype(F32))
    bbar = ((abar - 1.0) / lam)[..., None] * bmat
    cmat = lax.complex(c_re.astype(F32), c_im.astype(F32))
    nj, gl = S5_GROUPS // SUBLANES, SUBLANES
    eye = jnp.eye(gl, dtype=F32)
    a5 = jnp.stack([abar.real, abar.imag], axis=1).reshape(2, 2, nj, 1, gl * S5_STATE)
    a5 = jnp.transpose(a5, (0, 2, 1, 3, 4))

    def pack_b(x):
        x = x.reshape(2, nj, gl, S5_STATE, S5_GROUP)
        y = jnp.einsum('hg,djgpc->djhcgp', eye, x)
        return y.reshape(2, nj, gl * S5_GROUP, gl * S5_STATE)

    def pack_c(x):
        x = x.reshape(2, nj, gl, S5_GROUP, S5_STATE)
        y = jnp.einsum('hg,djgcp->djgphc', eye, x)
        return y.reshape(2, nj, gl * S5_STATE, gl * S5_GROUP)

    bm = jnp.concatenate([pack_b(bbar.real), pack_b(bbar.imag)], axis=-1).astype(BF16)
    cm = jnp.concatenate([pack_c(cmat.real), -pack_c(cmat.imag)], axis=-2).astype(BF16)
    return a5, bm, cm


def _s5_kernel(u_ref, a_ref, b_ref, c_ref, y_ref, buf_ref, h_ref, *, chunk):
    d = pl.program_id(1)
    nj = S5_GROUPS // SUBLANES
    sw = SUBLANES * S5_STATE
    rows = chunk * SUBLANES

    @pl.when(pl.program_id(2) == 0)
    def _():
        h_ref[...] = jnp.zeros_like(h_ref)

    u = u_ref[...].reshape(rows, HALF).astype(BF16)
    for j in range(nj):
        buf_ref[j] = _dot(u[:, j * LANES:(j + 1) * LANES], b_ref[0, j])

    def step(s, carry):
        t = jnp.where(d == 0, s, chunk - 1 - s)
        r0 = pl.multiple_of(t * SUBLANES, SUBLANES)
        new = []
        for j in range(nj):
            hr, hi = carry[2 * j], carry[2 * j + 1]
            ar = a_ref[0, j, 0]
            ai = a_ref[0, j, 1]
            br = buf_ref[j, pl.ds(r0, SUBLANES), 0:sw]
            bi = buf_ref[j, pl.ds(r0, SUBLANES), sw:2 * sw]
            nr = ar * hr - ai * hi + br
            ni = ar * hi + ai * hr + bi
            buf_ref[j, pl.ds(r0, SUBLANES), 0:sw] = nr
            buf_ref[j, pl.ds(r0, SUBLANES), sw:2 * sw] = ni
            new += [nr, ni]
        return tuple(new)

    init = tuple(h_ref[k] for k in range(2 * nj))
    fin = lax.fori_loop(0, chunk, step, init)
    for k in range(2 * nj):
        h_ref[k] = fin[k]

    for j in range(nj):
        yj = _dot(buf_ref[j].astype(BF16), c_ref[0, j])
        y_ref[0, :, :, j * LANES:(j + 1) * LANES] = yj.reshape(chunk, SUBLANES, LANES).astype(y_ref.dtype)


def s5_scan(u_tm, a5, bm, cm, bsz, seq, chunk=64):
    chunk = min(chunk, seq)
    nc = seq // chunk
    nj = S5_GROUPS // SUBLANES
    sw = SUBLANES * S5_STATE

    def tchunk(d, i):
        return i + d * (nc - 1 - 2 * i)

    return pl.pallas_call(
        functools.partial(_s5_kernel, chunk=chunk),
        grid=(bsz // SUBLANES, 2, nc),
        in_specs=[pl.BlockSpec((chunk, SUBLANES, HALF), lambda b, d, i: (tchunk(d, i), b, 0)),
                  pl.BlockSpec((1, nj, 2, 1, sw), lambda b, d, i: (d, 0, 0, 0, 0)),
                  pl.BlockSpec((1, nj, LANES, 2 * sw), lambda b, d, i: (d, 0, 0, 0)),
                  pl.BlockSpec((1, nj, 2 * sw, LANES), lambda b, d, i: (d, 0, 0, 0))],
        out_specs=pl.BlockSpec((1, chunk, SUBLANES, HALF), lambda b, d, i: (d, tchunk(d, i), b, 0)),
        out_shape=jax.ShapeDtypeStruct((2, seq, bsz, HALF), F32),
        scratch_shapes=[pltpu.VMEM((nj, chunk * SUBLANES, 2 * sw), F32),
                        pltpu.VMEM((2 * nj, SUBLANES, sw), F32)],
        compiler_params=_cparams("parallel", "arbitrary", "arbitrary"),
        name="s5_scan",
    )(u_tm, a5, bm, cm)


def _gelu_tanh(x):
    return 0.5 * x * (1.0 + jnp.tanh(math.sqrt(2.0 / math.pi) * (x + 0.044715 * (x * x * x))))


def _out_proj_cd_kernel(x_ref, u_ref, yf_ref, yb_ref, mla_ref, d_ref, gw_ref, gb_ref, w_ref, o_ref):
    y = u_ref[...] * d_ref[...] + yf_ref[0] + yb_ref[0]
    g = _gelu_tanh(y)
    z = _dot(g.astype(BF16), gw_ref[...]) + gb_ref[...]
    y_s5 = g * jax.nn.sigmoid(z)
    acc = _dot(y_s5.astype(BF16), w_ref[0:HALF, :]) + _dot(mla_ref[...], w_ref[HALF:, :])
    o_ref[...] = x_ref[...] + acc


def out_proj_cd(x2d, u_tm2, y_tm2, y_mla, d, glu_w, glu_b, w, bsz, seq, tm=512):
    tm = min(tm, seq)
    nl = seq // tm
    row = lambda b, i: (b * nl + i, 0)
    const = lambda b, i: (0, 0)
    return pl.pallas_call(
        _out_proj_cd_kernel,
        grid=(bsz, nl),
        in_specs=[pl.BlockSpec((tm, D_MODEL), row),
                  pl.BlockSpec((tm, HALF), lambda b, i: (i, b)),
                  pl.BlockSpec((1, tm, HALF), lambda b, i: (0, i, b)),
                  pl.BlockSpec((1, tm, HALF), lambda b, i: (1, i, b)),
                  pl.BlockSpec((tm, HALF), row),
                  pl.BlockSpec((1, HALF), const),
                  pl.BlockSpec((HALF, HALF), const),
                  pl.BlockSpec((1, HALF), const),
                  pl.BlockSpec((D_MODEL, D_MODEL), const)],
        out_specs=pl.BlockSpec((tm, D_MODEL), row),
        out_shape=jax.ShapeDtypeStruct(x2d.shape, F32),
        compiler_params=_cparams("parallel", "parallel"),
        name="out_proj_cd",
    )(x2d, u_tm2, y_tm2, y_tm2, y_mla, d.reshape(1, HALF), glu_w, glu_b.reshape(1, HALF), w)


MLA_HP = 128


def mla_weights(w_uq, w_ukv):
    rq = w_uq.shape[0]
    wq = w_uq.astype(F32).reshape(rq, MLA_HEADS, MLA_NOPE + MLA_ROPE)
    half = MLA_ROPE // 2
    x1, x2 = wq[..., MLA_NOPE:MLA_NOPE + half], wq[..., MLA_NOPE + half:]
    zpad = jnp.zeros((rq, MLA_HEADS, MLA_HP - MLA_NOPE - MLA_ROPE), F32)
    wq1 = jnp.concatenate([wq, zpad], axis=-1).reshape(rq, MLA_HEADS * MLA_HP)
    wq2 = jnp.concatenate([jnp.zeros((rq, MLA_HEADS, MLA_NOPE), F32), -x2, x1, zpad], axis=-1)
    wq2 = wq2.reshape(rq, MLA_HEADS * MLA_HP)
    rk = w_ukv.shape[0]
    wkv = w_ukv.astype(F32).reshape(rk, MLA_HEADS, MLA_NOPE + MLA_V)
    wk = jnp.concatenate([wkv[..., :MLA_NOPE], jnp.zeros((rk, MLA_HEADS, MLA_HP - MLA_NOPE), F32)], axis=-1)
    wk = wk.reshape(rk, MLA_HEADS * MLA_HP)
    wv = wkv[..., MLA_NOPE:].reshape(rk, MLA_HEADS * MLA_V)
    return wq1.astype(BF16), wq2.astype(BF16), wk.astype(BF16), wv.astype(BF16)


def rope_tables(seq):
    inv = 1.0 / (ROPE_THETA ** (jnp.arange(0, MLA_ROPE, 2, dtype=F32) / MLA_ROPE))
    ang = jnp.arange(seq, dtype=F32)[:, None] * inv[None, :]
    c, s = jnp.cos(ang), jnp.sin(ang)
    ones = jnp.ones((seq, MLA_NOPE), F32)
    zpad = jnp.zeros((seq, MLA_HP - MLA_NOPE - MLA_ROPE), F32)
    cos_t = jnp.concatenate([ones, c, c, zpad], axis=-1)
    sin_t = jnp.concatenate([0.0 * ones, s, s, zpad], axis=-1)
    return cos_t, sin_t


def _mla_prep_kernel(r_ref, qg_ref, kg_ref, wq1_ref, wq2_ref, wk_ref, wv_ref, cos_ref, sin_ref,
                     q_ref, k_ref, v_ref):
    scale = (MLA_NOPE + MLA_ROPE) ** -0.5
    cq = _rms(r_ref[:, 0:MLA_Q_RANK].astype(F32), qg_ref[...]).astype(BF16)
    o1 = MLA_Q_RANK + MLA_KV_RANK
    ckv = _rms(r_ref[:, MLA_Q_RANK:o1].astype(F32), kg_ref[...]).astype(BF16)
    cos_t, sin_t = cos_ref[...], sin_ref[...]
    kr = r_ref[:, o1:o1 + LANES].astype(F32) * cos_t + r_ref[:, o1 + LANES:o1 + 2 * LANES].astype(F32) * sin_t
    v_ref[...] = _dot(ckv, wv_ref[...]).astype(v_ref.dtype)
    for h in range(MLA_HEADS):
        sl = slice(h * MLA_HP, (h + 1) * MLA_HP)
        qh = _dot(cq, wq1_ref[:, sl]) * cos_t + _dot(cq, wq2_ref[:, sl]) * sin_t
        q_ref[:, sl] = (qh * scale).astype(q_ref.dtype)
        k_ref[:, sl] = (_dot(ckv, wk_ref[:, sl]) + kr).astype(k_ref.dtype)


def mla_prep(rest, q_norm, kv_norm, wq1, wq2, wk, wv, cos_t, sin_t, bsz, seq, tm=512):
    tm = min(tm, seq)
    nl = seq // tm
    t = bsz * seq
    wr = rest.shape[1]
    row = lambda b, i: (b * nl + i, 0)
    const = lambda b, i: (0, 0)
    qk = MLA_HEADS * MLA_HP
    return pl.pallas_call(
        _mla_prep_kernel,
        grid=(bsz, nl),
        in_specs=[pl.BlockSpec((tm, wr), row),
                  pl.BlockSpec((1, MLA_Q_RANK), const),
                  pl.BlockSpec((1, MLA_KV_RANK), const),
                  pl.BlockSpec((MLA_Q_RANK, qk), const),
                  pl.BlockSpec((MLA_Q_RANK, qk), const),
                  pl.BlockSpec((MLA_KV_RANK, qk), const),
                  pl.BlockSpec((MLA_KV_RANK, MLA_HEADS * MLA_V), const),
                  pl.BlockSpec((tm, MLA_HP), lambda b, i: (i, 0)),
                  pl.BlockSpec((tm, MLA_HP), lambda b, i: (i, 0))],
        out_specs=[pl.BlockSpec((tm, qk), row), pl.BlockSpec((tm, qk), row),
                   pl.BlockSpec((tm, MLA_HEADS * MLA_V), row)],
        out_shape=[jax.ShapeDtypeStruct((t, qk), BF16), jax.ShapeDtypeStruct((t, qk), BF16),
                   jax.ShapeDtypeStruct((t, MLA_HEADS * MLA_V), BF16)],
        compiler_params=_cparams("parallel", "parallel"),
        name="mla_prep",
    )(rest, q_norm.reshape(1, -1), kv_norm.reshape(1, -1), wq1, wq2, wk, wv, cos_t, sin_t)


def _mla_attn_kernel(q_ref, k_ref, v_ref, o_ref):
    v = v_ref[...]
    outs = []
    for h in range(2):
        sl = slice(h * MLA_HP, (h + 1) * MLA_HP)
        s = _dot_nt(q_ref[:, sl], k_ref[:, sl])
        m = jnp.max(s, axis=-1, keepdims=True)
        p = jnp.exp(s - m)
        den = jnp.sum(p, axis=-1, keepdims=True)
        outs.append(_dot(p.astype(BF16), v) / den)
    lane = lax.broadcasted_iota(jnp.int32, outs[0].shape, 1)
    o_ref[...] = jnp.where(lane < MLA_V, outs[0], outs[1]).astype(o_ref.dtype)


def mla_attn(q, k, v, bsz, seq, tq=256):
    tq = min(tq, seq)
    nq = seq // tq
    return pl.pallas_call(
        _mla_attn_kernel,
        grid=(bsz, MLA_HEADS // 2, nq),
        in_specs=[pl.BlockSpec((tq, 2 * MLA_HP), lambda b, p, i: (b * nq + i, p)),
                  pl.BlockSpec((seq, 2 * MLA_HP), lambda b, p, i: (b, p)),
                  pl.BlockSpec((seq, 2 * MLA_V), lambda b, p, i: (b, p))],
        out_specs=pl.BlockSpec((tq, 2 * MLA_V), lambda b, p, i: (b * nq + i, p)),
        out_shape=jax.ShapeDtypeStruct((bsz * seq, MLA_HEADS * MLA_V), BF16),
        compiler_params=_cparams("parallel", "parallel", "parallel"),
        name="mla_attn",
    )(q, k, v)


def _cross_router_kernel(x_ref, kv_ref, gc_ref, wq_ref, wo_ref, gf_ref, wr_ref, xo_ref, hn_ref, aff_ref):
    x = x_ref[...]
    h = _rms(x, gc_ref[...]).astype(BF16)
    q = (_dot(h, wq_ref[...]) * (CA_HEAD_DIM ** -0.5)).astype(BF16)
    hd = CA_HEADS * CA_HEAD_DIM
    outs = []
    for a in range(CA_HEADS):
        sl = slice(a * CA_HEAD_DIM, (a + 1) * CA_HEAD_DIM)
        s = _dot_nt(q[:, sl], kv_ref[:, sl])
        m = jnp.max(s, axis=-1, keepdims=True)
        p = jnp.exp(s - m)
        den = jnp.sum(p, axis=-1, keepdims=True)
        outs.append((_dot(p.astype(BF16), kv_ref[:, hd + a * CA_HEAD_DIM:hd + (a + 1) * CA_HEAD_DIM]) / den))
    o = jnp.concatenate(outs, axis=-1).astype(BF16)
    xn = x + _dot(o, wo_ref[...])
    xo_ref[...] = xn
    hf = _rms(xn, gf_ref[...])
    hb = hf.astype(BF16)
    hn_ref[...] = hb
    lo = (hf - hb.astype(F32)).astype(BF16)
    logits = _dot(hb, wr_ref[0]) + (_dot(lo, wr_ref[0]) + _dot(hb, wr_ref[1]))
    lane = lax.broadcasted_iota(jnp.int32, logits.shape, 1)
    logits = jnp.where(lane < N_EXPERTS, logits, NEG)
    m = jnp.max(logits, axis=-1, keepdims=True)
    e = jnp.exp(logits - m)
    aff = e / jnp.sum(e, axis=-1, keepdims=True)
    aff_ref[...] = aff[:, 0:N_EXPERTS]


def cross_router(x2d, kv, ln_cross, w_q, w_o, ln_ffn, w_router, bsz, seq, tq=256):
    tq = min(tq, seq)
    nq = seq // tq
    t = bsz * seq
    mem = kv.shape[0] // bsz
    hd = CA_HEADS * CA_HEAD_DIM
    wr = jnp.pad(w_router.astype(F32), ((0, 0), (0, LANES - N_EXPERTS)))
    wr_hi = wr.astype(BF16)
    wr2 = jnp.stack([wr_hi, (wr - wr_hi.astype(F32)).astype(BF16)])
    row = lambda b, i: (b * nq + i, 0)
    const = lambda b, i: (0, 0)
    return pl.pallas_call(
        _cross_router_kernel,
        grid=(bsz, nq),
        in_specs=[pl.BlockSpec((tq, D_MODEL), row),
                  pl.BlockSpec((mem, 2 * hd), lambda b, i: (b, 0)),
                  pl.BlockSpec((1, D_MODEL), const),
                  pl.BlockSpec((D_MODEL, hd), const),
                  pl.BlockSpec((hd, D_MODEL), const),
                  pl.BlockSpec((1, D_MODEL), const),
                  pl.BlockSpec((2, D_MODEL, LANES), lambda b, i: (0, 0, 0))],
        out_specs=[pl.BlockSpec((tq, D_MODEL), row), pl.BlockSpec((tq, D_MODEL), row),
                   pl.BlockSpec((tq, N_EXPERTS), row)],
        out_shape=[jax.ShapeDtypeStruct((t, D_MODEL), F32), jax.ShapeDtypeStruct((t, D_MODEL), BF16),
                   jax.ShapeDtypeStruct((t, N_EXPERTS), F32)],
        compiler_params=_cparams("parallel", "parallel"),
        name="cross_router",
    )(x2d, kv, ln_cross.reshape(1, -1), w_q, w_o, ln_ffn.reshape(1, -1), wr2)


def _expert_ffn_kernel(x_ref, g_ref, wg_ref, wu_ref, wd_ref, o_ref, acc_ref):
    f = pl.program_id(2)
    x = x_ref[0]
    a = _dot(x, wg_ref[0])
    u = _dot(x, wu_ref[0])
    hid = (a * jax.nn.sigmoid(a) * u).astype(BF16)
    part = _dot(hid, wd_ref[0])

    @pl.when(f == 0)
    def _():
        acc_ref[...] = part

    @pl.when(f > 0)
    def _():
        acc_ref[...] += part

    @pl.when(f == pl.num_programs(2) - 1)
    def _():
        o_ref[0] = acc_ref[...] * g_ref[0]


def expert_ffn(xe, gate, w_gate, w_up, w_down, tm=1024, tf=512):
    e, cap, _ = xe.shape
    tm = min(tm, cap)
    return pl.pallas_call(
        _expert_ffn_kernel,
        grid=(e, cap // tm, D_EXPERT // tf),
        in_specs=[pl.BlockSpec((1, tm, D_MODEL), lambda e, m, f: (e, m, 0)),
                  pl.BlockSpec((1, tm, 1), lambda e, m, f: (e, m, 0)),
                  pl.BlockSpec((1, D_MODEL, tf), lambda e, m, f: (e, 0, f)),
                  pl.BlockSpec((1, D_MODEL, tf), lambda e, m, f: (e, 0, f)),
                  pl.BlockSpec((1, tf, D_MODEL), lambda e, m, f: (e, f, 0))],
        out_specs=pl.BlockSpec((1, tm, D_MODEL), lambda e, m, f: (e, m, 0)),
        out_shape=jax.ShapeDtypeStruct((e, cap, D_MODEL), F32),
        scratch_shapes=[pltpu.VMEM((tm, D_MODEL), F32)],
        compiler_params=_cparams("parallel", "parallel", "arbitrary"),
        name="expert_ffn",
    )(xe, gate, w_gate, w_up, w_down)


def _final_norm_kernel(x_ref, g_ref, o_ref):
    o_ref[...] = _rms(x_ref[...], g_ref[...])


def final_norm(x2d, g, tm=1024):
    t = x2d.shape[0]
    tm = min(tm, t)
    return pl.pallas_call(
        _final_norm_kernel,
        grid=(t // tm,),
        in_specs=[pl.BlockSpec((tm, D_MODEL), lambda i: (i, 0)), pl.BlockSpec((1, D_MODEL), lambda i: (0, 0))],
        out_specs=pl.BlockSpec((tm, D_MODEL), lambda i: (i, 0)),
        out_shape=jax.ShapeDtypeStruct(x2d.shape, F32),
        compiler_params=_cparams("parallel"),
        name="final_norm",
    )(x2d, g.reshape(1, -1))


def mixer_ab(x2d, e, p, shared, bsz, seq):
    proj = norm_proj(x2d, p['ln_mix_l'], p['w_in_ab'][e], bsz, seq, [(0, AB_IN, BF16, False)])[0]
    vx, x0 = hyena_pre(proj, p['hy_conv_w'][e], p['hy_conv_b'][e], bsz, seq)
    kp, kq, kp2 = shared['hy_spec'][e]
    y_hy = hyena_conv(vx, x0, shared['dft_fwd'], shared['dft_inv'], kp, kq, kp2, p['hy_d'][e], bsz, seq)
    y_wa = window_attn(proj, p['attn_sink'][e], shared['wa_bias'], bsz, seq)
    return out_proj_ab(x2d, y_hy, y_wa, p['w_out_ab'][e])


def mixer_cd(x2d, o, p, shared, bsz, seq):
    u_tm2, rest = norm_proj(x2d, p['ln_mix_l'], shared['w_in_cd'][o], bsz, seq,
                            [(0, HALF, F32, True), (HALF, CD_PAD - HALF, BF16, False)])
    a5, bm, cm = shared['s5'][o]
    y_tm = s5_scan(u_tm2.reshape(seq, bsz, HALF), a5, bm, cm, bsz, seq)
    wq1, wq2, wk, wv = shared['mla_w'][o]
    q, k, v = mla_prep(rest, p['mla_q_norm'][o], p['mla_kv_norm'][o], wq1, wq2, wk, wv,
                       shared['rope_cos'], shared['rope_sin'], bsz, seq)
    y_mla = mla_attn(q, k, v, bsz, seq)
    return out_proj_cd(x2d, u_tm2, y_tm.reshape(2, seq, bsz * HALF), y_mla, p['s5_d'][o],
                       p['s5_glu_w'][o], p['s5_glu_b'][o], p['w_out_cd'][o], bsz, seq)


def ec_moe(x2d, hn, aff, w_gate, w_up, w_down):
    t = x2d.shape[0]
    cap = EC_CAPACITY_FACTOR * t // N_EXPERTS
    gate, idx = lax.top_k(aff.T, cap)
    xe = hn[idx]
    ye = expert_ffn(xe, gate[..., None], w_gate, w_up, w_down)
    return x2d.at[idx.reshape(-1)].add(ye.reshape(-1, D_MODEL))


def prepare_shared(p, seq):
    sh = {}
    sh['wa_bias'] = window_bias_mask(p['rel_bias'])
    sh['dft_fwd'], sh['dft_inv'] = dft_matrices(seq)
    sh['rope_cos'], sh['rope_sin'] = rope_tables(seq)
    specs = []
    for e in range(p['w_in_ab'].shape[0]):
        h = hyena_filters(seq, p['hy_filt_w1'][e], p['hy_filt_b1'][e], p['hy_filt_w2'][e], p['hy_filt_b2'][e],
                          p['hy_filt_w3'][e], p['hy_filt_freq'][e])
        h_fwd, h_bwd = h[:, :HY_WIDTH], h[:, HY_WIDTH:]
        k = jnp.concatenate([h_fwd, jnp.zeros_like(h_fwd[:1]), h_bwd[:0:-1]], axis=0)
        kf = kernel_spectrum(sh['dft_fwd'], k)
        k_r, k_s = kf[:seq], kf[seq:]
        specs.append((k_r, k_s.at[0].set(0.0), k_r.at[0].set(k_s[0])))
    sh['hy_spec'] = specs
    s5, mla_w, w_in_cd = [], [], []
    for o in range(p['w_in_cd'].shape[0]):
        s5.append(s5_discretise(p['s5_a_re'][o], p['s5_a_im'][o], p['s5_log_dt'][o], p['s5_b_re'][o],
                                p['s5_b_im'][o], p['s5_c_re'][o], p['s5_c_im'][o]))
        mla_w.append(mla_weights(p['mla_w_uq'][o], p['mla_w_ukv'][o]))
        w = p['w_in_cd'][o].astype(F32)
        o2 = HALF + MLA_Q_RANK + MLA_KV_RANK
        kr = w[:, o2:o2 + MLA_ROPE]
        half = MLA_ROPE // 2
        kr_rot = jnp.concatenate([-kr[:, half:], kr[:, :half]], axis=1)
        z64 = jnp.zeros((D_MODEL, MLA_NOPE), F32)
        z32 = jnp.zeros((D_MODEL, MLA_HP - MLA_NOPE - MLA_ROPE), F32)
        w_in_cd.append(jnp.concatenate([w[:, :o2], z64, kr, z32, z64, kr_rot, z32], axis=1).astype(BF16))
    sh['s5'], sh['mla_w'], sh['w_in_cd'] = s5, mla_w, w_in_cd
    return sh


def run_trunk(x, mem, p, shared):
    bsz, seq, _ = x.shape
    x2d = x.reshape(bsz * seq, D_MODEL)
    mem2d = mem.reshape(bsz * mem.shape[1], D_MODEL)
    for layer in range(DEPTH):
        pl_ = dict(p, ln_mix_l=p['ln_mix'][layer])
        if layer % 2 == 0:
            x2d = mixer_ab(x2d, layer // 2, pl_, shared, bsz, seq)
        else:
            x2d = mixer_cd(x2d, layer // 2, pl_, shared, bsz, seq)
        kv = norm_proj(mem2d, p['ln_mem'][layer], p['ca_w_kv'][layer], bsz, mem.shape[1],
                       [(0, 2 * CA_HEADS * CA_HEAD_DIM, BF16, False)])[0]
        x2d, hn, aff = cross_router(x2d, kv, p['ln_cross'][layer], p['ca_w_q'][layer], p['ca_w_o'][layer],
                                    p['ln_ffn'][layer], p['moe_w_router'][layer], bsz, seq)
        x2d = ec_moe(x2d, hn, aff, p['moe_w_gate'][layer], p['moe_w_up'][layer], p['moe_w_down'][layer])
    return final_norm(x2d, p['ln_final']).reshape(bsz, seq, D_MODEL)


_BF16_WEIGHTS = ('w_in_ab', 'w_out_ab', 'w_out_cd', 's5_glu_w', 'ca_w_q', 'ca_w_kv', 'ca_w_o',
                 'moe_w_gate', 'moe_w_up', 'moe_w_down')


def kernel(x_prompt, x_sample, mem_prompt, mem_sample, ln_mix, ln_cross, ln_mem, ln_ffn, ln_final, rel_bias, w_in_ab, w_out_ab, hy_conv_w, hy_conv_b, hy_filt_w1, hy_filt_b1, hy_filt_w2, hy_filt_b2, hy_filt_w3, hy_filt_freq, hy_d, attn_sink, w_in_cd, w_out_cd, s5_a_re, s5_a_im, s5_log_dt, s5_b_re, s5_b_im, s5_c_re, s5_c_im, s5_d, s5_glu_w, s5_glu_b, mla_q_norm, mla_w_uq, mla_kv_norm, mla_w_ukv, ca_w_q, ca_w_kv, ca_w_o, moe_w_router, moe_w_gate, moe_w_up, moe_w_down):
    p = dict(ln_mix=ln_mix, ln_cross=ln_cross, ln_mem=ln_mem, ln_ffn=ln_ffn, ln_final=ln_final,
             rel_bias=rel_bias, w_in_ab=w_in_ab, w_out_ab=w_out_ab, hy_conv_w=hy_conv_w,
             hy_conv_b=hy_conv_b, hy_filt_w1=hy_filt_w1, hy_filt_b1=hy_filt_b1,
             hy_filt_w2=hy_filt_w2, hy_filt_b2=hy_filt_b2, hy_filt_w3=hy_filt_w3,
             hy_filt_freq=hy_filt_freq, hy_d=hy_d, attn_sink=attn_sink, w_in_cd=w_in_cd,
             w_out_cd=w_out_cd, s5_a_re=s5_a_re, s5_a_im=s5_a_im, s5_log_dt=s5_log_dt,
             s5_b_re=s5_b_re, s5_b_im=s5_b_im, s5_c_re=s5_c_re, s5_c_im=s5_c_im, s5_d=s5_d,
             s5_glu_w=s5_glu_w, s5_glu_b=s5_glu_b, mla_q_norm=mla_q_norm, mla_w_uq=mla_w_uq,
             mla_kv_norm=mla_kv_norm, mla_w_ukv=mla_w_ukv, ca_w_q=ca_w_q, ca_w_kv=ca_w_kv,
             ca_w_o=ca_w_o, moe_w_router=moe_w_router, moe_w_gate=moe_w_gate,
             moe_w_up=moe_w_up, moe_w_down=moe_w_down)
    assert x_prompt.shape[1] == x_sample.shape[1]
    shared = prepare_shared(p, x_prompt.shape[1])
    for name in _BF16_WEIGHTS:
        p[name] = p[name].astype(BF16)
    y_prompt = run_trunk(x_prompt, mem_prompt, p, shared)
    y_sample = run_trunk(x_sample, mem_sample, p, shared)
    return (y_prompt, y_sample)
```

```python
import functools
import math

import jax
import jax.numpy as jnp
import numpy as np
from jax import lax
from jax.experimental import pallas as pl
from jax.experimental.pallas import tpu as pltpu

D_MODEL = 1024
DEPTH = 4
HALF = 512
HEAD_DIM = 64
EPS = 1e-6
NEG = -1e30

HY_WIDTH = HALF
HY_EMB = 33
HY_BANDS = (HY_EMB - 1) // 2
HY_FILT_HIDDEN = 64
HY_DECAY_TARGET = 1e-2
HY_FAST = 0.3
HY_SLOW = 1.5
HY_MIN_DECAY = math.log(HY_DECAY_TARGET) / HY_SLOW
HY_MAX_DECAY = math.log(HY_DECAY_TARGET) / HY_FAST
HY_SHIFT = 0.05

WA_HEADS = 8
WA_KV_HEADS = 2
WA_REP = 4
WA_WINDOW = 128
WA_BLOCK = 128
REL_BUCKETS = 32
REL_MAX_DIST = 128

S5_GROUP = 16
S5_GROUPS = 32
S5_STATE = 64

MLA_HEADS = 8
MLA_NOPE = 64
MLA_ROPE = 32
MLA_V = 64
MLA_Q_RANK = 256
MLA_KV_RANK = 128
ROPE_THETA = 10000.0

CA_HEADS = 4
CA_HEAD_DIM = 128

N_EXPERTS = 16
EC_CAPACITY_FACTOR = 2
D_EXPERT = 2048

AB_IN = 3 * HY_WIDTH + (WA_HEADS + 2 * WA_KV_HEADS) * HEAD_DIM
CD_PAD = 1152

V7X_VMEM_LIMIT_BYTES = 56 * 1024 * 1024
LANES = 128
SUBLANES = 8
LOG2E = math.log2(math.e)

BF16 = jnp.bfloat16
F32 = jnp.float32


def _cparams(*sem):
    return pltpu.CompilerParams(dimension_semantics=sem, vmem_limit_bytes=V7X_VMEM_LIMIT_BYTES)


def _dot(a, b):
    return jnp.dot(a, b, preferred_element_type=F32)


def _dot_nt(a, b):
    return lax.dot_general(a, b, (((1,), (1,)), ((), ())), preferred_element_type=F32)


def _rms(xf, g):
    return xf * lax.rsqrt(jnp.mean(xf * xf, axis=-1, keepdims=True) + EPS) * g


def _norm_proj_kernel(x_ref, g_ref, w_ref, *out_refs, splits):
    hn = _rms(x_ref[...].astype(F32), g_ref[...]).astype(BF16)
    for o_ref, (start, width) in zip(out_refs, splits):
        for c0 in range(0, width, 512):
            cw = min(512, width - c0)
            o_ref[:, c0:c0 + cw] = _dot(hn, w_ref[:, start + c0:start + c0 + cw]).astype(o_ref.dtype)


def norm_proj(x2d, gain, w, bsz, seq, outs, tm=512):
    tm = min(tm, seq)
    nl = seq // tm
    n = w.shape[1]
    out_shapes, out_specs, splits = [], [], []
    for start, width, dtype, time_major in outs:
        splits.append((start, width))
        if time_major:
            out_shapes.append(jax.ShapeDtypeStruct((seq, bsz * width), dtype))
            out_specs.append(pl.BlockSpec((tm, width), lambda b, i: (i, b)))
        else:
            out_shapes.append(jax.ShapeDtypeStruct((bsz * seq, width), dtype))
            out_specs.append(pl.BlockSpec((tm, width), lambda b, i, nl=nl: (b * nl + i, 0)))
    return pl.pallas_call(
        functools.partial(_norm_proj_kernel, splits=tuple(splits)),
        grid=(bsz, nl),
        in_specs=[pl.BlockSpec((tm, D_MODEL), lambda b, i, nl=nl: (b * nl + i, 0)),
                  pl.BlockSpec((1, D_MODEL), lambda b, i: (0, 0)),
                  pl.BlockSpec((D_MODEL, n), lambda b, i: (0, 0))],
        out_specs=out_specs,
        out_shape=out_shapes,
        compiler_params=_cparams("parallel", "parallel"),
        name="norm_proj",
    )(x2d, gain.reshape(1, D_MODEL), w)


def _hyena_filter_kernel(z_ref, w1_ref, b1_ref, w2_ref, b2_ref, w3_ref, fr_ref, win_ref, o_ref):
    hp = lax.Precision.HIGHEST
    fr = fr_ref[...]
    h = jnp.sin(fr * (jnp.dot(z_ref[...], w1_ref[...], precision=hp, preferred_element_type=F32) + b1_ref[...]))
    h = jnp.sin(fr * (jnp.dot(h, w2_ref[...], precision=hp, preferred_element_type=F32) + b2_ref[...]))
    h = jnp.dot(h, w3_ref[...], precision=hp, preferred_element_type=F32)
    o_ref[...] = h * win_ref[...]


def hyena_filters(seq, w1, b1, w2, b2, w3, freq):
    t = jnp.linspace(0.0, 1.0, seq, dtype=F32)[:, None]
    ang = 2.0 * math.pi * jnp.arange(seq, dtype=F32)[:, None] / seq
    bands = jnp.linspace(1e-4, HY_BANDS - 1, HY_BANDS, dtype=F32)[None, :]
    z = jnp.concatenate([t, jnp.cos(bands * ang), -jnp.sin(bands * ang)], axis=-1)
    zp = jnp.pad(z, ((0, 0), (0, HY_FILT_HIDDEN - HY_EMB)))
    w1p = jnp.pad(w1.astype(F32), ((0, HY_FILT_HIDDEN - HY_EMB), (0, 0)))
    deltas = jnp.abs(jnp.linspace(HY_MIN_DECAY, HY_MAX_DECAY, HY_WIDTH, dtype=F32))
    window = jnp.exp(-t * deltas[None, :]) + HY_SHIFT
    win2 = jnp.concatenate([window, window], axis=-1)
    tl = min(512, seq)
    hh = HY_FILT_HIDDEN
    full = lambda r, c: pl.BlockSpec((r, c), lambda i: (0, 0))
    return pl.pallas_call(
        _hyena_filter_kernel,
        grid=(seq // tl,),
        in_specs=[pl.BlockSpec((tl, hh), lambda i: (i, 0)), full(hh, hh), full(1, hh), full(hh, hh), full(1, hh),
                  full(hh, 2 * HY_WIDTH), full(1, hh), pl.BlockSpec((tl, 2 * HY_WIDTH), lambda i: (i, 0))],
        out_specs=pl.BlockSpec((tl, 2 * HY_WIDTH), lambda i: (i, 0)),
        out_shape=jax.ShapeDtypeStruct((seq, 2 * HY_WIDTH), F32),
        compiler_params=_cparams("parallel"),
        name="hyena_filter",
    )(zp, w1p, b1.reshape(1, hh), w2, b2.reshape(1, hh), w3, freq.reshape(1, hh), win2)


def dft_matrices(seq):
    n = 2 * seq
    r = jnp.arange(seq, dtype=jnp.int32)[:, None]
    t = jnp.arange(n, dtype=jnp.int32)[None, :]
    ang = ((r * t) % n).astype(F32) * (2.0 * math.pi / n)
    c, s = jnp.cos(ang), jnp.sin(ang)
    nyq = jnp.where(t % 2 == 0, 1.0, -1.0).astype(F32)
    fwd = jnp.concatenate([c, jnp.where(r == 0, nyq, -s)], axis=0)
    ct = c[:, :seq].T
    st = s[:, :seq].T
    r_row = r.T
    inv_r = jnp.where(r_row == 0, 1.0, 2.0 * ct) / n
    inv_s = jnp.where(r_row == 0, nyq[:, :seq].T, -2.0 * st) / n
    inv = jnp.concatenate([inv_r, inv_s], axis=1)
    return fwd.astype(BF16), inv.astype(BF16)


def _kernel_dft_kernel(a_ref, khi_ref, klo_ref, o_ref):
    o_ref[...] = _dot(a_ref[...], khi_ref[...]) + _dot(a_ref[...], klo_ref[...])


def kernel_spectrum(fwd, k):
    n = fwd.shape[0]
    khi = k.astype(BF16)
    klo = (k - khi.astype(F32)).astype(BF16)
    tf = min(256, n)
    return pl.pallas_call(
        _kernel_dft_kernel,
        grid=(n // tf,),
        in_specs=[pl.BlockSpec((tf, n), lambda i: (i, 0)),
                  pl.BlockSpec((n, HY_WIDTH), lambda i: (0, 0)),
                  pl.BlockSpec((n, HY_WIDTH), lambda i: (0, 0))],
        out_specs=pl.BlockSpec((tf, HY_WIDTH), lambda i: (i, 0)),
        out_shape=jax.ShapeDtypeStruct((n, HY_WIDTH), F32),
        compiler_params=_cparams("parallel"),
        name="hyena_kernel_dft",
    )(fwd, khi, klo)


def _shift_down(u):
    rows = lax.broadcasted_iota(jnp.int32, u.shape, 0)
    return jnp.where(rows == 0, 0.0, pltpu.roll(u, 1, 0))


def _shift_up(u):
    n = u.shape[0]
    rows = lax.broadcasted_iota(jnp.int32, u.shape, 0)
    return jnp.where(rows == n - 1, 0.0, pltpu.roll(u, n - 1, 0))


def _hyena_pre_kernel(u_ref, w_ref, b_ref, vx_ref, x0_ref):
    def conv(c0):
        u = u_ref[:, c0:c0 + LANES].astype(F32)
        w = w_ref[:, c0:c0 + LANES]
        return _shift_down(u) * w[0:1] + u * w[1:2] + _shift_up(u) * w[2:3] + b_ref[:, c0:c0 + LANES]

    for c in range(0, HY_WIDTH, LANES):
        x0_ref[:, c:c + LANES] = conv(c).astype(x0_ref.dtype)
        vx_ref[:, c:c + LANES] = (conv(2 * HY_WIDTH + c) * conv(HY_WIDTH + c)).astype(vx_ref.dtype)


def hyena_pre(proj, conv_w, conv_b, bsz, seq):
    w3 = 3 * HY_WIDTH
    out = jax.ShapeDtypeStruct((bsz * seq, HY_WIDTH), BF16)
    return pl.pallas_call(
        _hyena_pre_kernel,
        grid=(bsz,),
        in_specs=[pl.BlockSpec((seq, w3), lambda b: (b, 0)),
                  pl.BlockSpec((3, w3), lambda b: (0, 0)),
                  pl.BlockSpec((1, w3), lambda b: (0, 0))],
        out_specs=[pl.BlockSpec((seq, HY_WIDTH), lambda b: (b, 0))] * 2,
        out_shape=[out, out],
        compiler_params=_cparams("parallel"),
        name="hyena_pre",
    )(proj, conv_w, conv_b.reshape(1, w3))


def _hyena_conv_kernel(vx_ref, x0_ref, ar_ref, as_ref, br_ref, bs_ref, kp_ref, kq_ref, kp2_ref, d_ref,
                       o_ref, acc_ref):
    f = pl.program_id(1)
    vx = vx_ref[...]
    r = _dot(ar_ref[...], vx)
    s = _dot(as_ref[...], vx)
    kq = kq_ref[...]
    zr = (r * kp_ref[...] - s * kq).astype(BF16)
    zs = (r * kq + s * kp2_ref[...]).astype(BF16)
    part = _dot(br_ref[...], zr) + _dot(bs_ref[...], zs)

    @pl.when(f == 0)
    def _():
        acc_ref[...] = part

    @pl.when(f > 0)
    def _():
        acc_ref[...] += part

    @pl.when(f == pl.num_programs(1) - 1)
    def _():
        y = acc_ref[...] + vx.astype(F32) * d_ref[...]
        o_ref[...] = (y * x0_ref[...].astype(F32)).astype(o_ref.dtype)


def hyena_conv(vx, x0, fwd, inv, kp, kq, kp2, d, bsz, seq):
    tf = min(256, seq)
    nf = seq // tf
    w = HY_WIDTH
    return pl.pallas_call(
        _hyena_conv_kernel,
        grid=(bsz, nf),
        in_specs=[pl.BlockSpec((seq, w), lambda b, f: (b, 0)),
                  pl.BlockSpec((seq, w), lambda b, f: (b, 0)),
                  pl.BlockSpec((tf, seq), lambda b, f: (f, 0)),
                  pl.BlockSpec((tf, seq), lambda b, f, nf=nf: (nf + f, 0)),
                  pl.BlockSpec((seq, tf), lambda b, f: (0, f)),
                  pl.BlockSpec((seq, tf), lambda b, f, nf=nf: (0, nf + f)),
                  pl.BlockSpec((tf, w), lambda b, f: (f, 0)),
                  pl.BlockSpec((tf, w), lambda b, f: (f, 0)),
                  pl.BlockSpec((tf, w), lambda b, f: (f, 0)),
                  pl.BlockSpec((1, w), lambda b, f: (0, 0))],
        out_specs=pl.BlockSpec((seq, w), lambda b, f: (b, 0)),
        out_shape=jax.ShapeDtypeStruct((bsz * seq, w), BF16),
        scratch_shapes=[pltpu.VMEM((seq, w), F32)],
        compiler_params=_cparams("parallel", "arbitrary"),
        name="hyena_conv",
    )(vx, x0, fwd, fwd, inv, inv, kp, kq, kp2, d.reshape(1, w))


def _rel_bucket(rel):
    nb = REL_BUCKETS // 2
    max_exact = nb // 2
    ret = (rel > 0).astype(jnp.int32) * nb
    n = jnp.abs(rel)
    nf = jnp.maximum(n, 1).astype(F32)
    large = max_exact + (jnp.log(nf / max_exact) / math.log(REL_MAX_DIST / max_exact)
                         * (nb - max_exact)).astype(jnp.int32)
    large = jnp.minimum(large, nb - 1)
    return ret + jnp.where(n < max_exact, n, large)


def window_bias_mask(rel_bias):
    j = jnp.arange(WA_BLOCK, dtype=jnp.int32)[:, None]
    s = jnp.arange(3 * WA_BLOCK, dtype=jnp.int32)[None, :]
    rel = (s - WA_BLOCK) - j
    bias = jnp.transpose(rel_bias.astype(F32)[_rel_bucket(rel)], (2, 0, 1))
    band = jnp.abs(rel) <= WA_WINDOW
    return jnp.where(band[None], bias, NEG)


def _window_attn_kernel(sink_ref, q_ref, k_ref, v_ref, bias_ref, o_ref):
    nb = q_ref.shape[0] // WA_BLOCK
    scale = HEAD_DIM ** -0.5
    col = lax.broadcasted_iota(jnp.int32, (WA_BLOCK, 3 * WA_BLOCK), 1)

    def block(i, carry):
        ip = jnp.maximum(i - 1, 0)
        inx = jnp.minimum(i + 1, nb - 1)
        rows = lambda j: pl.ds(pl.multiple_of(j * WA_BLOCK, WA_BLOCK), WA_BLOCK)
        qb = q_ref[rows(i), :]
        kslab = jnp.concatenate([k_ref[rows(ip), :], k_ref[rows(i), :], k_ref[rows(inx), :]], axis=0)
        vslab = jnp.concatenate([v_ref[rows(ip), :], v_ref[rows(i), :], v_ref[rows(inx), :]], axis=0)
        lo = jnp.where(i > 0, 0, WA_BLOCK)
        hi = jnp.where(i < nb - 1, 3 * WA_BLOCK, 2 * WA_BLOCK)
        valid = jnp.logical_and(col >= lo, col < hi)
        outs = []
        for h in range(WA_HEADS):
            g = h // WA_REP
            kg = kslab[:, g * HEAD_DIM:(g + 1) * HEAD_DIM]
            vg = vslab[:, g * HEAD_DIM:(g + 1) * HEAD_DIM]
            s = _dot_nt(qb[:, h * HEAD_DIM:(h + 1) * HEAD_DIM], kg) * scale + bias_ref[h]
            s = jnp.where(valid, s, NEG)
            sk = sink_ref[h]
            m = jnp.maximum(jnp.max(s, axis=-1, keepdims=True), sk)
            p = jnp.exp(s - m)
            den = jnp.sum(p, axis=-1, keepdims=True) + jnp.exp(sk - m)
            outs.append(_dot(p.astype(BF16), vg) / den)
        o_ref[rows(i), :] = jnp.concatenate(outs, axis=-1).astype(o_ref.dtype)
        return carry

    lax.fori_loop(0, nb, block, 0)


def window_attn(proj, sink, bias_mask, bsz, seq):
    hq = WA_HEADS * HEAD_DIM
    hkv = WA_KV_HEADS * HEAD_DIM
    q_blk = (3 * HY_WIDTH) // hq
    k_blk = (3 * HY_WIDTH + hq) // hkv
    return pl.pallas_call(
        _window_attn_kernel,
        grid=(bsz,),
        in_specs=[pl.BlockSpec(memory_space=pltpu.SMEM),
                  pl.BlockSpec((seq, hq), lambda b: (b, q_blk)),
                  pl.BlockSpec((seq, hkv), lambda b: (b, k_blk)),
                  pl.BlockSpec((seq, hkv), lambda b: (b, k_blk + 1)),
                  pl.BlockSpec((WA_HEADS, WA_BLOCK, 3 * WA_BLOCK), lambda b: (0, 0, 0))],
        out_specs=pl.BlockSpec((seq, hq), lambda b: (b, 0)),
        out_shape=jax.ShapeDtypeStruct((bsz * seq, hq), BF16),
        compiler_params=_cparams("parallel"),
        name="window_attn",
    )(sink.astype(F32), proj, proj, proj, bias_mask)


def _out_proj_kernel(x_ref, a_ref, b_ref, w_ref, o_ref):
    acc = _dot(a_ref[...], w_ref[0:HALF, :]) + _dot(b_ref[...], w_ref[HALF:, :])
    o_ref[...] = x_ref[...] + acc


def out_proj_ab(x2d, y_a, y_b, w, tm=512):
    t = x2d.shape[0]
    tm = min(tm, t)
    return pl.pallas_call(
        _out_proj_kernel,
        grid=(t // tm,),
        in_specs=[pl.BlockSpec((tm, D_MODEL), lambda i: (i, 0)),
                  pl.BlockSpec((tm, HALF), lambda i: (i, 0)),
                  pl.BlockSpec((tm, HALF), lambda i: (i, 0)),
                  pl.BlockSpec((D_MODEL, D_MODEL), lambda i: (0, 0))],
        out_specs=pl.BlockSpec((tm, D_MODEL), lambda i: (i, 0)),
        out_shape=jax.ShapeDtypeStruct(x2d.shape, F32),
        compiler_params=_cparams("parallel"),
        name="out_proj_ab",
    )(x2d, y_a, y_b, w)


def s5_discretise(a_re, a_im, log_dt, b_re, b_im, c_re, c_im):
    lam = lax.complex(a_re.astype(F32), a_im.astype(F32))
    dt = jnp.exp(log_dt.astype(F32))[..., None]
    abar = jnp.exp(lam * dt)
    bmat = lax.complex(b_re.astype(F32), b_im.astype(F32))
    bbar = ((abar - 1.0) / lam)[..., None] * bmat
    cmat = lax.complex(c_re.astype(F32), c_im.astype(F32))
    nj, gl = S5_GROUPS // SUBLANES, SUBLANES
    eye = jnp.eye(gl, dtype=F32)
    a5 = jnp.stack([abar.real, abar.imag], axis=1).reshape(2, 2, nj, 1, gl * S5_STATE)
    a5 = jnp.transpose(a5, (0, 2, 1, 3, 4))

    def pack_b(x):
        x = x.reshape(2, nj, gl, S5_STATE, S5_GROUP)
        y = jnp.einsum('hg,djgpc->djhcgp', eye, x)
        return y.reshape(2, nj, gl * S5_GROUP, gl * S5_STATE)

    def pack_c(x):
        x = x.reshape(2, nj, gl, S5_GROUP, S5_STATE)
        y = jnp.einsum('hg,djgcp->djgphc', eye, x)
        return y.reshape(2, nj, gl * S5_STATE, gl * S5_GROUP)

    bm = jnp.concatenate([pack_b(bbar.real), pack_b(bbar.imag)], axis=-1).astype(BF16)
    cm = jnp.concatenate([pack_c(cmat.real), -pack_c(cmat.imag)], axis=-2).astype(BF16)
    return a5, bm, cm


def _s5_kernel(u_ref, a_ref, b_ref, c_ref, y_ref, buf_ref, h_ref, *, chunk):
    d = pl.program_id(1)
    nj = S5_GROUPS // SUBLANES
    sw = SUBLANES * S5_STATE
    rows = chunk * SUBLANES

    @pl.when(pl.program_id(2) == 0)
    def _():
        h_ref[...] = jnp.zeros_like(h_ref)

    u = u_ref[...].reshape(rows, HALF).astype(BF16)
    for j in range(nj):
        buf_ref[j] = _dot(u[:, j * LANES:(j + 1) * LANES], b_ref[0, j])

    def step(s, carry):
        t = jnp.where(d == 0, s, chunk - 1 - s)
        r0 = pl.multiple_of(t * SUBLANES, SUBLANES)
        new = []
        for j in range(nj):
            hr, hi = carry[2 * j], carry[2 * j + 1]
            ar = a_ref[0, j, 0]
            ai = a_ref[0, j, 1]
            br = buf_ref[j, pl.ds(r0, SUBLANES), 0:sw]
            bi = buf_ref[j, pl.ds(r0, SUBLANES), sw:2 * sw]
            nr = ar * hr - ai * hi + br
            ni = ar * hi + ai * hr + bi
            buf_ref[j, pl.ds(r0, SUBLANES), 0:sw] = nr
            buf_ref[j, pl.ds(r0, SUBLANES), sw:2 * sw] = ni
            new += [nr, ni]
        return tuple(new)

    init = tuple(h_ref[k] for k in range(2 * nj))
    fin = lax.fori_loop(0, chunk, step, init)
    for k in range(2 * nj):
        h_ref[k] = fin[k]

    for j in range(nj):
        yj = _dot(buf_ref[j].astype(BF16), c_ref[0, j])
        y_ref[0, :, :, j * LANES:(j + 1) * LANES] = yj.reshape(chunk, SUBLANES, LANES).astype(y_ref.dtype)


def s5_scan(u_tm, a5, bm, cm, bsz, seq, chunk=64):
    chunk = min(chunk, seq)
    nc = seq // chunk
    nj = S5_GROUPS // SUBLANES
    sw = SUBLANES * S5_STATE

    def tchunk(d, i):
        return i + d * (nc - 1 - 2 * i)

    return pl.pallas_call(
        functools.partial(_s5_kernel, chunk=chunk),
        grid=(bsz // SUBLANES, 2, nc),
        in_specs=[pl.BlockSpec((chunk, SUBLANES, HALF), lambda b, d, i: (tchunk(d, i), b, 0)),
                  pl.BlockSpec((1, nj, 2, 1, sw), lambda b, d, i: (d, 0, 0, 0, 0)),
                  pl.BlockSpec((1, nj, LANES, 2 * sw), lambda b, d, i: (d, 0, 0, 0)),
                  pl.BlockSpec((1, nj, 2 * sw, LANES), lambda b, d, i: (d, 0, 0, 0))],
        out_specs=pl.BlockSpec((1, chunk, SUBLANES, HALF), lambda b, d, i: (d, tchunk(d, i), b, 0)),
        out_shape=jax.ShapeDtypeStruct((2, seq, bsz, HALF), F32),
        scratch_shapes=[pltpu.VMEM((nj, chunk * SUBLANES, 2 * sw), F32),
                        pltpu.VMEM((2 * nj, SUBLANES, sw), F32)],
        compiler_params=_cparams("parallel", "arbitrary", "arbitrary"),
        name="s5_scan",
    )(u_tm, a5, bm, cm)


def _gelu_tanh(x):
    return 0.5 * x * (1.0 + jnp.tanh(math.sqrt(2.0 / math.pi) * (x + 0.044715 * (x * x * x))))


def _out_proj_cd_kernel(x_ref, u_ref, yf_ref, yb_ref, mla_ref, d_ref, gw_ref, gb_ref, w_ref, o_ref):
    y = u_ref[...] * d_ref[...] + yf_ref[0] + yb_ref[0]
    g = _gelu_tanh(y)
    z = _dot(g.astype(BF16), gw_ref[...]) + gb_ref[...]
    y_s5 = g * jax.nn.sigmoid(z)
    acc = _dot(y_s5.astype(BF16), w_ref[0:HALF, :]) + _dot(mla_ref[...], w_ref[HALF:, :])
    o_ref[...] = x_ref[...] + acc


def out_proj_cd(x2d, u_tm2, y_tm2, y_mla, d, glu_w, glu_b, w, bsz, seq, tm=512):
    tm = min(tm, seq)
    nl = seq // tm
    row = lambda b, i: (b * nl + i, 0)
    const = lambda b, i: (0, 0)
    return pl.pallas_call(
        _out_proj_cd_kernel,
        grid=(bsz, nl),
        in_specs=[pl.BlockSpec((tm, D_MODEL), row),
                  pl.BlockSpec((tm, HALF), lambda b, i: (i, b)),
                  pl.BlockSpec((1, tm, HALF), lambda b, i: (0, i, b)),
                  pl.BlockSpec((1, tm, HALF), lambda b, i: (1, i, b)),
                  pl.BlockSpec((tm, HALF), row),
                  pl.BlockSpec((1, HALF), const),
                  pl.BlockSpec((HALF, HALF), const),
                  pl.BlockSpec((1, HALF), const),
                  pl.BlockSpec((D_MODEL, D_MODEL), const)],
        out_specs=pl.BlockSpec((tm, D_MODEL), row),
        out_shape=jax.ShapeDtypeStruct(x2d.shape, F32),
        compiler_params=_cparams("parallel", "parallel"),
        name="out_proj_cd",
    )(x2d, u_tm2, y_tm2, y_tm2, y_mla, d.reshape(1, HALF), glu_w, glu_b.reshape(1, HALF), w)


MLA_HP = 128


def mla_weights(w_uq, w_ukv):
    rq = w_uq.shape[0]
    wq = w_uq.astype(F32).reshape(rq, MLA_HEADS, MLA_NOPE + MLA_ROPE)
    half = MLA_ROPE // 2
    x1, x2 = wq[..., MLA_NOPE:MLA_NOPE + half], wq[..., MLA_NOPE + half:]
    zpad = jnp.zeros((rq, MLA_HEADS, MLA_HP - MLA_NOPE - MLA_ROPE), F32)
    wq1 = jnp.concatenate([wq, zpad], axis=-1).reshape(rq, MLA_HEADS * MLA_HP)
    wq2 = jnp.concatenate([jnp.zeros((rq, MLA_HEADS, MLA_NOPE), F32), -x2, x1, zpad], axis=-1)
    wq2 = wq2.reshape(rq, MLA_HEADS * MLA_HP)
    rk = w_ukv.shape[0]
    wkv = w_ukv.astype(F32).reshape(rk, MLA_HEADS, MLA_NOPE + MLA_V)
    wk = jnp.concatenate([wkv[..., :MLA_NOPE], jnp.zeros((rk, MLA_HEADS, MLA_HP - MLA_NOPE), F32)], axis=-1)
    wk = wk.reshape(rk, MLA_HEADS * MLA_HP)
    wv = wkv[..., MLA_NOPE:].reshape(rk, MLA_HEADS * MLA_V)
    return wq1.astype(BF16), wq2.astype(BF16), wk.astype(BF16), wv.astype(BF16)


def rope_tables(seq):
    inv = 1.0 / (ROPE_THETA ** (jnp.arange(0, MLA_ROPE, 2, dtype=F32) / MLA_ROPE))
    ang = jnp.arange(seq, dtype=F32)[:, None] * inv[None, :]
    c, s = jnp.cos(ang), jnp.sin(ang)
    ones = jnp.ones((seq, MLA_NOPE), F32)
    zpad = jnp.zeros((seq, MLA_HP - MLA_NOPE - MLA_ROPE), F32)
    cos_t = jnp.concatenate([ones, c, c, zpad], axis=-1)
    sin_t = jnp.concatenate([0.0 * ones, s, s, zpad], axis=-1)
    return cos_t, sin_t


def _mla_prep_kernel(r_ref, qg_ref, kg_ref, wq1_ref, wq2_ref, wk_ref, wv_ref, cos_ref, sin_ref,
                     q_ref, k_ref, v_ref):
    scale = (MLA_NOPE + MLA_ROPE) ** -0.5 * LOG2E
    cq = _rms(r_ref[:, 0:MLA_Q_RANK].astype(F32), qg_ref[...]).astype(BF16)
    o1 = MLA_Q_RANK + MLA_KV_RANK
    ckv = _rms(r_ref[:, MLA_Q_RANK:o1].astype(F32), kg_ref[...]).astype(BF16)
    cos_t, sin_t = cos_ref[...], sin_ref[...]
    kr = r_ref[:, o1:o1 + LANES].astype(F32) * cos_t + r_ref[:, o1 + LANES:o1 + 2 * LANES].astype(F32) * sin_t
    v_ref[...] = _dot(ckv, wv_ref[...]).astype(v_ref.dtype)
    for h in range(MLA_HEADS):
        sl = slice(h * MLA_HP, (h + 1) * MLA_HP)
        qh = _dot(cq, wq1_ref[:, sl]) * cos_t + _dot(cq, wq2_ref[:, sl]) * sin_t
        q_ref[:, sl] = (qh * scale).astype(q_ref.dtype)
        k_ref[:, sl] = (_dot(ckv, wk_ref[:, sl]) + kr).astype(k_ref.dtype)


def mla_prep(rest, q_norm, kv_norm, wq1, wq2, wk, wv, cos_t, sin_t, bsz, seq, tm=512):
    tm = min(tm, seq)
    nl = seq // tm
    t = bsz * seq
    wr = rest.shape[1]
    row = lambda b, i: (b * nl + i, 0)
    const = lambda b, i: (0, 0)
    qk = MLA_HEADS * MLA_HP
    return pl.pallas_call(
        _mla_prep_kernel,
        grid=(bsz, nl),
        in_specs=[pl.BlockSpec((tm, wr), row),
                  pl.BlockSpec((1, MLA_Q_RANK), const),
                  pl.BlockSpec((1, MLA_KV_RANK), const),
                  pl.BlockSpec((MLA_Q_RANK, qk), const),
                  pl.BlockSpec((MLA_Q_RANK, qk), const),
                  pl.BlockSpec((MLA_KV_RANK, qk), const),
                  pl.BlockSpec((MLA_KV_RANK, MLA_HEADS * MLA_V), const),
                  pl.BlockSpec((tm, MLA_HP), lambda b, i: (i, 0)),
                  pl.BlockSpec((tm, MLA_HP), lambda b, i: (i, 0))],
        out_specs=[pl.BlockSpec((tm, qk), row), pl.BlockSpec((tm, qk), row),
                   pl.BlockSpec((tm, MLA_HEADS * MLA_V), row)],
        out_shape=[jax.ShapeDtypeStruct((t, qk), BF16), jax.ShapeDtypeStruct((t, qk), BF16),
                   jax.ShapeDtypeStruct((t, MLA_HEADS * MLA_V), BF16)],
        compiler_params=_cparams("parallel", "parallel"),
        name="mla_prep",
    )(rest, q_norm.reshape(1, -1), kv_norm.reshape(1, -1), wq1, wq2, wk, wv, cos_t, sin_t)


def _mla_attn_kernel(q_ref, k_ref, v_ref, o_ref):
    v = v_ref[...]
    outs = []
    for h in range(2):
        sl = slice(h * MLA_HP, (h + 1) * MLA_HP)
        s = _dot_nt(q_ref[:, sl], k_ref[:, sl])
        m = jnp.max(s, axis=-1, keepdims=True)
        p = jnp.exp2(s - m)
        den = jnp.sum(p, axis=-1, keepdims=True)
        outs.append(_dot(p.astype(BF16), v) / den)
    lane = lax.broadcasted_iota(jnp.int32, outs[0].shape, 1)
    o_ref[...] = jnp.where(lane < MLA_V, outs[0], outs[1]).astype(o_ref.dtype)


def mla_attn(q, k, v, bsz, seq, tq=512):
    tq = min(tq, seq)
    nq = seq // tq
    return pl.pallas_call(
        _mla_attn_kernel,
        grid=(bsz, MLA_HEADS // 2, nq),
        in_specs=[pl.BlockSpec((tq, 2 * MLA_HP), lambda b, p, i: (b * nq + i, p)),
                  pl.BlockSpec((seq, 2 * MLA_HP), lambda b, p, i: (b, p)),
                  pl.BlockSpec((seq, 2 * MLA_V), lambda b, p, i: (b, p))],
        out_specs=pl.BlockSpec((tq, 2 * MLA_V), lambda b, p, i: (b * nq + i, p)),
        out_shape=jax.ShapeDtypeStruct((bsz * seq, MLA_HEADS * MLA_V), BF16),
        compiler_params=_cparams("parallel", "parallel", "parallel"),
        name="mla_attn",
    )(q, k, v)


def _cross_router_kernel(x_ref, kv_ref, gc_ref, wq_ref, wo_ref, gf_ref, wr_ref, xo_ref, hn_ref, aff_ref):
    x = x_ref[...]
    h = _rms(x, gc_ref[...]).astype(BF16)
    q = (_dot(h, wq_ref[...]) * (CA_HEAD_DIM ** -0.5 * LOG2E)).astype(BF16)
    hd = CA_HEADS * CA_HEAD_DIM
    outs = []
    for a in range(CA_HEADS):
        sl = slice(a * CA_HEAD_DIM, (a + 1) * CA_HEAD_DIM)
        s = _dot_nt(q[:, sl], kv_ref[:, sl])
        m = jnp.max(s, axis=-1, keepdims=True)
        p = jnp.exp2(s - m)
        den = jnp.sum(p, axis=-1, keepdims=True)
        outs.append((_dot(p.astype(BF16), kv_ref[:, hd + a * CA_HEAD_DIM:hd + (a + 1) * CA_HEAD_DIM]) / den))
    o = jnp.concatenate(outs, axis=-1).astype(BF16)
    xn = x + _dot(o, wo_ref[...])
    xo_ref[...] = xn
    hf = _rms(xn, gf_ref[...])
    hb = hf.astype(BF16)
    hn_ref[...] = hb
    lo = (hf - hb.astype(F32)).astype(BF16)
    logits = _dot(hb, wr_ref[0]) + (_dot(lo, wr_ref[0]) + _dot(hb, wr_ref[1]))
    lane = lax.broadcasted_iota(jnp.int32, logits.shape, 1)
    logits = jnp.where(lane < N_EXPERTS, logits, NEG)
    m = jnp.max(logits, axis=-1, keepdims=True)
    e = jnp.exp(logits - m)
    aff = e / jnp.sum(e, axis=-1, keepdims=True)
    aff_ref[...] = aff[:, 0:N_EXPERTS]


def cross_router(x2d, kv, ln_cross, w_q, w_o, ln_ffn, w_router, bsz, seq, tq=256):
    tq = min(tq, seq)
    nq = seq // tq
    t = bsz * seq
    mem = kv.shape[0] // bsz
    hd = CA_HEADS * CA_HEAD_DIM
    wr = jnp.pad(w_router.astype(F32), ((0, 0), (0, LANES - N_EXPERTS)))
    wr_hi = wr.astype(BF16)
    wr2 = jnp.stack([wr_hi, (wr - wr_hi.astype(F32)).astype(BF16)])
    row = lambda b, i: (b * nq + i, 0)
    const = lambda b, i: (0, 0)
    return pl.pallas_call(
        _cross_router_kernel,
        grid=(bsz, nq),
        in_specs=[pl.BlockSpec((tq, D_MODEL), row),
                  pl.BlockSpec((mem, 2 * hd), lambda b, i: (b, 0)),
                  pl.BlockSpec((1, D_MODEL), const),
                  pl.BlockSpec((D_MODEL, hd), const),
                  pl.BlockSpec((hd, D_MODEL), const),
                  pl.BlockSpec((1, D_MODEL), const),
                  pl.BlockSpec((2, D_MODEL, LANES), lambda b, i: (0, 0, 0))],
        out_specs=[pl.BlockSpec((tq, D_MODEL), row), pl.BlockSpec((tq, D_MODEL), row),
                   pl.BlockSpec((tq, N_EXPERTS), row)],
        out_shape=[jax.ShapeDtypeStruct((t, D_MODEL), F32), jax.ShapeDtypeStruct((t, D_MODEL), BF16),
                   jax.ShapeDtypeStruct((t, N_EXPERTS), F32)],
        compiler_params=_cparams("parallel", "parallel"),
        name="cross_router",
    )(x2d, kv, ln_cross.reshape(1, -1), w_q, w_o, ln_ffn.reshape(1, -1), wr2)


def _expert_ffn_kernel(x_ref, g_ref, wg_ref, wu_ref, wd_ref, o_ref, acc_ref):
    f = pl.program_id(2)
    x = x_ref[0]
    a = _dot(x, wg_ref[0])
    u = _dot(x, wu_ref[0])
    hid = (a * jax.nn.sigmoid(a) * u).astype(BF16)
    part = _dot(hid, wd_ref[0])

    @pl.when(f == 0)
    def _():
        acc_ref[...] = part

    @pl.when(f > 0)
    def _():
        acc_ref[...] += part

    @pl.when(f == pl.num_programs(2) - 1)
    def _():
        o_ref[0] = (acc_ref[...] * g_ref[0]).astype(o_ref.dtype)


def expert_ffn(xe, gate, w_gate, w_up, w_down, tm=1024, tf=512):
    e, cap, _ = xe.shape
    tm = min(tm, cap)
    return pl.pallas_call(
        _expert_ffn_kernel,
        grid=(e, cap // tm, D_EXPERT // tf),
        in_specs=[pl.BlockSpec((1, tm, D_MODEL), lambda e, m, f: (e, m, 0)),
                  pl.BlockSpec((1, tm, 1), lambda e, m, f: (e, m, 0)),
                  pl.BlockSpec((1, D_MODEL, tf), lambda e, m, f: (e, 0, f)),
                  pl.BlockSpec((1, D_MODEL, tf), lambda e, m, f: (e, 0, f)),
                  pl.BlockSpec((1, tf, D_MODEL), lambda e, m, f: (e, f, 0))],
        out_specs=pl.BlockSpec((1, tm, D_MODEL), lambda e, m, f: (e, m, 0)),
        out_shape=jax.ShapeDtypeStruct((e, cap, D_MODEL), BF16),
        scratch_shapes=[pltpu.VMEM((tm, D_MODEL), F32)],
        compiler_params=_cparams("parallel", "parallel", "arbitrary"),
        name="expert_ffn",
    )(xe, gate, w_gate, w_up, w_down)


ROUTE_GROUP = SUBLANES


def _route_thr_kernel(a_ref, thr_ref, *, cap):
    bits = pltpu.bitcast(a_ref[...], jnp.int32)

    def body(i, lo):
        cand = lo | jnp.left_shift(jnp.int32(1), 30 - i)
        cnt = jnp.sum(jnp.where(bits >= cand, 1.0, 0.0), axis=1, keepdims=True)
        return jnp.where(cnt >= cap, cand, lo)

    thr_ref[...] = lax.fori_loop(0, 31, body, jnp.zeros(thr_ref.shape, jnp.int32))


def _prefix_rows(m, upper, lower):
    mb = m.astype(BF16)
    incl = _dot(mb, upper)
    tot = jnp.broadcast_to(incl[:, LANES - 1:LANES], incl.shape).astype(BF16)
    return incl - m + _dot(lower, tot)


def _route_mask_kernel(thr_ref, a_ref, upper_ref, lower_ref, sel_ref, pos_ref, *, cap):
    e = pl.program_id(0)
    thr = thr_ref[e]
    bits = pltpu.bitcast(a_ref[0], jnp.int32)
    gt = jnp.where(bits > thr, 1.0, 0.0)
    eq = jnp.where(bits == thr, 1.0, 0.0)
    need = cap - jnp.sum(jnp.sum(gt, axis=1, keepdims=True), axis=0, keepdims=True)
    eq_rank = _prefix_rows(eq, upper_ref[...], lower_ref[...])
    sel = gt + jnp.where(eq_rank < need, eq, 0.0)
    sel_ref[0] = sel
    pos_ref[0] = _prefix_rows(sel, upper_ref[...], lower_ref[...]).astype(jnp.int32)


def _route_compact_kernel(glo_ref, ghi_ref, a_ref, sel_ref, pos_ref, idx_ref, gate_ref):
    e = pl.program_id(0)
    nc = idx_ref.shape[1]
    gw = ROUTE_GROUP * LANES
    slot0 = lax.broadcasted_iota(jnp.int32, (LANES, LANES), 0)
    lane = lax.broadcasted_iota(jnp.int32, (1, gw), 1)
    zeros = jnp.zeros((2 * SUBLANES - 5, gw), F32)

    def chunk(c, carry):
        slot = slot0 + c * LANES

        def group(g, acc):
            r0 = pl.multiple_of(g * ROUTE_GROUP, ROUTE_GROUP)
            pos = pos_ref[0, pl.ds(r0, ROUTE_GROUP), :]
            sel = sel_ref[0, pl.ds(r0, ROUTE_GROUP), :]
            aff = a_ref[0, pl.ds(r0, ROUTE_GROUP), :]
            hit = jnp.concatenate(
                [jnp.where(jnp.logical_and(pos[j:j + 1] == slot, sel[j:j + 1] > 0.0), 1.0, 0.0)
                 for j in range(ROUTE_GROUP)], axis=1).astype(BF16)
            arow = jnp.concatenate([aff[j:j + 1] for j in range(ROUTE_GROUP)], axis=1)
            tok = lane + g * gw
            g0 = arow.astype(BF16).astype(F32)
            r1 = arow - g0
            g1 = r1.astype(BF16).astype(F32)
            g2 = r1 - g1
            lhs = jnp.concatenate([(tok >> 8).astype(F32), (tok & 255).astype(F32), g0, g1, g2, zeros], axis=0)
            return acc + _dot_nt(lhs.astype(BF16), hit)

        acc = lax.fori_loop(glo_ref[e * nc + c], ghi_ref[e * nc + c] + 1, group,
                            jnp.zeros((2 * SUBLANES, LANES), F32))
        idx_ref[0, pl.ds(c, 1), :] = (acc[0:1] * 256.0 + acc[1:2]).astype(jnp.int32)
        gate_ref[0, pl.ds(c, 1), :] = (acc[2:3] + acc[3:4]) + acc[4:5]
        return carry

    lax.fori_loop(0, nc, chunk, 0)


def route_tokens(aff, cap):
    t, ne = aff.shape
    rows = t // LANES
    nc = cap // LANES
    aff_t = aff.T
    thr = pl.pallas_call(
        functools.partial(_route_thr_kernel, cap=cap),
        out_shape=jax.ShapeDtypeStruct((ne, 1), jnp.int32),
        compiler_params=pltpu.CompilerParams(vmem_limit_bytes=V7X_VMEM_LIMIT_BYTES),
        name="route_threshold",
    )(aff_t)
    aff3 = aff_t.reshape(ne, rows, LANES)
    ii = jnp.arange(LANES)
    upper = (ii[:, None] <= ii[None, :]).astype(BF16)
    rr = jnp.arange(rows)
    lower = (rr[None, :] < rr[:, None]).astype(BF16)
    blk = pl.BlockSpec((1, rows, LANES), lambda e: (e, 0, 0))
    sel, pos = pl.pallas_call(
        functools.partial(_route_mask_kernel, cap=cap),
        grid=(ne,),
        in_specs=[pl.BlockSpec(memory_space=pltpu.SMEM), blk,
                  pl.BlockSpec((LANES, LANES), lambda e: (0, 0)),
                  pl.BlockSpec((rows, rows), lambda e: (0, 0))],
        out_specs=[blk, blk],
        out_shape=[jax.ShapeDtypeStruct((ne, rows, LANES), F32), jax.ShapeDtypeStruct((ne, rows, LANES), jnp.int32)],
        compiler_params=_cparams("parallel"),
        name="route_mask",
    )(thr.reshape(ne), aff3, upper, lower)
    gsz = ROUTE_GROUP
    first = pos[:, ::gsz, 0]
    starts = jnp.arange(nc, dtype=jnp.int32) * LANES
    glo = jax.vmap(lambda f: jnp.searchsorted(f, starts, side='right'))(first).astype(jnp.int32) - 1
    ghi = jax.vmap(lambda f: jnp.searchsorted(f, starts + (LANES - 1), side='right'))(first).astype(jnp.int32) - 1
    out_blk = pl.BlockSpec((1, nc, LANES), lambda e, *_: (e, 0, 0))
    grid_spec = pltpu.PrefetchScalarGridSpec(
        num_scalar_prefetch=2,
        grid=(ne,),
        in_specs=[pl.BlockSpec((1, rows, LANES), lambda e, *_: (e, 0, 0))] * 3,
        out_specs=[out_blk, out_blk],
    )
    idx, gate = pl.pallas_call(
        _route_compact_kernel,
        grid_spec=grid_spec,
        out_shape=[jax.ShapeDtypeStruct((ne, nc, LANES), jnp.int32), jax.ShapeDtypeStruct((ne, nc, LANES), F32)],
        compiler_params=_cparams("parallel"),
        name="route_compact",
    )(glo.reshape(-1), ghi.reshape(-1), aff3, sel, pos)
    return idx.reshape(ne, cap), gate.reshape(ne, cap)


COMBINE_ROWS = 128


def _combine_kernel(clo_ref, chi_ref, x_ref, idx_ref, ye_hbm, o_ref, buf, sem, *, tb, maxc):
    b = pl.program_id(0)
    nb = pl.num_programs(0)
    ne = N_EXPERTS
    t0 = b * tb
    row_id = lax.broadcasted_iota(jnp.int32, (tb, COMBINE_ROWS), 0) + t0

    def copy(bb, e, k):
        c = clo_ref[bb * ne + e] + k
        src = ye_hbm.at[e, pl.ds(pl.multiple_of(c * COMBINE_ROWS, COMBINE_ROWS), COMBINE_ROWS), :]
        return pltpu.make_async_copy(src, buf.at[e % 2, k], sem.at[e % 2, k])

    def start(bb, e):
        n = chi_ref[bb * ne + e] - clo_ref[bb * ne + e]
        for k in range(maxc):
            @pl.when(k < n)
            def _():
                copy(bb, e, k).start()

    @pl.when(b == 0)
    def _():
        start(b, 0)

    o_ref[...] = x_ref[...]
    for e in range(ne):
        if e + 1 < ne:
            start(b, e + 1)
        else:
            @pl.when(b + 1 < nb)
            def _():
                start(b + 1, 0)
        clo = clo_ref[b * ne + e]
        n = chi_ref[b * ne + e] - clo
        for k in range(maxc):
            @pl.when(k < n)
            def _():
                copy(b, e, k).wait()
                tok = idx_ref[e, pl.ds(clo + k, 1), :]
                onehot = jnp.where(row_id == tok, 1.0, 0.0).astype(BF16)
                o_ref[...] += _dot(onehot, buf[e % 2, k])


def moe_combine(x2d, ye, idx, tb=512):
    t = x2d.shape[0]
    e, cap = idx.shape
    tb = min(tb, t)
    nb = t // tb
    maxc = tb // COMBINE_ROWS + 1
    bounds = jnp.arange(nb + 1, dtype=jnp.int32) * tb
    pos = jax.vmap(lambda row: jnp.searchsorted(row, bounds, side='left'))(idx).astype(jnp.int32)
    lo, hi = pos[:, :-1], pos[:, 1:]
    clo = lo // COMBINE_ROWS
    chi = jnp.where(hi > lo, (hi + COMBINE_ROWS - 1) // COMBINE_ROWS, clo)
    clo = clo.T.reshape(-1)
    chi = chi.T.reshape(-1)
    idx3 = idx.reshape(e, cap // COMBINE_ROWS, COMBINE_ROWS)
    grid_spec = pltpu.PrefetchScalarGridSpec(
        num_scalar_prefetch=2,
        grid=(nb,),
        in_specs=[pl.BlockSpec((tb, D_MODEL), lambda b, *_: (b, 0)),
                  pl.BlockSpec((e, cap // COMBINE_ROWS, COMBINE_ROWS), lambda b, *_: (0, 0, 0)),
                  pl.BlockSpec(memory_space=pl.ANY)],
        out_specs=pl.BlockSpec((tb, D_MODEL), lambda b, *_: (b, 0)),
        scratch_shapes=[pltpu.VMEM((2, maxc, COMBINE_ROWS, D_MODEL), BF16),
                        pltpu.SemaphoreType.DMA((2, maxc))],
    )
    return pl.pallas_call(
        functools.partial(_combine_kernel, tb=tb, maxc=maxc),
        grid_spec=grid_spec,
        out_shape=jax.ShapeDtypeStruct(x2d.shape, F32),
        compiler_params=_cparams("arbitrary"),
        name="moe_combine",
    )(clo, chi, x2d, idx3, ye)


def _final_norm_kernel(x_ref, g_ref, o_ref):
    o_ref[...] = _rms(x_ref[...], g_ref[...])


def final_norm(x2d, g, tm=1024):
    t = x2d.shape[0]
    tm = min(tm, t)
    return pl.pallas_call(
        _final_norm_kernel,
        grid=(t // tm,),
        in_specs=[pl.BlockSpec((tm, D_MODEL), lambda i: (i, 0)), pl.BlockSpec((1, D_MODEL), lambda i: (0, 0))],
        out_specs=pl.BlockSpec((tm, D_MODEL), lambda i: (i, 0)),
        out_shape=jax.ShapeDtypeStruct(x2d.shape, F32),
        compiler_params=_cparams("parallel"),
        name="final_norm",
    )(x2d, g.reshape(1, -1))


def mixer_ab(x2d, e, p, shared, bsz, seq):
    proj = norm_proj(x2d, p['ln_mix_l'], p['w_in_ab'][e], bsz, seq, [(0, AB_IN, BF16, False)])[0]
    vx, x0 = hyena_pre(proj, p['hy_conv_w'][e], p['hy_conv_b'][e], bsz, seq)
    kp, kq, kp2 = shared['hy_spec'][e]
    y_hy = hyena_conv(vx, x0, shared['dft_fwd'], shared['dft_inv'], kp, kq, kp2, p['hy_d'][e], bsz, seq)
    y_wa = window_attn(proj, p['attn_sink'][e], shared['wa_bias'], bsz, seq)
    return out_proj_ab(x2d, y_hy, y_wa, p['w_out_ab'][e])


def mixer_cd(x2d, o, p, shared, bsz, seq):
    u_tm2, rest = norm_proj(x2d, p['ln_mix_l'], shared['w_in_cd'][o], bsz, seq,
                            [(0, HALF, F32, True), (HALF, CD_PAD - HALF, BF16, False)])
    a5, bm, cm = shared['s5'][o]
    y_tm = s5_scan(u_tm2.reshape(seq, bsz, HALF), a5, bm, cm, bsz, seq)
    wq1, wq2, wk, wv = shared['mla_w'][o]
    q, k, v = mla_prep(rest, p['mla_q_norm'][o], p['mla_kv_norm'][o], wq1, wq2, wk, wv,
                       shared['rope_cos'], shared['rope_sin'], bsz, seq)
    y_mla = mla_attn(q, k, v, bsz, seq)
    return out_proj_cd(x2d, u_tm2, y_tm.reshape(2, seq, bsz * HALF), y_mla, p['s5_d'][o],
                       p['s5_glu_w'][o], p['s5_glu_b'][o], p['w_out_cd'][o], bsz, seq)


def ec_moe(x2d, hn, aff, w_gate, w_up, w_down):
    t = x2d.shape[0]
    cap = EC_CAPACITY_FACTOR * t // N_EXPERTS
    idx, gate = route_tokens(aff, cap)
    xe = hn[idx]
    ye = expert_ffn(xe, gate[..., None], w_gate, w_up, w_down)
    return moe_combine(x2d, ye, idx)


def prepare_shared(p, seq):
    sh = {}
    sh['wa_bias'] = window_bias_mask(p['rel_bias'])
    sh['dft_fwd'], sh['dft_inv'] = dft_matrices(seq)
    sh['rope_cos'], sh['rope_sin'] = rope_tables(seq)
    specs = []
    for e in range(p['w_in_ab'].shape[0]):
        h = hyena_filters(seq, p['hy_filt_w1'][e], p['hy_filt_b1'][e], p['hy_filt_w2'][e], p['hy_filt_b2'][e],
                          p['hy_filt_w3'][e], p['hy_filt_freq'][e])
        h_fwd, h_bwd = h[:, :HY_WIDTH], h[:, HY_WIDTH:]
        k = jnp.concatenate([h_fwd, jnp.zeros_like(h_fwd[:1]), h_bwd[:0:-1]], axis=0)
        kf = kernel_spectrum(sh['dft_fwd'], k)
        k_r, k_s = kf[:seq], kf[seq:]
        specs.append((k_r, k_s.at[0].set(0.0), k_r.at[0].set(k_s[0])))
    sh['hy_spec'] = specs
    s5, mla_w, w_in_cd = [], [], []
    for o in range(p['w_in_cd'].shape[0]):
        s5.append(s5_discretise(p['s5_a_re'][o], p['s5_a_im'][o], p['s5_log_dt'][o], p['s5_b_re'][o],
                                p['s5_b_im'][o], p['s5_c_re'][o], p['s5_c_im'][o]))
        mla_w.append(mla_weights(p['mla_w_uq'][o], p['mla_w_ukv'][o]))
        w = p['w_in_cd'][o].astype(F32)
        o2 = HALF + MLA_Q_RANK + MLA_KV_RANK
        kr = w[:, o2:o2 + MLA_ROPE]
        half = MLA_ROPE // 2
        kr_rot = jnp.concatenate([-kr[:, half:], kr[:, :half]], axis=1)
        z64 = jnp.zeros((D_MODEL, MLA_NOPE), F32)
        z32 = jnp.zeros((D_MODEL, MLA_HP - MLA_NOPE - MLA_ROPE), F32)
        w_in_cd.append(jnp.concatenate([w[:, :o2], z64, kr, z32, z64, kr_rot, z32], axis=1).astype(BF16))
    sh['s5'], sh['mla_w'], sh['w_in_cd'] = s5, mla_w, w_in_cd
    return sh


def run_trunk(x, mem, p, shared):
    bsz, seq, _ = x.shape
    x2d = x.reshape(bsz * seq, D_MODEL)
    mem2d = mem.reshape(bsz * mem.shape[1], D_MODEL)
    for layer in range(DEPTH):
        pl_ = dict(p, ln_mix_l=p['ln_mix'][layer])
        if layer % 2 == 0:
            x2d = mixer_ab(x2d, layer // 2, pl_, shared, bsz, seq)
        else:
            x2d = mixer_cd(x2d, layer // 2, pl_, shared, bsz, seq)
        kv = norm_proj(mem2d, p['ln_mem'][layer], p['ca_w_kv'][layer], bsz, mem.shape[1],
                       [(0, 2 * CA_HEADS * CA_HEAD_DIM, BF16, False)])[0]
        x2d, hn, aff = cross_router(x2d, kv, p['ln_cross'][layer], p['ca_w_q'][layer], p['ca_w_o'][layer],
                                    p['ln_ffn'][layer], p['moe_w_router'][layer], bsz, seq)
        x2d = ec_moe(x2d, hn, aff, p['moe_w_gate'][layer], p['moe_w_up'][layer], p['moe_w_down'][layer])
    return final_norm(x2d, p['ln_final']).reshape(bsz, seq, D_MODEL)


_BF16_WEIGHTS = ('w_in_ab', 'w_out_ab', 'w_out_cd', 's5_glu_w', 'ca_w_q', 'ca_w_kv', 'ca_w_o',
                 'moe_w_gate', 'moe_w_up', 'moe_w_down')


def kernel(x_prompt, x_sample, mem_prompt, mem_sample, ln_mix, ln_cross, ln_mem, ln_ffn, ln_final, rel_bias, w_in_ab, w_out_ab, hy_conv_w, hy_conv_b, hy_filt_w1, hy_filt_b1, hy_filt_w2, hy_filt_b2, hy_filt_w3, hy_filt_freq, hy_d, attn_sink, w_in_cd, w_out_cd, s5_a_re, s5_a_im, s5_log_dt, s5_b_re, s5_b_im, s5_c_re, s5_c_im, s5_d, s5_glu_w, s5_glu_b, mla_q_norm, mla_w_uq, mla_kv_norm, mla_w_ukv, ca_w_q, ca_w_kv, ca_w_o, moe_w_router, moe_w_gate, moe_w_up, moe_w_down):
    p = dict(ln_mix=ln_mix, ln_cross=ln_cross, ln_mem=ln_mem, ln_ffn=ln_ffn, ln_final=ln_final,
             rel_bias=rel_bias, w_in_ab=w_in_ab, w_out_ab=w_out_ab, hy_conv_w=hy_conv_w,
             hy_conv_b=hy_conv_b, hy_filt_w1=hy_filt_w1, hy_filt_b1=hy_filt_b1,
             hy_filt_w2=hy_filt_w2, hy_filt_b2=hy_filt_b2, hy_filt_w3=hy_filt_w3,
             hy_filt_freq=hy_filt_freq, hy_d=hy_d, attn_sink=attn_sink, w_in_cd=w_in_cd,
             w_out_cd=w_out_cd, s5_a_re=s5_a_re, s5_a_im=s5_a_im, s5_log_dt=s5_log_dt,
             s5_b_re=s5_b_re, s5_b_im=s5_b_im, s5_c_re=s5_c_re, s5_c_im=s5_c_im, s5_d=s5_d,
             s5_glu_w=s5_glu_w, s5_glu_b=s5_glu_b, mla_q_norm=mla_q_norm, mla_w_uq=mla_w_uq,
             mla_kv_norm=mla_kv_norm, mla_w_ukv=mla_w_ukv, ca_w_q=ca_w_q, ca_w_kv=ca_w_kv,
             ca_w_o=ca_w_o, moe_w_router=moe_w_router, moe_w_gate=moe_w_gate,
             moe_w_up=moe_w_up, moe_w_down=moe_w_down)
    assert x_prompt.shape[1] == x_sample.shape[1]
    shared = prepare_shared(p, x_prompt.shape[1])
    for name in _BF16_WEIGHTS:
        p[name] = p[name].astype(BF16)
    y_prompt = run_trunk(x_prompt, mem_prompt, p, shared)
    y_sample = run_trunk(x_sample, mem_sample, p, shared)
    return (y_prompt, y_sample)
```

```python
import functools
import math

import jax
import jax.numpy as jnp
import numpy as np
from jax import lax
from jax.experimental import pallas as pl
from jax.experimental.pallas import tpu as pltpu

D_MODEL = 1024
DEPTH = 4
HALF = 512
HEAD_DIM = 64
EPS = 1e-6
NEG = -1e30

HY_WIDTH = HALF
HY_EMB = 33
HY_BANDS = (HY_EMB - 1) // 2
HY_FILT_HIDDEN = 64
HY_DECAY_TARGET = 1e-2
HY_FAST = 0.3
HY_SLOW = 1.5
HY_MIN_DECAY = math.log(HY_DECAY_TARGET) / HY_SLOW
HY_MAX_DECAY = math.log(HY_DECAY_TARGET) / HY_FAST
HY_SHIFT = 0.05

WA_HEADS = 8
WA_KV_HEADS = 2
WA_REP = 4
WA_WINDOW = 128
WA_BLOCK = 128
REL_BUCKETS = 32
REL_MAX_DIST = 128

S5_GROUP = 16
S5_GROUPS = 32
S5_STATE = 64

MLA_HEADS = 8
MLA_NOPE = 64
MLA_ROPE = 32
MLA_V = 64
MLA_Q_RANK = 256
MLA_KV_RANK = 128
ROPE_THETA = 10000.0

CA_HEADS = 4
CA_HEAD_DIM = 128

N_EXPERTS = 16
EC_CAPACITY_FACTOR = 2
D_EXPERT = 2048

AB_IN = 3 * HY_WIDTH + (WA_HEADS + 2 * WA_KV_HEADS) * HEAD_DIM
CD_PAD = 1152

V7X_VMEM_LIMIT_BYTES = 56 * 1024 * 1024
LANES = 128
SUBLANES = 8
LOG2E = math.log2(math.e)

BF16 = jnp.bfloat16
F32 = jnp.float32


def _cparams(*sem):
    return pltpu.CompilerParams(dimension_semantics=sem, vmem_limit_bytes=V7X_VMEM_LIMIT_BYTES)


def _dot(a, b):
    return jnp.dot(a, b, preferred_element_type=F32)


def _dot_nt(a, b):
    return lax.dot_general(a, b, (((1,), (1,)), ((), ())), preferred_element_type=F32)


def _rms(xf, g):
    return xf * lax.rsqrt(jnp.mean(xf * xf, axis=-1, keepdims=True) + EPS) * g


def _norm_proj_kernel(x_ref, g_ref, w_ref, *out_refs, splits):
    hn = _rms(x_ref[...].astype(F32), g_ref[...]).astype(BF16)
    for o_ref, (start, width) in zip(out_refs, splits):
        for c0 in range(0, width, 512):
            cw = min(512, width - c0)
            o_ref[:, c0:c0 + cw] = _dot(hn, w_ref[:, start + c0:start + c0 + cw]).astype(o_ref.dtype)


def norm_proj(x2d, gain, w, bsz, seq, outs, tm=512):
    tm = min(tm, seq)
    nl = seq // tm
    n = w.shape[1]
    out_shapes, out_specs, splits = [], [], []
    for start, width, dtype, time_major in outs:
        splits.append((start, width))
        if time_major:
            out_shapes.append(jax.ShapeDtypeStruct((seq, bsz * width), dtype))
            out_specs.append(pl.BlockSpec((tm, width), lambda b, i: (i, b)))
        else:
            out_shapes.append(jax.ShapeDtypeStruct((bsz * seq, width), dtype))
            out_specs.append(pl.BlockSpec((tm, width), lambda b, i, nl=nl: (b * nl + i, 0)))
    return pl.pallas_call(
        functools.partial(_norm_proj_kernel, splits=tuple(splits)),
        grid=(bsz, nl),
        in_specs=[pl.BlockSpec((tm, D_MODEL), lambda b, i, nl=nl: (b * nl + i, 0)),
                  pl.BlockSpec((1, D_MODEL), lambda b, i: (0, 0)),
                  pl.BlockSpec((D_MODEL, n), lambda b, i: (0, 0))],
        out_specs=out_specs,
        out_shape=out_shapes,
        compiler_params=_cparams("parallel", "parallel"),
        name="norm_proj",
    )(x2d, gain.reshape(1, D_MODEL), w)


def _hyena_filter_kernel(z_ref, w1_ref, b1_ref, w2_ref, b2_ref, w3_ref, fr_ref, win_ref, o_ref):
    hp = lax.Precision.HIGHEST
    fr = fr_ref[...]
    h = jnp.sin(fr * (jnp.dot(z_ref[...], w1_ref[...], precision=hp, preferred_element_type=F32) + b1_ref[...]))
    h = jnp.sin(fr * (jnp.dot(h, w2_ref[...], precision=hp, preferred_element_type=F32) + b2_ref[...]))
    h = jnp.dot(h, w3_ref[...], precision=hp, preferred_element_type=F32)
    o_ref[...] = h * win_ref[...]


def hyena_filters(seq, w1, b1, w2, b2, w3, freq):
    t = jnp.linspace(0.0, 1.0, seq, dtype=F32)[:, None]
    ang = 2.0 * math.pi * jnp.arange(seq, dtype=F32)[:, None] / seq
    bands = jnp.linspace(1e-4, HY_BANDS - 1, HY_BANDS, dtype=F32)[None, :]
    z = jnp.concatenate([t, jnp.cos(bands * ang), -jnp.sin(bands * ang)], axis=-1)
    zp = jnp.pad(z, ((0, 0), (0, HY_FILT_HIDDEN - HY_EMB)))
    w1p = jnp.pad(w1.astype(F32), ((0, HY_FILT_HIDDEN - HY_EMB), (0, 0)))
    deltas = jnp.abs(jnp.linspace(HY_MIN_DECAY, HY_MAX_DECAY, HY_WIDTH, dtype=F32))
    window = jnp.exp(-t * deltas[None, :]) + HY_SHIFT
    win2 = jnp.concatenate([window, window], axis=-1)
    tl = min(512, seq)
    hh = HY_FILT_HIDDEN
    full = lambda r, c: pl.BlockSpec((r, c), lambda i: (0, 0))
    return pl.pallas_call(
        _hyena_filter_kernel,
        grid=(seq // tl,),
        in_specs=[pl.BlockSpec((tl, hh), lambda i: (i, 0)), full(hh, hh), full(1, hh), full(hh, hh), full(1, hh),
                  full(hh, 2 * HY_WIDTH), full(1, hh), pl.BlockSpec((tl, 2 * HY_WIDTH), lambda i: (i, 0))],
        out_specs=pl.BlockSpec((tl, 2 * HY_WIDTH), lambda i: (i, 0)),
        out_shape=jax.ShapeDtypeStruct((seq, 2 * HY_WIDTH), F32),
        compiler_params=_cparams("parallel"),
        name="hyena_filter",
    )(zp, w1p, b1.reshape(1, hh), w2, b2.reshape(1, hh), w3, freq.reshape(1, hh), win2)


def dft_matrices(seq):
    n = 2 * seq
    r = jnp.arange(seq, dtype=jnp.int32)[:, None]
    t = jnp.arange(n, dtype=jnp.int32)[None, :]
    ang = ((r * t) % n).astype(F32) * (2.0 * math.pi / n)
    c, s = jnp.cos(ang), jnp.sin(ang)
    nyq = jnp.where(t % 2 == 0, 1.0, -1.0).astype(F32)
    fwd = jnp.concatenate([c, jnp.where(r == 0, nyq, -s)], axis=0)
    ct = c[:, :seq].T
    st = s[:, :seq].T
    r_row = r.T
    inv_r = jnp.where(r_row == 0, 1.0, 2.0 * ct) / n
    inv_s = jnp.where(r_row == 0, nyq[:, :seq].T, -2.0 * st) / n
    inv = jnp.concatenate([inv_r, inv_s], axis=1)
    return fwd.astype(BF16), inv.astype(BF16)


def _kernel_dft_kernel(a_ref, khi_ref, klo_ref, o_ref):
    o_ref[...] = _dot(a_ref[...], khi_ref[...]) + _dot(a_ref[...], klo_ref[...])


def kernel_spectrum(fwd, k):
    n = fwd.shape[0]
    khi = k.astype(BF16)
    klo = (k - khi.astype(F32)).astype(BF16)
    tf = min(256, n)
    return pl.pallas_call(
        _kernel_dft_kernel,
        grid=(n // tf,),
        in_specs=[pl.BlockSpec((tf, n), lambda i: (i, 0)),
                  pl.BlockSpec((n, HY_WIDTH), lambda i: (0, 0)),
                  pl.BlockSpec((n, HY_WIDTH), lambda i: (0, 0))],
        out_specs=pl.BlockSpec((tf, HY_WIDTH), lambda i: (i, 0)),
        out_shape=jax.ShapeDtypeStruct((n, HY_WIDTH), F32),
        compiler_params=_cparams("parallel"),
        name="hyena_kernel_dft",
    )(fwd, khi, klo)


def _shift_down(u):
    rows = lax.broadcasted_iota(jnp.int32, u.shape, 0)
    return jnp.where(rows == 0, 0.0, pltpu.roll(u, 1, 0))


def _shift_up(u):
    n = u.shape[0]
    rows = lax.broadcasted_iota(jnp.int32, u.shape, 0)
    return jnp.where(rows == n - 1, 0.0, pltpu.roll(u, n - 1, 0))


def _hyena_pre_kernel(u_ref, w_ref, b_ref, vx_ref, x0_ref):
    def conv(c0):
        u = u_ref[:, c0:c0 + LANES].astype(F32)
        w = w_ref[:, c0:c0 + LANES]
        return _shift_down(u) * w[0:1] + u * w[1:2] + _shift_up(u) * w[2:3] + b_ref[:, c0:c0 + LANES]

    for c in range(0, HY_WIDTH, LANES):
        x0_ref[:, c:c + LANES] = conv(c).astype(x0_ref.dtype)
        vx_ref[:, c:c + LANES] = (conv(2 * HY_WIDTH + c) * conv(HY_WIDTH + c)).astype(vx_ref.dtype)


def hyena_pre(proj, conv_w, conv_b, bsz, seq):
    w3 = 3 * HY_WIDTH
    out = jax.ShapeDtypeStruct((bsz * seq, HY_WIDTH), BF16)
    return pl.pallas_call(
        _hyena_pre_kernel,
        grid=(bsz,),
        in_specs=[pl.BlockSpec((seq, w3), lambda b: (b, 0)),
                  pl.BlockSpec((3, w3), lambda b: (0, 0)),
                  pl.BlockSpec((1, w3), lambda b: (0, 0))],
        out_specs=[pl.BlockSpec((seq, HY_WIDTH), lambda b: (b, 0))] * 2,
        out_shape=[out, out],
        compiler_params=_cparams("parallel"),
        name="hyena_pre",
    )(proj, conv_w, conv_b.reshape(1, w3))


def _hyena_conv_kernel(vx_ref, x0_ref, ar_ref, as_ref, br_ref, bs_ref, kp_ref, kq_ref, kp2_ref, d_ref,
                       o_ref, acc_ref):
    f = pl.program_id(1)
    vx = vx_ref[...]
    r = _dot(ar_ref[...], vx)
    s = _dot(as_ref[...], vx)
    kq = kq_ref[...]
    zr = (r * kp_ref[...] - s * kq).astype(BF16)
    zs = (r * kq + s * kp2_ref[...]).astype(BF16)
    part = _dot(br_ref[...], zr) + _dot(bs_ref[...], zs)

    @pl.when(f == 0)
    def _():
        acc_ref[...] = part

    @pl.when(f > 0)
    def _():
        acc_ref[...] += part

    @pl.when(f == pl.num_programs(1) - 1)
    def _():
        y = acc_ref[...] + vx.astype(F32) * d_ref[...]
        o_ref[...] = (y * x0_ref[...].astype(F32)).astype(o_ref.dtype)


def hyena_conv(vx, x0, fwd, inv, kp, kq, kp2, d, bsz, seq):
    tf = min(256, seq)
    nf = seq // tf
    w = HY_WIDTH
    return pl.pallas_call(
        _hyena_conv_kernel,
        grid=(bsz, nf),
        in_specs=[pl.BlockSpec((seq, w), lambda b, f: (b, 0)),
                  pl.BlockSpec((seq, w), lambda b, f: (b, 0)),
                  pl.BlockSpec((tf, seq), lambda b, f: (f, 0)),
                  pl.BlockSpec((tf, seq), lambda b, f, nf=nf: (nf + f, 0)),
                  pl.BlockSpec((seq, tf), lambda b, f: (0, f)),
                  pl.BlockSpec((seq, tf), lambda b, f, nf=nf: (0, nf + f)),
                  pl.BlockSpec((tf, w), lambda b, f: (f, 0)),
                  pl.BlockSpec((tf, w), lambda b, f: (f, 0)),
                  pl.BlockSpec((tf, w), lambda b, f: (f, 0)),
                  pl.BlockSpec((1, w), lambda b, f: (0, 0))],
        out_specs=pl.BlockSpec((seq, w), lambda b, f: (b, 0)),
        out_shape=jax.ShapeDtypeStruct((bsz * seq, w), BF16),
        scratch_shapes=[pltpu.VMEM((seq, w), F32)],
        compiler_params=_cparams("parallel", "arbitrary"),
        name="hyena_conv",
    )(vx, x0, fwd, fwd, inv, inv, kp, kq, kp2, d.reshape(1, w))


def _rel_bucket(rel):
    nb = REL_BUCKETS // 2
    max_exact = nb // 2
    ret = (rel > 0).astype(jnp.int32) * nb
    n = jnp.abs(rel)
    nf = jnp.maximum(n, 1).astype(F32)
    large = max_exact + (jnp.log(nf / max_exact) / math.log(REL_MAX_DIST / max_exact)
                         * (nb - max_exact)).astype(jnp.int32)
    large = jnp.minimum(large, nb - 1)
    return ret + jnp.where(n < max_exact, n, large)


def window_bias_mask(rel_bias):
    j = jnp.arange(WA_BLOCK, dtype=jnp.int32)[:, None]
    s = jnp.arange(3 * WA_BLOCK, dtype=jnp.int32)[None, :]
    rel = (s - WA_BLOCK) - j
    bias = jnp.transpose(rel_bias.astype(F32)[_rel_bucket(rel)], (2, 0, 1))
    band = jnp.abs(rel) <= WA_WINDOW
    return jnp.where(band[None], bias, NEG)


def _window_attn_kernel(sink_ref, q_ref, k_ref, v_ref, bias_ref, o_ref):
    nb = q_ref.shape[0] // WA_BLOCK
    scale = HEAD_DIM ** -0.5
    col = lax.broadcasted_iota(jnp.int32, (WA_BLOCK, 3 * WA_BLOCK), 1)

    def block(i, carry):
        ip = jnp.maximum(i - 1, 0)
        inx = jnp.minimum(i + 1, nb - 1)
        rows = lambda j: pl.ds(pl.multiple_of(j * WA_BLOCK, WA_BLOCK), WA_BLOCK)
        qb = q_ref[rows(i), :]
        kslab = jnp.concatenate([k_ref[rows(ip), :], k_ref[rows(i), :], k_ref[rows(inx), :]], axis=0)
        vslab = jnp.concatenate([v_ref[rows(ip), :], v_ref[rows(i), :], v_ref[rows(inx), :]], axis=0)
        lo = jnp.where(i > 0, 0, WA_BLOCK)
        hi = jnp.where(i < nb - 1, 3 * WA_BLOCK, 2 * WA_BLOCK)
        valid = jnp.logical_and(col >= lo, col < hi)
        outs = []
        for h in range(WA_HEADS):
            g = h // WA_REP
            kg = kslab[:, g * HEAD_DIM:(g + 1) * HEAD_DIM]
            vg = vslab[:, g * HEAD_DIM:(g + 1) * HEAD_DIM]
            s = _dot_nt(qb[:, h * HEAD_DIM:(h + 1) * HEAD_DIM], kg) * scale + bias_ref[h]
            s = jnp.where(valid, s, NEG)
            sk = sink_ref[h]
            m = jnp.maximum(jnp.max(s, axis=-1, keepdims=True), sk)
            p = jnp.exp(s - m)
            den = jnp.sum(p, axis=-1, keepdims=True) + jnp.exp(sk - m)
            outs.append(_dot(p.astype(BF16), vg) / den)
        o_ref[rows(i), :] = jnp.concatenate(outs, axis=-1).astype(o_ref.dtype)
        return carry

    lax.fori_loop(0, nb, block, 0, unroll=2)


def window_attn(proj, sink, bias_mask, bsz, seq):
    hq = WA_HEADS * HEAD_DIM
    hkv = WA_KV_HEADS * HEAD_DIM
    q_blk = (3 * HY_WIDTH) // hq
    k_blk = (3 * HY_WIDTH + hq) // hkv
    return pl.pallas_call(
        _window_attn_kernel,
        grid=(bsz,),
        in_specs=[pl.BlockSpec(memory_space=pltpu.SMEM),
                  pl.BlockSpec((seq, hq), lambda b: (b, q_blk)),
                  pl.BlockSpec((seq, hkv), lambda b: (b, k_blk)),
                  pl.BlockSpec((seq, hkv), lambda b: (b, k_blk + 1)),
                  pl.BlockSpec((WA_HEADS, WA_BLOCK, 3 * WA_BLOCK), lambda b: (0, 0, 0))],
        out_specs=pl.BlockSpec((seq, hq), lambda b: (b, 0)),
        out_shape=jax.ShapeDtypeStruct((bsz * seq, hq), BF16),
        compiler_params=_cparams("parallel"),
        name="window_attn",
    )(sink.astype(F32), proj, proj, proj, bias_mask)


def _out_proj_kernel(x_ref, a_ref, b_ref, w_ref, o_ref):
    acc = _dot(a_ref[...], w_ref[0:HALF, :]) + _dot(b_ref[...], w_ref[HALF:, :])
    o_ref[...] = x_ref[...] + acc


def out_proj_ab(x2d, y_a, y_b, w, tm=512):
    t = x2d.shape[0]
    tm = min(tm, t)
    return pl.pallas_call(
        _out_proj_kernel,
        grid=(t // tm,),
        in_specs=[pl.BlockSpec((tm, D_MODEL), lambda i: (i, 0)),
                  pl.BlockSpec((tm, HALF), lambda i: (i, 0)),
                  pl.BlockSpec((tm, HALF), lambda i: (i, 0)),
                  pl.BlockSpec((D_MODEL, D_MODEL), lambda i: (0, 0))],
        out_specs=pl.BlockSpec((tm, D_MODEL), lambda i: (i, 0)),
        out_shape=jax.ShapeDtypeStruct(x2d.shape, F32),
        compiler_params=_cparams("parallel"),
        name="out_proj_ab",
    )(x2d, y_a, y_b, w)


def s5_discretise(a_re, a_im, log_dt, b_re, b_im, c_re, c_im):
    lam = lax.complex(a_re.astype(F32), a_im.astype(F32))
    dt = jnp.exp(log_dt.astype(F32))[..., None]
    abar = jnp.exp(lam * dt)
    bmat = lax.complex(b_re.astype(F32), b_im.astype(F32))
    bbar = ((abar - 1.0) / lam)[..., None] * bmat
    cmat = lax.complex(c_re.astype(F32), c_im.astype(F32))
    nj, gl = S5_GROUPS // SUBLANES, SUBLANES
    eye = jnp.eye(gl, dtype=F32)
    a5 = jnp.stack([abar.real, abar.imag], axis=1).reshape(2, 2, nj, 1, gl * S5_STATE)
    a5 = jnp.transpose(a5, (0, 2, 1, 3, 4))
    a5 = jnp.broadcast_to(a5, (2, nj, 2, SUBLANES, gl * S5_STATE))

    def pack_b(x):
        x = x.reshape(2, nj, gl, S5_STATE, S5_GROUP)
        y = jnp.einsum('hg,djgpc->djhcgp', eye, x)
        return y.reshape(2, nj, gl * S5_GROUP, gl * S5_STATE)

    def pack_c(x):
        x = x.reshape(2, nj, gl, S5_GROUP, S5_STATE)
        y = jnp.einsum('hg,djgcp->djgphc', eye, x)
        return y.reshape(2, nj, gl * S5_STATE, gl * S5_GROUP)

    bm = jnp.concatenate([pack_b(bbar.real), pack_b(bbar.imag)], axis=-1).astype(BF16)
    cm = jnp.concatenate([pack_c(cmat.real), -pack_c(cmat.imag)], axis=-2).astype(BF16)
    return a5, bm, cm


def _s5_kernel(u_ref, a_ref, b_ref, c_ref, y_ref, buf_ref, h_ref, *, chunk):
    d = pl.program_id(1)
    nj = S5_GROUPS // SUBLANES
    sw = SUBLANES * S5_STATE
    rows = chunk * SUBLANES

    @pl.when(pl.program_id(2) == 0)
    def _():
        h_ref[...] = jnp.zeros_like(h_ref)

    u = u_ref[...].reshape(rows, HALF).astype(BF16)
    for j in range(nj):
        buf_ref[j] = _dot(u[:, j * LANES:(j + 1) * LANES], b_ref[0, j])

    def step(s, carry):
        t = jnp.where(d == 0, s, chunk - 1 - s)
        r0 = pl.multiple_of(t * SUBLANES, SUBLANES)
        new = []
        for j in range(nj):
            hr, hi = carry[2 * j], carry[2 * j + 1]
            ar = a_ref[0, j, 0]
            ai = a_ref[0, j, 1]
            br = buf_ref[j, pl.ds(r0, SUBLANES), 0:sw]
            bi = buf_ref[j, pl.ds(r0, SUBLANES), sw:2 * sw]
            nr = ar * hr - ai * hi + br
            ni = ar * hi + ai * hr + bi
            buf_ref[j, pl.ds(r0, SUBLANES), 0:sw] = nr
            buf_ref[j, pl.ds(r0, SUBLANES), sw:2 * sw] = ni
            new += [nr, ni]
        return tuple(new)

    init = tuple(h_ref[k] for k in range(2 * nj))
    fin = lax.fori_loop(0, chunk, step, init)
    for k in range(2 * nj):
        h_ref[k] = fin[k]

    for j in range(nj):
        yj = _dot(buf_ref[j].astype(BF16), c_ref[0, j])
        y_ref[0, :, :, j * LANES:(j + 1) * LANES] = yj.reshape(chunk, SUBLANES, LANES).astype(y_ref.dtype)


def s5_scan(u_tm, a5, bm, cm, bsz, seq, chunk=64):
    chunk = min(chunk, seq)
    nc = seq // chunk
    nj = S5_GROUPS // SUBLANES
    sw = SUBLANES * S5_STATE

    def tchunk(d, i):
        return i + d * (nc - 1 - 2 * i)

    return pl.pallas_call(
        functools.partial(_s5_kernel, chunk=chunk),
        grid=(bsz // SUBLANES, 2, nc),
        in_specs=[pl.BlockSpec((chunk, SUBLANES, HALF), lambda b, d, i: (tchunk(d, i), b, 0)),
                  pl.BlockSpec((1, nj, 2, SUBLANES, sw), lambda b, d, i: (d, 0, 0, 0, 0)),
                  pl.BlockSpec((1, nj, LANES, 2 * sw), lambda b, d, i: (d, 0, 0, 0)),
                  pl.BlockSpec((1, nj, 2 * sw, LANES), lambda b, d, i: (d, 0, 0, 0))],
        out_specs=pl.BlockSpec((1, chunk, SUBLANES, HALF), lambda b, d, i: (d, tchunk(d, i), b, 0)),
        out_shape=jax.ShapeDtypeStruct((2, seq, bsz, HALF), F32),
        scratch_shapes=[pltpu.VMEM((nj, chunk * SUBLANES, 2 * sw), F32),
                        pltpu.VMEM((2 * nj, SUBLANES, sw), F32)],
        compiler_params=_cparams("parallel", "arbitrary", "arbitrary"),
        name="s5_scan",
    )(u_tm, a5, bm, cm)


def _gelu_tanh(x):
    return 0.5 * x * (1.0 + jnp.tanh(math.sqrt(2.0 / math.pi) * (x + 0.044715 * (x * x * x))))


def _out_proj_cd_kernel(x_ref, u_ref, yf_ref, yb_ref, mla_ref, d_ref, gw_ref, gb_ref, w_ref, o_ref):
    y = u_ref[...] * d_ref[...] + yf_ref[0] + yb_ref[0]
    g = _gelu_tanh(y)
    z = _dot(g.astype(BF16), gw_ref[...]) + gb_ref[...]
    y_s5 = g * jax.nn.sigmoid(z)
    acc = _dot(y_s5.astype(BF16), w_ref[0:HALF, :]) + _dot(mla_ref[...], w_ref[HALF:, :])
    o_ref[...] = x_ref[...] + acc


def out_proj_cd(x2d, u_tm2, y_tm2, y_mla, d, glu_w, glu_b, w, bsz, seq, tm=512):
    tm = min(tm, seq)
    nl = seq // tm
    row = lambda b, i: (b * nl + i, 0)
    const = lambda b, i: (0, 0)
    return pl.pallas_call(
        _out_proj_cd_kernel,
        grid=(bsz, nl),
        in_specs=[pl.BlockSpec((tm, D_MODEL), row),
                  pl.BlockSpec((tm, HALF), lambda b, i: (i, b)),
                  pl.BlockSpec((1, tm, HALF), lambda b, i: (0, i, b)),
                  pl.BlockSpec((1, tm, HALF), lambda b, i: (1, i, b)),
                  pl.BlockSpec((tm, HALF), row),
                  pl.BlockSpec((1, HALF), const),
                  pl.BlockSpec((HALF, HALF), const),
                  pl.BlockSpec((1, HALF), const),
                  pl.BlockSpec((D_MODEL, D_MODEL), const)],
        out_specs=pl.BlockSpec((tm, D_MODEL), row),
        out_shape=jax.ShapeDtypeStruct(x2d.shape, F32),
        compiler_params=_cparams("parallel", "parallel"),
        name="out_proj_cd",
    )(x2d, u_tm2, y_tm2, y_tm2, y_mla, d.reshape(1, HALF), glu_w, glu_b.reshape(1, HALF), w)


MLA_HP = 128


def mla_weights(w_uq, w_ukv):
    rq = w_uq.shape[0]
    wq = w_uq.astype(F32).reshape(rq, MLA_HEADS, MLA_NOPE + MLA_ROPE)
    half = MLA_ROPE // 2
    x1, x2 = wq[..., MLA_NOPE:MLA_NOPE + half], wq[..., MLA_NOPE + half:]
    zpad = jnp.zeros((rq, MLA_HEADS, MLA_HP - MLA_NOPE - MLA_ROPE), F32)
    wq1 = jnp.concatenate([wq, zpad], axis=-1).reshape(rq, MLA_HEADS * MLA_HP)
    wq2 = jnp.concatenate([jnp.zeros((rq, MLA_HEADS, MLA_NOPE), F32), -x2, x1, zpad], axis=-1)
    wq2 = wq2.reshape(rq, MLA_HEADS * MLA_HP)
    rk = w_ukv.shape[0]
    wkv = w_ukv.astype(F32).reshape(rk, MLA_HEADS, MLA_NOPE + MLA_V)
    wk = jnp.concatenate([wkv[..., :MLA_NOPE], jnp.zeros((rk, MLA_HEADS, MLA_HP - MLA_NOPE), F32)], axis=-1)
    wk = wk.reshape(rk, MLA_HEADS * MLA_HP)
    wv = wkv[..., MLA_NOPE:].reshape(rk, MLA_HEADS * MLA_V)
    return wq1.astype(BF16), wq2.astype(BF16), wk.astype(BF16), wv.astype(BF16)


def rope_tables(seq):
    inv = 1.0 / (ROPE_THETA ** (jnp.arange(0, MLA_ROPE, 2, dtype=F32) / MLA_ROPE))
    ang = jnp.arange(seq, dtype=F32)[:, None] * inv[None, :]
    c, s = jnp.cos(ang), jnp.sin(ang)
    ones = jnp.ones((seq, MLA_NOPE), F32)
    zpad = jnp.zeros((seq, MLA_HP - MLA_NOPE - MLA_ROPE), F32)
    cos_t = jnp.concatenate([ones, c, c, zpad], axis=-1)
    sin_t = jnp.concatenate([0.0 * ones, s, s, zpad], axis=-1)
    return cos_t, sin_t


def _mla_prep_kernel(r_ref, qg_ref, kg_ref, wq1_ref, wq2_ref, wk_ref, wv_ref, cos_ref, sin_ref,
                     q_ref, k_ref, v_ref):
    scale = (MLA_NOPE + MLA_ROPE) ** -0.5 * LOG2E
    cq = _rms(r_ref[:, 0:MLA_Q_RANK].astype(F32), qg_ref[...]).astype(BF16)
    o1 = MLA_Q_RANK + MLA_KV_RANK
    ckv = _rms(r_ref[:, MLA_Q_RANK:o1].astype(F32), kg_ref[...]).astype(BF16)
    cos_t, sin_t = cos_ref[...], sin_ref[...]
    kr = r_ref[:, o1:o1 + LANES].astype(F32) * cos_t + r_ref[:, o1 + LANES:o1 + 2 * LANES].astype(F32) * sin_t
    v_ref[...] = _dot(ckv, wv_ref[...]).astype(v_ref.dtype)
    for h in range(MLA_HEADS):
        sl = slice(h * MLA_HP, (h + 1) * MLA_HP)
        qh = _dot(cq, wq1_ref[:, sl]) * cos_t + _dot(cq, wq2_ref[:, sl]) * sin_t
        q_ref[:, sl] = (qh * scale).astype(q_ref.dtype)
        k_ref[:, sl] = (_dot(ckv, wk_ref[:, sl]) + kr).astype(k_ref.dtype)


def mla_prep(rest, q_norm, kv_norm, wq1, wq2, wk, wv, cos_t, sin_t, bsz, seq, tm=512):
    tm = min(tm, seq)
    nl = seq // tm
    t = bsz * seq
    wr = rest.shape[1]
    row = lambda b, i: (b * nl + i, 0)
    const = lambda b, i: (0, 0)
    qk = MLA_HEADS * MLA_HP
    return pl.pallas_call(
        _mla_prep_kernel,
        grid=(bsz, nl),
        in_specs=[pl.BlockSpec((tm, wr), row),
                  pl.BlockSpec((1, MLA_Q_RANK), const),
                  pl.BlockSpec((1, MLA_KV_RANK), const),
                  pl.BlockSpec((MLA_Q_RANK, qk), const),
                  pl.BlockSpec((MLA_Q_RANK, qk), const),
                  pl.BlockSpec((MLA_KV_RANK, qk), const),
                  pl.BlockSpec((MLA_KV_RANK, MLA_HEADS * MLA_V), const),
                  pl.BlockSpec((tm, MLA_HP), lambda b, i: (i, 0)),
                  pl.BlockSpec((tm, MLA_HP), lambda b, i: (i, 0))],
        out_specs=[pl.BlockSpec((tm, qk), row), pl.BlockSpec((tm, qk), row),
                   pl.BlockSpec((tm, MLA_HEADS * MLA_V), row)],
        out_shape=[jax.ShapeDtypeStruct((t, qk), BF16), jax.ShapeDtypeStruct((t, qk), BF16),
                   jax.ShapeDtypeStruct((t, MLA_HEADS * MLA_V), BF16)],
        compiler_params=_cparams("parallel", "parallel"),
        name="mla_prep",
    )(rest, q_norm.reshape(1, -1), kv_norm.reshape(1, -1), wq1, wq2, wk, wv, cos_t, sin_t)


def _mla_attn_kernel(q_ref, k_ref, v_ref, o_ref):
    v = v_ref[...]
    outs = []
    for h in range(2):
        sl = slice(h * MLA_HP, (h + 1) * MLA_HP)
        s = _dot_nt(q_ref[:, sl], k_ref[:, sl])
        m = jnp.max(s, axis=-1, keepdims=True)
        p = jnp.exp2(s - m)
        den = jnp.sum(p, axis=-1, keepdims=True)
        outs.append(_dot(p.astype(BF16), v) / den)
    lane = lax.broadcasted_iota(jnp.int32, outs[0].shape, 1)
    o_ref[...] = jnp.where(lane < MLA_V, outs[0], outs[1]).astype(o_ref.dtype)


def mla_attn(q, k, v, bsz, seq, tq=512):
    tq = min(tq, seq)
    nq = seq // tq
    return pl.pallas_call(
        _mla_attn_kernel,
        grid=(bsz, MLA_HEADS // 2, nq),
        in_specs=[pl.BlockSpec((tq, 2 * MLA_HP), lambda b, p, i: (b * nq + i, p)),
                  pl.BlockSpec((seq, 2 * MLA_HP), lambda b, p, i: (b, p)),
                  pl.BlockSpec((seq, 2 * MLA_V), lambda b, p, i: (b, p))],
        out_specs=pl.BlockSpec((tq, 2 * MLA_V), lambda b, p, i: (b * nq + i, p)),
        out_shape=jax.ShapeDtypeStruct((bsz * seq, MLA_HEADS * MLA_V), BF16),
        compiler_params=_cparams("parallel", "parallel", "parallel"),
        name="mla_attn",
    )(q, k, v)


def _cross_router_kernel(x_ref, kv_ref, gc_ref, wq_ref, wo_ref, gf_ref, wr_ref, xo_ref, hn_ref, aff_ref):
    x = x_ref[...]
    h = _rms(x, gc_ref[...]).astype(BF16)
    q = (_dot(h, wq_ref[...]) * (CA_HEAD_DIM ** -0.5 * LOG2E)).astype(BF16)
    hd = CA_HEADS * CA_HEAD_DIM
    outs = []
    for a in range(CA_HEADS):
        sl = slice(a * CA_HEAD_DIM, (a + 1) * CA_HEAD_DIM)
        s = _dot_nt(q[:, sl], kv_ref[:, sl])
        m = jnp.max(s, axis=-1, keepdims=True)
        p = jnp.exp2(s - m)
        den = jnp.sum(p, axis=-1, keepdims=True)
        outs.append((_dot(p.astype(BF16), kv_ref[:, hd + a * CA_HEAD_DIM:hd + (a + 1) * CA_HEAD_DIM]) / den))
    o = jnp.concatenate(outs, axis=-1).astype(BF16)
    xn = x + _dot(o, wo_ref[...])
    xo_ref[...] = xn
    hf = _rms(xn, gf_ref[...])
    hb = hf.astype(BF16)
    hn_ref[...] = hb
    lo = (hf - hb.astype(F32)).astype(BF16)
    logits = _dot(hb, wr_ref[0]) + (_dot(lo, wr_ref[0]) + _dot(hb, wr_ref[1]))
    lane = lax.broadcasted_iota(jnp.int32, logits.shape, 1)
    logits = jnp.where(lane < N_EXPERTS, logits, NEG)
    m = jnp.max(logits, axis=-1, keepdims=True)
    e = jnp.exp(logits - m)
    aff = e / jnp.sum(e, axis=-1, keepdims=True)
    aff_ref[...] = aff[:, 0:N_EXPERTS]


def cross_router(x2d, kv, ln_cross, w_q, w_o, ln_ffn, w_router, bsz, seq, tq=256):
    tq = min(tq, seq)
    nq = seq // tq
    t = bsz * seq
    mem = kv.shape[0] // bsz
    hd = CA_HEADS * CA_HEAD_DIM
    wr = jnp.pad(w_router.astype(F32), ((0, 0), (0, LANES - N_EXPERTS)))
    wr_hi = wr.astype(BF16)
    wr2 = jnp.stack([wr_hi, (wr - wr_hi.astype(F32)).astype(BF16)])
    row = lambda b, i: (b * nq + i, 0)
    const = lambda b, i: (0, 0)
    return pl.pallas_call(
        _cross_router_kernel,
        grid=(bsz, nq),
        in_specs=[pl.BlockSpec((tq, D_MODEL), row),
                  pl.BlockSpec((mem, 2 * hd), lambda b, i: (b, 0)),
                  pl.BlockSpec((1, D_MODEL), const),
                  pl.BlockSpec((D_MODEL, hd), const),
                  pl.BlockSpec((hd, D_MODEL), const),
                  pl.BlockSpec((1, D_MODEL), const),
                  pl.BlockSpec((2, D_MODEL, LANES), lambda b, i: (0, 0, 0))],
        out_specs=[pl.BlockSpec((tq, D_MODEL), row), pl.BlockSpec((tq, D_MODEL), row),
                   pl.BlockSpec((tq, N_EXPERTS), row)],
        out_shape=[jax.ShapeDtypeStruct((t, D_MODEL), F32), jax.ShapeDtypeStruct((t, D_MODEL), BF16),
                   jax.ShapeDtypeStruct((t, N_EXPERTS), F32)],
        compiler_params=_cparams("parallel", "parallel"),
        name="cross_router",
    )(x2d, kv, ln_cross.reshape(1, -1), w_q, w_o, ln_ffn.reshape(1, -1), wr2)


def _expert_ffn_kernel(x_ref, g_ref, wg_ref, wu_ref, wd_ref, o_ref, hid_ref, *, tf):
    x = x_ref[0]
    for c0 in range(0, D_EXPERT, tf):
        a = _dot(x, wg_ref[0, :, c0:c0 + tf])
        u = _dot(x, wu_ref[0, :, c0:c0 + tf])
        hid_ref[:, c0:c0 + tf] = (a * jax.nn.sigmoid(a) * u).astype(BF16)
    o_ref[0] = (_dot(hid_ref[...], wd_ref[0]) * g_ref[0]).astype(o_ref.dtype)


def expert_ffn(xe, gate, w_gate, w_up, w_down, tm=1024, tf=512):
    e, cap, _ = xe.shape
    tm = min(tm, cap)
    return pl.pallas_call(
        functools.partial(_expert_ffn_kernel, tf=tf),
        grid=(e, cap // tm),
        in_specs=[pl.BlockSpec((1, tm, D_MODEL), lambda e, m: (e, m, 0)),
                  pl.BlockSpec((1, tm, 1), lambda e, m: (e, m, 0)),
                  pl.BlockSpec((1, D_MODEL, D_EXPERT), lambda e, m: (e, 0, 0)),
                  pl.BlockSpec((1, D_MODEL, D_EXPERT), lambda e, m: (e, 0, 0)),
                  pl.BlockSpec((1, D_EXPERT, D_MODEL), lambda e, m: (e, 0, 0))],
        out_specs=pl.BlockSpec((1, tm, D_MODEL), lambda e, m: (e, m, 0)),
        out_shape=jax.ShapeDtypeStruct((e, cap, D_MODEL), BF16),
        scratch_shapes=[pltpu.VMEM((tm, D_EXPERT), BF16)],
        compiler_params=_cparams("parallel", "arbitrary"),
        name="expert_ffn",
    )(xe, gate, w_gate, w_up, w_down)


ROUTE_GROUP = SUBLANES


def _route_thr_kernel(a_ref, thr_ref, *, cap):
    bits = pltpu.bitcast(a_ref[...], jnp.int32)

    def body(i, lo):
        cand = lo | jnp.left_shift(jnp.int32(1), 30 - i)
        cnt = jnp.sum(jnp.where(bits >= cand, 1.0, 0.0), axis=1, keepdims=True)
        return jnp.where(cnt >= cap, cand, lo)

    thr_ref[...] = lax.fori_loop(0, 31, body, jnp.zeros(thr_ref.shape, jnp.int32))


def _prefix_rows(m, upper, lower):
    mb = m.astype(BF16)
    incl = _dot(mb, upper)
    tot = jnp.broadcast_to(incl[:, LANES - 1:LANES], incl.shape).astype(BF16)
    return incl - m + _dot(lower, tot)


def _route_mask_kernel(thr_ref, a_ref, upper_ref, lower_ref, sel_ref, pos_ref, *, cap):
    e = pl.program_id(0)
    thr = thr_ref[e]
    bits = pltpu.bitcast(a_ref[0], jnp.int32)
    gt = jnp.where(bits > thr, 1.0, 0.0)
    eq = jnp.where(bits == thr, 1.0, 0.0)
    need = cap - jnp.sum(jnp.sum(gt, axis=1, keepdims=True), axis=0, keepdims=True)
    eq_rank = _prefix_rows(eq, upper_ref[...], lower_ref[...])
    sel = gt + jnp.where(eq_rank < need, eq, 0.0)
    sel_ref[0] = sel
    pos_ref[0] = _prefix_rows(sel, upper_ref[...], lower_ref[...]).astype(jnp.int32)


def _route_compact_kernel(glo_ref, ghi_ref, a_ref, pos_ref, idx_ref, gate_ref):
    e = pl.program_id(0)
    nc = idx_ref.shape[1]
    gw = ROUTE_GROUP * LANES
    slot0 = lax.broadcasted_iota(jnp.int32, (LANES, LANES), 0)
    lane = lax.broadcasted_iota(jnp.int32, (1, gw), 1)
    zeros = jnp.zeros((2 * SUBLANES - 5, gw), F32)

    def chunk(c, carry):
        slot = slot0 + c * LANES

        def group(g, acc):
            r0 = pl.multiple_of(g * ROUTE_GROUP, ROUTE_GROUP)
            pos = pos_ref[0, pl.ds(r0, ROUTE_GROUP), :]
            aff = a_ref[0, pl.ds(r0, ROUTE_GROUP), :]
            hit = jnp.concatenate(
                [jnp.where(pos[j:j + 1] == slot, 1.0, 0.0) for j in range(ROUTE_GROUP)],
                axis=1).astype(BF16)
            arow = jnp.concatenate([aff[j:j + 1] for j in range(ROUTE_GROUP)], axis=1)
            tok = lane + g * gw
            g0 = arow.astype(BF16).astype(F32)
            r1 = arow - g0
            g1 = r1.astype(BF16).astype(F32)
            g2 = r1 - g1
            lhs = jnp.concatenate([(tok >> 8).astype(F32), (tok & 255).astype(F32), g0, g1, g2, zeros], axis=0)
            return acc + _dot_nt(lhs.astype(BF16), hit)

        acc = lax.fori_loop(glo_ref[e * nc + c], ghi_ref[e * nc + c] + 1, group,
                            jnp.zeros((2 * SUBLANES, LANES), F32))
        idx_ref[0, pl.ds(c, 1), :] = (acc[0:1] * 256.0 + acc[1:2]).astype(jnp.int32)
        gate_ref[0, pl.ds(c, 1), :] = (acc[2:3] + acc[3:4]) + acc[4:5]
        return carry

    lax.fori_loop(0, nc, chunk, 0)


def route_tokens(aff, cap):
    t, ne = aff.shape
    rows = t // LANES
    nc = cap // LANES
    aff_t = aff.T
    thr = pl.pallas_call(
        functools.partial(_route_thr_kernel, cap=cap),
        out_shape=jax.ShapeDtypeStruct((ne, 1), jnp.int32),
        compiler_params=pltpu.CompilerParams(vmem_limit_bytes=V7X_VMEM_LIMIT_BYTES),
        name="route_threshold",
    )(aff_t)
    aff3 = aff_t.reshape(ne, rows, LANES)
    ii = jnp.arange(LANES)
    upper = (ii[:, None] <= ii[None, :]).astype(BF16)
    rr = jnp.arange(rows)
    lower = (rr[None, :] < rr[:, None]).astype(BF16)
    blk = pl.BlockSpec((1, rows, LANES), lambda e: (e, 0, 0))
    sel, pos = pl.pallas_call(
        functools.partial(_route_mask_kernel, cap=cap),
        grid=(ne,),
        in_specs=[pl.BlockSpec(memory_space=pltpu.SMEM), blk,
                  pl.BlockSpec((LANES, LANES), lambda e: (0, 0)),
                  pl.BlockSpec((rows, rows), lambda e: (0, 0))],
        out_specs=[blk, blk],
        out_shape=[jax.ShapeDtypeStruct((ne, rows, LANES), F32), jax.ShapeDtypeStruct((ne, rows, LANES), jnp.int32)],
        compiler_params=_cparams("parallel"),
        name="route_mask",
    )(thr.reshape(ne), aff3, upper, lower)
    gsz = ROUTE_GROUP
    first = pos[:, ::gsz, 0]
    starts = jnp.arange(nc, dtype=jnp.int32) * LANES
    glo = jnp.sum(first[:, None, :] <= starts[None, :, None], axis=-1, dtype=jnp.int32) - 1
    ghi = jnp.sum(first[:, None, :] <= (starts + (LANES - 1))[None, :, None], axis=-1, dtype=jnp.int32) - 1
    posm = jnp.where(sel > 0, pos, -1)
    out_blk = pl.BlockSpec((1, nc, LANES), lambda e, *_: (e, 0, 0))
    grid_spec = pltpu.PrefetchScalarGridSpec(
        num_scalar_prefetch=2,
        grid=(ne,),
        in_specs=[pl.BlockSpec((1, rows, LANES), lambda e, *_: (e, 0, 0))] * 2,
        out_specs=[out_blk, out_blk],
    )
    idx, gate = pl.pallas_call(
        _route_compact_kernel,
        grid_spec=grid_spec,
        out_shape=[jax.ShapeDtypeStruct((ne, nc, LANES), jnp.int32), jax.ShapeDtypeStruct((ne, nc, LANES), F32)],
        compiler_params=_cparams("parallel"),
        name="route_compact",
    )(glo.reshape(-1), ghi.reshape(-1), aff3, posm)
    return idx.reshape(ne, cap), gate.reshape(ne, cap), pos.reshape(ne, t), posm.reshape(ne, t)


COMBINE_ROWS = 128
COMBINE_ALIGN = 16


def _combine_kernel(wst_ref, nex_ref, x_ref, pos_ref, ye_hbm, o_ref, ybuf, sem, xbuf, xsem, *, tb, cap):
    b = pl.program_id(0)
    nb = pl.num_programs(0)
    ne = N_EXPERTS
    w = COMBINE_ROWS
    slot = b % 2
    lane = lax.broadcasted_iota(jnp.int32, (1, w), 1)

    def window(bb, sl, e):
        st = pl.multiple_of(wst_ref[bb * ne + e], COMBINE_ALIGN)
        return pltpu.make_async_copy(ye_hbm.at[e, pl.ds(st, w), :], ybuf.at[sl, pl.ds(e * w, w), :], sem.at[sl, e])

    def start_all(bb, sl):
        for e in range(ne):
            window(bb, sl, e).start()

    @pl.when(b == 0)
    def _():
        start_all(b, slot)

    @pl.when(b + 1 < nb)
    def _():
        start_all(b + 1, 1 - slot)

    pos = pos_ref[...]
    cols = [jnp.broadcast_to(pos[:, e:e + 1], (tb, w)) for e in range(ne)]
    onehot = jnp.concatenate(
        [jnp.where(cols[e] - wst_ref[b * ne + e] == lane, 1.0, 0.0) for e in range(ne)], axis=1).astype(BF16)
    for e in range(ne):
        window(b, slot, e).wait()
    o_ref[...] = x_ref[...] + _dot(onehot, ybuf[slot])

    for e in range(ne):
        def extra(k, carry, e=e):
            first = wst_ref[b * ne + e] + k * w
            st = pl.multiple_of(jnp.minimum(first, cap - w), COMBINE_ALIGN)
            cp = pltpu.make_async_copy(ye_hbm.at[e, pl.ds(st, w), :], xbuf, xsem.at[0])
            cp.start()
            cp.wait()
            hit = jnp.logical_and(cols[e] - st == lane, cols[e] >= first)
            o_ref[...] += _dot(jnp.where(hit, 1.0, 0.0).astype(BF16), xbuf[...])
            return carry

        lax.fori_loop(1, nex_ref[b * ne + e] + 1, extra, 0)


def moe_combine(x2d, ye, pos, posm, tb=512):
    t = x2d.shape[0]
    ne, cap = ye.shape[0], ye.shape[1]
    tb = min(tb, t)
    nb = t // tb
    w = COMBINE_ROWS
    lo = pos[:, ::tb]
    hi = jnp.concatenate([lo[:, 1:], jnp.full((ne, 1), cap, jnp.int32)], axis=1)
    wst = jnp.minimum((lo // COMBINE_ALIGN) * COMBINE_ALIGN, cap - w)
    nex = jnp.maximum(hi - (wst + w) + (w - 1), 0) // w
    posm = posm.T
    grid_spec = pltpu.PrefetchScalarGridSpec(
        num_scalar_prefetch=2,
        grid=(nb,),
        in_specs=[pl.BlockSpec((tb, D_MODEL), lambda b, *_: (b, 0)),
                  pl.BlockSpec((tb, ne), lambda b, *_: (b, 0)),
                  pl.BlockSpec(memory_space=pl.ANY)],
        out_specs=pl.BlockSpec((tb, D_MODEL), lambda b, *_: (b, 0)),
        scratch_shapes=[pltpu.VMEM((2, ne * w, D_MODEL), BF16),
                        pltpu.SemaphoreType.DMA((2, ne)),
                        pltpu.VMEM((w, D_MODEL), BF16),
                        pltpu.SemaphoreType.DMA((1,))],
    )
    return pl.pallas_call(
        functools.partial(_combine_kernel, tb=tb, cap=cap),
        grid_spec=grid_spec,
        out_shape=jax.ShapeDtypeStruct(x2d.shape, F32),
        compiler_params=_cparams("arbitrary"),
        name="moe_combine",
    )(wst.T.reshape(-1), nex.T.reshape(-1), x2d, posm, ye)


def _final_norm_kernel(x_ref, g_ref, o_ref):
    o_ref[...] = _rms(x_ref[...], g_ref[...])


def final_norm(x2d, g, tm=1024):
    t = x2d.shape[0]
    tm = min(tm, t)
    return pl.pallas_call(
        _final_norm_kernel,
        grid=(t // tm,),
        in_specs=[pl.BlockSpec((tm, D_MODEL), lambda i: (i, 0)), pl.BlockSpec((1, D_MODEL), lambda i: (0, 0))],
        out_specs=pl.BlockSpec((tm, D_MODEL), lambda i: (i, 0)),
        out_shape=jax.ShapeDtypeStruct(x2d.shape, F32),
        compiler_params=_cparams("parallel"),
        name="final_norm",
    )(x2d, g.reshape(1, -1))


def mixer_ab(x2d, e, p, shared, bsz, seq):
    proj = norm_proj(x2d, p['ln_mix_l'], p['w_in_ab'][e], bsz, seq, [(0, AB_IN, BF16, False)])[0]
    vx, x0 = hyena_pre(proj, p['hy_conv_w'][e], p['hy_conv_b'][e], bsz, seq)
    kp, kq, kp2 = shared['hy_spec'][e]
    y_hy = hyena_conv(vx, x0, shared['dft_fwd'], shared['dft_inv'], kp, kq, kp2, p['hy_d'][e], bsz, seq)
    y_wa = window_attn(proj, p['attn_sink'][e], shared['wa_bias'], bsz, seq)
    return out_proj_ab(x2d, y_hy, y_wa, p['w_out_ab'][e])


def mixer_cd(x2d, o, p, shared, bsz, seq):
    u_tm2, rest = norm_proj(x2d, p['ln_mix_l'], shared['w_in_cd'][o], bsz, seq,
                            [(0, HALF, F32, True), (HALF, CD_PAD - HALF, BF16, False)])
    a5, bm, cm = shared['s5'][o]
    y_tm = s5_scan(u_tm2.reshape(seq, bsz, HALF), a5, bm, cm, bsz, seq)
    wq1, wq2, wk, wv = shared['mla_w'][o]
    q, k, v = mla_prep(rest, p['mla_q_norm'][o], p['mla_kv_norm'][o], wq1, wq2, wk, wv,
                       shared['rope_cos'], shared['rope_sin'], bsz, seq)
    y_mla = mla_attn(q, k, v, bsz, seq)
    return out_proj_cd(x2d, u_tm2, y_tm.reshape(2, seq, bsz * HALF), y_mla, p['s5_d'][o],
                       p['s5_glu_w'][o], p['s5_glu_b'][o], p['w_out_cd'][o], bsz, seq)


def ec_moe(x2d, hn, aff, w_gate, w_up, w_down):
    t = x2d.shape[0]
    cap = EC_CAPACITY_FACTOR * t // N_EXPERTS
    idx, gate, pos, posm = route_tokens(aff, cap)
    xe = hn[idx]
    ye = expert_ffn(xe, gate[..., None], w_gate, w_up, w_down)
    return moe_combine(x2d, ye, pos, posm)


def prepare_shared(p, seq):
    sh = {}
    sh['wa_bias'] = window_bias_mask(p['rel_bias'])
    sh['dft_fwd'], sh['dft_inv'] = dft_matrices(seq)
    sh['rope_cos'], sh['rope_sin'] = rope_tables(seq)
    specs = []
    for e in range(p['w_in_ab'].shape[0]):
        h = hyena_filters(seq, p['hy_filt_w1'][e], p['hy_filt_b1'][e], p['hy_filt_w2'][e], p['hy_filt_b2'][e],
                          p['hy_filt_w3'][e], p['hy_filt_freq'][e])
        h_fwd, h_bwd = h[:, :HY_WIDTH], h[:, HY_WIDTH:]
        k = jnp.concatenate([h_fwd, jnp.zeros_like(h_fwd[:1]), h_bwd[:0:-1]], axis=0)
        kf = kernel_spectrum(sh['dft_fwd'], k)
        k_r, k_s = kf[:seq], kf[seq:]
        specs.append((k_r, k_s.at[0].set(0.0), k_r.at[0].set(k_s[0])))
    sh['hy_spec'] = specs
    s5, mla_w, w_in_cd = [], [], []
    for o in range(p['w_in_cd'].shape[0]):
        s5.append(s5_discretise(p['s5_a_re'][o], p['s5_a_im'][o], p['s5_log_dt'][o], p['s5_b_re'][o],
                                p['s5_b_im'][o], p['s5_c_re'][o], p['s5_c_im'][o]))
        mla_w.append(mla_weights(p['mla_w_uq'][o], p['mla_w_ukv'][o]))
        w = p['w_in_cd'][o].astype(F32)
        o2 = HALF + MLA_Q_RANK + MLA_KV_RANK
        kr = w[:, o2:o2 + MLA_ROPE]
        half = MLA_ROPE // 2
        kr_rot = jnp.concatenate([-kr[:, half:], kr[:, :half]], axis=1)
        z64 = jnp.zeros((D_MODEL, MLA_NOPE), F32)
        z32 = jnp.zeros((D_MODEL, MLA_HP - MLA_NOPE - MLA_ROPE), F32)
        w_in_cd.append(jnp.concatenate([w[:, :o2], z64, kr, z32, z64, kr_rot, z32], axis=1).astype(BF16))
    sh['s5'], sh['mla_w'], sh['w_in_cd'] = s5, mla_w, w_in_cd
    return sh


def run_trunk(x, mem, p, shared):
    bsz, seq, _ = x.shape
    x2d = x.reshape(bsz * seq, D_MODEL)
    mem2d = mem.reshape(bsz * mem.shape[1], D_MODEL)
    for layer in range(DEPTH):
        pl_ = dict(p, ln_mix_l=p['ln_mix'][layer])
        if layer % 2 == 0:
            x2d = mixer_ab(x2d, layer // 2, pl_, shared, bsz, seq)
        else:
            x2d = mixer_cd(x2d, layer // 2, pl_, shared, bsz, seq)
        kv = norm_proj(mem2d, p['ln_mem'][layer], p['ca_w_kv'][layer], bsz, mem.shape[1],
                       [(0, 2 * CA_HEADS * CA_HEAD_DIM, BF16, False)])[0]
        x2d, hn, aff = cross_router(x2d, kv, p['ln_cross'][layer], p['ca_w_q'][layer], p['ca_w_o'][layer],
                                    p['ln_ffn'][layer], p['moe_w_router'][layer], bsz, seq)
        x2d = ec_moe(x2d, hn, aff, p['moe_w_gate'][layer], p['moe_w_up'][layer], p['moe_w_down'][layer])
    return final_norm(x2d, p['ln_final']).reshape(bsz, seq, D_MODEL)


_BF16_WEIGHTS = ('w_in_ab', 'w_out_ab', 'w_out_cd', 's5_glu_w', 'ca_w_q', 'ca_w_kv', 'ca_w_o',
                 'moe_w_gate', 'moe_w_up', 'moe_w_down')


def kernel(x_prompt, x_sample, mem_prompt, mem_sample, ln_mix, ln_cross, ln_mem, ln_ffn, ln_final, rel_bias, w_in_ab, w_out_ab, hy_conv_w, hy_conv_b, hy_filt_w1, hy_filt_b1, hy_filt_w2, hy_filt_b2, hy_filt_w3, hy_filt_freq, hy_d, attn_sink, w_in_cd, w_out_cd, s5_a_re, s5_a_im, s5_log_dt, s5_b_re, s5_b_im, s5_c_re, s5_c_im, s5_d, s5_glu_w, s5_glu_b, mla_q_norm, mla_w_uq, mla_kv_norm, mla_w_ukv, ca_w_q, ca_w_kv, ca_w_o, moe_w_router, moe_w_gate, moe_w_up, moe_w_down):
    p = dict(ln_mix=ln_mix, ln_cross=ln_cross, ln_mem=ln_mem, ln_ffn=ln_ffn, ln_final=ln_final,
             rel_bias=rel_bias, w_in_ab=w_in_ab, w_out_ab=w_out_ab, hy_conv_w=hy_conv_w,
             hy_conv_b=hy_conv_b, hy_filt_w1=hy_filt_w1, hy_filt_b1=hy_filt_b1,
             hy_filt_w2=hy_filt_w2, hy_filt_b2=hy_filt_b2, hy_filt_w3=hy_filt_w3,
             hy_filt_freq=hy_filt_freq, hy_d=hy_d, attn_sink=attn_sink, w_in_cd=w_in_cd,
             w_out_cd=w_out_cd, s5_a_re=s5_a_re, s5_a_im=s5_a_im, s5_log_dt=s5_log_dt,
             s5_b_re=s5_b_re, s5_b_im=s5_b_im, s5_c_re=s5_c_re, s5_c_im=s5_c_im, s5_d=s5_d,
             s5_glu_w=s5_glu_w, s5_glu_b=s5_glu_b, mla_q_norm=mla_q_norm, mla_w_uq=mla_w_uq,
             mla_kv_norm=mla_kv_norm, mla_w_ukv=mla_w_ukv, ca_w_q=ca_w_q, ca_w_kv=ca_w_kv,
             ca_w_o=ca_w_o, moe_w_router=moe_w_router, moe_w_gate=moe_w_gate,
             moe_w_up=moe_w_up, moe_w_down=moe_w_down)
    assert x_prompt.shape[1] == x_sample.shape[1]
    shared = prepare_shared(p, x_prompt.shape[1])
    for name in _BF16_WEIGHTS:
        p[name] = p[name].astype(BF16)
    y_prompt = run_trunk(x_prompt, mem_prompt, p, shared)
    y_sample = run_trunk(x_sample, mem_sample, p, shared)
    return (y_prompt, y_sample)
```

```python
import functools
import math

import jax
import jax.numpy as jnp
import numpy as np
from jax import lax
from jax.experimental import pallas as pl
from jax.experimental.pallas import tpu as pltpu

D_MODEL = 1024
DEPTH = 4
HALF = 512
HEAD_DIM = 64
EPS = 1e-6
NEG = -1e30

HY_WIDTH = HALF
HY_EMB = 33
HY_BANDS = (HY_EMB - 1) // 2
HY_FILT_HIDDEN = 64
HY_DECAY_TARGET = 1e-2
HY_FAST = 0.3
HY_SLOW = 1.5
HY_MIN_DECAY = math.log(HY_DECAY_TARGET) / HY_SLOW
HY_MAX_DECAY = math.log(HY_DECAY_TARGET) / HY_FAST
HY_SHIFT = 0.05

WA_HEADS = 8
WA_KV_HEADS = 2
WA_REP = 4
WA_WINDOW = 128
WA_BLOCK = 128
REL_BUCKETS = 32
REL_MAX_DIST = 128

S5_GROUP = 16
S5_GROUPS = 32
S5_STATE = 64

MLA_HEADS = 8
MLA_NOPE = 64
MLA_ROPE = 32
MLA_V = 64
MLA_Q_RANK = 256
MLA_KV_RANK = 128
ROPE_THETA = 10000.0

CA_HEADS = 4
CA_HEAD_DIM = 128

N_EXPERTS = 16
EC_CAPACITY_FACTOR = 2
D_EXPERT = 2048

AB_IN = 3 * HY_WIDTH + (WA_HEADS + 2 * WA_KV_HEADS) * HEAD_DIM
CD_PAD = 1152

V7X_VMEM_LIMIT_BYTES = 56 * 1024 * 1024
LANES = 128
SUBLANES = 8
LOG2E = math.log2(math.e)

BF16 = jnp.bfloat16
F32 = jnp.float32


def _cparams(*sem):
    return pltpu.CompilerParams(dimension_semantics=sem, vmem_limit_bytes=V7X_VMEM_LIMIT_BYTES)


def _dot(a, b):
    return jnp.dot(a, b, preferred_element_type=F32)


def _dot_nt(a, b):
    return lax.dot_general(a, b, (((1,), (1,)), ((), ())), preferred_element_type=F32)


def _rms(xf, g):
    return xf * lax.rsqrt(jnp.mean(xf * xf, axis=-1, keepdims=True) + EPS) * g


def _norm_proj_kernel(x_ref, g_ref, w_ref, *out_refs, splits):
    hn = _rms(x_ref[...].astype(F32), g_ref[...]).astype(BF16)
    for o_ref, (start, width) in zip(out_refs, splits):
        for c0 in range(0, width, 512):
            cw = min(512, width - c0)
            o_ref[:, c0:c0 + cw] = _dot(hn, w_ref[:, start + c0:start + c0 + cw]).astype(o_ref.dtype)


def norm_proj(x2d, gain, w, bsz, seq, outs, tm=512):
    tm = min(tm, seq)
    nl = seq // tm
    n = w.shape[1]
    out_shapes, out_specs, splits = [], [], []
    for start, width, dtype, time_major in outs:
        splits.append((start, width))
        if time_major:
            out_shapes.append(jax.ShapeDtypeStruct((seq, bsz * width), dtype))
            out_specs.append(pl.BlockSpec((tm, width), lambda b, i: (i, b)))
        else:
            out_shapes.append(jax.ShapeDtypeStruct((bsz * seq, width), dtype))
            out_specs.append(pl.BlockSpec((tm, width), lambda b, i, nl=nl: (b * nl + i, 0)))
    return pl.pallas_call(
        functools.partial(_norm_proj_kernel, splits=tuple(splits)),
        grid=(bsz, nl),
        in_specs=[pl.BlockSpec((tm, D_MODEL), lambda b, i, nl=nl: (b * nl + i, 0)),
                  pl.BlockSpec((1, D_MODEL), lambda b, i: (0, 0)),
                  pl.BlockSpec((D_MODEL, n), lambda b, i: (0, 0))],
        out_specs=out_specs,
        out_shape=out_shapes,
        compiler_params=_cparams("parallel", "parallel"),
        name="norm_proj",
    )(x2d, gain.reshape(1, D_MODEL), w)


def _hyena_filter_kernel(z_ref, w1_ref, b1_ref, w2_ref, b2_ref, w3_ref, fr_ref, win_ref, o_ref):
    hp = lax.Precision.HIGHEST
    fr = fr_ref[...]
    h = jnp.sin(fr * (jnp.dot(z_ref[...], w1_ref[...], precision=hp, preferred_element_type=F32) + b1_ref[...]))
    h = jnp.sin(fr * (jnp.dot(h, w2_ref[...], precision=hp, preferred_element_type=F32) + b2_ref[...]))
    h = jnp.dot(h, w3_ref[...], precision=hp, preferred_element_type=F32)
    o_ref[...] = h * win_ref[...]


def hyena_filters(seq, w1, b1, w2, b2, w3, freq):
    t = jnp.linspace(0.0, 1.0, seq, dtype=F32)[:, None]
    ang = 2.0 * math.pi * jnp.arange(seq, dtype=F32)[:, None] / seq
    bands = jnp.linspace(1e-4, HY_BANDS - 1, HY_BANDS, dtype=F32)[None, :]
    z = jnp.concatenate([t, jnp.cos(bands * ang), -jnp.sin(bands * ang)], axis=-1)
    zp = jnp.pad(z, ((0, 0), (0, HY_FILT_HIDDEN - HY_EMB)))
    w1p = jnp.pad(w1.astype(F32), ((0, HY_FILT_HIDDEN - HY_EMB), (0, 0)))
    deltas = jnp.abs(jnp.linspace(HY_MIN_DECAY, HY_MAX_DECAY, HY_WIDTH, dtype=F32))
    window = jnp.exp(-t * deltas[None, :]) + HY_SHIFT
    win2 = jnp.concatenate([window, window], axis=-1)
    tl = min(512, seq)
    hh = HY_FILT_HIDDEN
    full = lambda r, c: pl.BlockSpec((r, c), lambda i: (0, 0))
    return pl.pallas_call(
        _hyena_filter_kernel,
        grid=(seq // tl,),
        in_specs=[pl.BlockSpec((tl, hh), lambda i: (i, 0)), full(hh, hh), full(1, hh), full(hh, hh), full(1, hh),
                  full(hh, 2 * HY_WIDTH), full(1, hh), pl.BlockSpec((tl, 2 * HY_WIDTH), lambda i: (i, 0))],
        out_specs=pl.BlockSpec((tl, 2 * HY_WIDTH), lambda i: (i, 0)),
        out_shape=jax.ShapeDtypeStruct((seq, 2 * HY_WIDTH), F32),
        compiler_params=_cparams("parallel"),
        name="hyena_filter",
    )(zp, w1p, b1.reshape(1, hh), w2, b2.reshape(1, hh), w3, freq.reshape(1, hh), win2)


def dft_matrices(seq):
    n = 2 * seq
    r = jnp.arange(seq, dtype=jnp.int32)[:, None]
    t = jnp.arange(n, dtype=jnp.int32)[None, :]
    ang = ((r * t) % n).astype(F32) * (2.0 * math.pi / n)
    c, s = jnp.cos(ang), jnp.sin(ang)
    nyq = jnp.where(t % 2 == 0, 1.0, -1.0).astype(F32)
    fwd = jnp.concatenate([c, jnp.where(r == 0, nyq, -s)], axis=0)
    ct = c[:, :seq].T
    st = s[:, :seq].T
    r_row = r.T
    inv_r = jnp.where(r_row == 0, 1.0, 2.0 * ct) / n
    inv_s = jnp.where(r_row == 0, nyq[:, :seq].T, -2.0 * st) / n
    inv = jnp.concatenate([inv_r, inv_s], axis=1)
    return fwd.astype(BF16), inv.astype(BF16)


def _kernel_dft_kernel(a_ref, khi_ref, klo_ref, o_ref):
    o_ref[...] = _dot(a_ref[...], khi_ref[...]) + _dot(a_ref[...], klo_ref[...])


def kernel_spectrum(fwd, k):
    n = fwd.shape[0]
    khi = k.astype(BF16)
    klo = (k - khi.astype(F32)).astype(BF16)
    tf = min(256, n)
    return pl.pallas_call(
        _kernel_dft_kernel,
        grid=(n // tf,),
        in_specs=[pl.BlockSpec((tf, n), lambda i: (i, 0)),
                  pl.BlockSpec((n, HY_WIDTH), lambda i: (0, 0)),
                  pl.BlockSpec((n, HY_WIDTH), lambda i: (0, 0))],
        out_specs=pl.BlockSpec((tf, HY_WIDTH), lambda i: (i, 0)),
        out_shape=jax.ShapeDtypeStruct((n, HY_WIDTH), F32),
        compiler_params=_cparams("parallel"),
        name="hyena_kernel_dft",
    )(fwd, khi, klo)


def _shift_down(u):
    rows = lax.broadcasted_iota(jnp.int32, u.shape, 0)
    return jnp.where(rows == 0, 0.0, pltpu.roll(u, 1, 0))


def _shift_up(u):
    n = u.shape[0]
    rows = lax.broadcasted_iota(jnp.int32, u.shape, 0)
    return jnp.where(rows == n - 1, 0.0, pltpu.roll(u, n - 1, 0))


def _hyena_pre_kernel(u_ref, w_ref, b_ref, vx_ref, x0_ref):
    def conv(c0):
        u = u_ref[:, c0:c0 + LANES].astype(F32)
        w = w_ref[:, c0:c0 + LANES]
        return _shift_down(u) * w[0:1] + u * w[1:2] + _shift_up(u) * w[2:3] + b_ref[:, c0:c0 + LANES]

    for c in range(0, HY_WIDTH, LANES):
        x0_ref[:, c:c + LANES] = conv(c).astype(x0_ref.dtype)
        vx_ref[:, c:c + LANES] = (conv(2 * HY_WIDTH + c) * conv(HY_WIDTH + c)).astype(vx_ref.dtype)


def hyena_pre(proj, conv_w, conv_b, bsz, seq):
    w3 = 3 * HY_WIDTH
    out = jax.ShapeDtypeStruct((bsz * seq, HY_WIDTH), BF16)
    return pl.pallas_call(
        _hyena_pre_kernel,
        grid=(bsz,),
        in_specs=[pl.BlockSpec((seq, w3), lambda b: (b, 0)),
                  pl.BlockSpec((3, w3), lambda b: (0, 0)),
                  pl.BlockSpec((1, w3), lambda b: (0, 0))],
        out_specs=[pl.BlockSpec((seq, HY_WIDTH), lambda b: (b, 0))] * 2,
        out_shape=[out, out],
        compiler_params=_cparams("parallel"),
        name="hyena_pre",
    )(proj, conv_w, conv_b.reshape(1, w3))


def _hyena_conv_kernel(vx_ref, x0_ref, ar_ref, as_ref, inv_ref, kp_ref, kq_ref, kp2_ref, d_ref,
                       o_ref, z_ref, *, nf, tf):
    step = pl.program_id(1)
    half = nf * tf

    @pl.when(step < nf)
    def _():
        vx = vx_ref[...]
        r = _dot(ar_ref[...], vx)
        s = _dot(as_ref[...], vx)
        kq = kq_ref[...]
        f0 = pl.multiple_of(step * tf, tf)
        z_ref[pl.ds(f0, tf), :] = (r * kp_ref[...] - s * kq).astype(BF16)
        z_ref[pl.ds(half + f0, tf), :] = (r * kq + s * kp2_ref[...]).astype(BF16)

    @pl.when(step >= nf)
    def _():
        t0 = pl.multiple_of((step - nf) * tf, tf)
        y = _dot(inv_ref[...], z_ref[...]) + vx_ref[pl.ds(t0, tf), :].astype(F32) * d_ref[...]
        o_ref[pl.ds(t0, tf), :] = (y * x0_ref[pl.ds(t0, tf), :].astype(F32)).astype(o_ref.dtype)


def hyena_conv(vx, x0, fwd, inv, kp, kq, kp2, d, bsz, seq):
    tf = min(256, seq)
    nf = seq // tf
    w = HY_WIDTH
    fwd_tile = lambda b, s: (jnp.minimum(s, nf - 1), 0)
    return pl.pallas_call(
        functools.partial(_hyena_conv_kernel, nf=nf, tf=tf),
        grid=(bsz, 2 * nf),
        in_specs=[pl.BlockSpec((seq, w), lambda b, s: (b, 0)),
                  pl.BlockSpec((seq, w), lambda b, s: (b, 0)),
                  pl.BlockSpec((tf, seq), fwd_tile),
                  pl.BlockSpec((tf, seq), lambda b, s: (nf + jnp.minimum(s, nf - 1), 0)),
                  pl.BlockSpec((tf, 2 * seq), lambda b, s: (jnp.maximum(s - nf, 0), 0)),
                  pl.BlockSpec((tf, w), fwd_tile),
                  pl.BlockSpec((tf, w), fwd_tile),
                  pl.BlockSpec((tf, w), fwd_tile),
                  pl.BlockSpec((1, w), lambda b, s: (0, 0))],
        out_specs=pl.BlockSpec((seq, w), lambda b, s: (b, 0)),
        out_shape=jax.ShapeDtypeStruct((bsz * seq, w), BF16),
        scratch_shapes=[pltpu.VMEM((2 * seq, w), BF16)],
        compiler_params=_cparams("parallel", "arbitrary"),
        name="hyena_conv",
    )(vx, x0, fwd, fwd, inv, kp, kq, kp2, d.reshape(1, w))


def _rel_bucket(rel):
    nb = REL_BUCKETS // 2
    max_exact = nb // 2
    ret = (rel > 0).astype(jnp.int32) * nb
    n = jnp.abs(rel)
    nf = jnp.maximum(n, 1).astype(F32)
    large = max_exact + (jnp.log(nf / max_exact) / math.log(REL_MAX_DIST / max_exact)
                         * (nb - max_exact)).astype(jnp.int32)
    large = jnp.minimum(large, nb - 1)
    return ret + jnp.where(n < max_exact, n, large)


def window_bias_mask(rel_bias):
    j = jnp.arange(WA_BLOCK, dtype=jnp.int32)[:, None]
    s = jnp.arange(3 * WA_BLOCK, dtype=jnp.int32)[None, :]
    rel = (s - WA_BLOCK) - j
    bias = jnp.transpose(rel_bias.astype(F32)[_rel_bucket(rel)], (2, 0, 1))
    band = jnp.abs(rel) <= WA_WINDOW
    return jnp.where(band[None], bias, NEG)


def _window_attn_kernel(sink_ref, q_ref, k_ref, v_ref, bias_ref, o_ref):
    nb = q_ref.shape[0] // WA_BLOCK
    scale = HEAD_DIM ** -0.5 * LOG2E
    rq = WA_REP * WA_BLOCK
    col = lax.broadcasted_iota(jnp.int32, (rq, 3 * WA_BLOCK), 1)
    head_of_row = lax.broadcasted_iota(jnp.int32, (rq, 1), 0) // WA_BLOCK

    def block(i, carry):
        ip = jnp.maximum(i - 1, 0)
        inx = jnp.minimum(i + 1, nb - 1)
        rows = lambda j: pl.ds(pl.multiple_of(j * WA_BLOCK, WA_BLOCK), WA_BLOCK)
        qb = q_ref[rows(i), :]
        kslab = jnp.concatenate([k_ref[rows(ip), :], k_ref[rows(i), :], k_ref[rows(inx), :]], axis=0)
        vslab = jnp.concatenate([v_ref[rows(ip), :], v_ref[rows(i), :], v_ref[rows(inx), :]], axis=0)
        lo = jnp.where(i > 0, 0, WA_BLOCK)
        hi = jnp.where(i < nb - 1, 3 * WA_BLOCK, 2 * WA_BLOCK)
        valid = jnp.logical_and(col >= lo, col < hi)
        outs = []
        for g in range(WA_KV_HEADS):
            heads = range(g * WA_REP, (g + 1) * WA_REP)
            q4 = jnp.concatenate([qb[:, h * HEAD_DIM:(h + 1) * HEAD_DIM] for h in heads], axis=0)
            kg = kslab[:, g * HEAD_DIM:(g + 1) * HEAD_DIM]
            vg = vslab[:, g * HEAD_DIM:(g + 1) * HEAD_DIM]
            s = _dot_nt(q4, kg) * scale + bias_ref[g]
            s = jnp.where(valid, s, NEG)
            sk = jnp.zeros((rq, 1), F32)
            for r, h in enumerate(heads):
                sk = jnp.where(head_of_row == r, sink_ref[h], sk)
            m = jnp.maximum(jnp.max(s, axis=-1, keepdims=True), sk)
            p = jnp.exp2(s - m)
            den = jnp.sum(p, axis=-1, keepdims=True) + jnp.exp2(sk - m)
            o4 = _dot(p.astype(BF16), vg) / den
            outs += [o4[r * WA_BLOCK:(r + 1) * WA_BLOCK] for r in range(WA_REP)]
        o_ref[rows(i), :] = jnp.concatenate(outs, axis=-1).astype(o_ref.dtype)
        return carry

    lax.fori_loop(0, nb, block, 0, unroll=2)


def window_attn(proj, sink, bias_mask, bsz, seq):
    hq = WA_HEADS * HEAD_DIM
    hkv = WA_KV_HEADS * HEAD_DIM
    q_blk = (3 * HY_WIDTH) // hq
    k_blk = (3 * HY_WIDTH + hq) // hkv
    bias2 = (bias_mask * LOG2E).reshape(WA_KV_HEADS, WA_REP * WA_BLOCK, 3 * WA_BLOCK)
    return pl.pallas_call(
        _window_attn_kernel,
        grid=(bsz,),
        in_specs=[pl.BlockSpec(memory_space=pltpu.SMEM),
                  pl.BlockSpec((seq, hq), lambda b: (b, q_blk)),
                  pl.BlockSpec((seq, hkv), lambda b: (b, k_blk)),
                  pl.BlockSpec((seq, hkv), lambda b: (b, k_blk + 1)),
                  pl.BlockSpec((WA_KV_HEADS, WA_REP * WA_BLOCK, 3 * WA_BLOCK), lambda b: (0, 0, 0))],
        out_specs=pl.BlockSpec((seq, hq), lambda b: (b, 0)),
        out_shape=jax.ShapeDtypeStruct((bsz * seq, hq), BF16),
        compiler_params=_cparams("parallel"),
        name="window_attn",
    )(sink.astype(F32) * LOG2E, proj, proj, proj, bias2)


def _out_proj_kernel(x_ref, a_ref, b_ref, w_ref, o_ref):
    acc = _dot(a_ref[...], w_ref[0:HALF, :]) + _dot(b_ref[...], w_ref[HALF:, :])
    o_ref[...] = x_ref[...] + acc


def out_proj_ab(x2d, y_a, y_b, w, tm=512):
    t = x2d.shape[0]
    tm = min(tm, t)
    return pl.pallas_call(
        _out_proj_kernel,
        grid=(t // tm,),
        in_specs=[pl.BlockSpec((tm, D_MODEL), lambda i: (i, 0)),
                  pl.BlockSpec((tm, HALF), lambda i: (i, 0)),
                  pl.BlockSpec((tm, HALF), lambda i: (i, 0)),
                  pl.BlockSpec((D_MODEL, D_MODEL), lambda i: (0, 0))],
        out_specs=pl.BlockSpec((tm, D_MODEL), lambda i: (i, 0)),
        out_shape=jax.ShapeDtypeStruct(x2d.shape, F32),
        compiler_params=_cparams("parallel"),
        name="out_proj_ab",
    )(x2d, y_a, y_b, w)


def s5_discretise(a_re, a_im, log_dt, b_re, b_im, c_re, c_im):
    lam = lax.complex(a_re.astype(F32), a_im.astype(F32))
    dt = jnp.exp(log_dt.astype(F32))[..., None]
    abar = jnp.exp(lam * dt)
    bmat = lax.complex(b_re.astype(F32), b_im.astype(F32))
    bbar = ((abar - 1.0) / lam)[..., None] * bmat
    cmat = lax.complex(c_re.astype(F32), c_im.astype(F32))
    nj, gl = S5_GROUPS // SUBLANES, SUBLANES
    eye = jnp.eye(gl, dtype=F32)
    a5 = jnp.stack([abar.real, abar.imag], axis=1).reshape(2, 2, nj, 1, gl * S5_STATE)
    a5 = jnp.transpose(a5, (0, 2, 1, 3, 4))
    a5 = jnp.broadcast_to(a5, (2, nj, 2, SUBLANES, gl * S5_STATE))

    def pack_b(x):
        x = x.reshape(2, nj, gl, S5_STATE, S5_GROUP)
        y = jnp.einsum('hg,djgpc->djhcgp', eye, x)
        return y.reshape(2, nj, gl * S5_GROUP, gl * S5_STATE)

    def pack_c(x):
        x = x.reshape(2, nj, gl, S5_GROUP, S5_STATE)
        y = jnp.einsum('hg,djgcp->djgphc', eye, x)
        return y.reshape(2, nj, gl * S5_STATE, gl * S5_GROUP)

    bm = jnp.concatenate([pack_b(bbar.real), pack_b(bbar.imag)], axis=-1).astype(BF16)
    cm = jnp.concatenate([pack_c(cmat.real), -pack_c(cmat.imag)], axis=-2).astype(BF16)
    return a5, bm, cm


def _s5_kernel(u_ref, a_ref, b_ref, c_ref, y_ref, buf_ref, h_ref, *, chunk):
    d = pl.program_id(1)
    nj = S5_GROUPS // SUBLANES
    sw = SUBLANES * S5_STATE
    rows = chunk * SUBLANES

    @pl.when(pl.program_id(2) == 0)
    def _():
        h_ref[...] = jnp.zeros_like(h_ref)

    u = u_ref[...].reshape(rows, HALF).astype(BF16)
    for j in range(nj):
        buf_ref[j] = _dot(u[:, j * LANES:(j + 1) * LANES], b_ref[0, j])

    for j0 in range(0, nj, 2):
        js = (j0, j0 + 1)

        def step(s, carry, js=js):
            t = jnp.where(d == 0, s, chunk - 1 - s)
            r0 = pl.multiple_of(t * SUBLANES, SUBLANES)
            new = []
            for n, j in enumerate(js):
                hr, hi = carry[2 * n], carry[2 * n + 1]
                ar = a_ref[0, j, 0]
                ai = a_ref[0, j, 1]
                br = buf_ref[j, pl.ds(r0, SUBLANES), 0:sw]
                bi = buf_ref[j, pl.ds(r0, SUBLANES), sw:2 * sw]
                nr = ar * hr - ai * hi + br
                ni = ar * hi + ai * hr + bi
                buf_ref[j, pl.ds(r0, SUBLANES), 0:sw] = nr
                buf_ref[j, pl.ds(r0, SUBLANES), sw:2 * sw] = ni
                new += [nr, ni]
            return tuple(new)

        init = tuple(h_ref[2 * j + k] for j in js for k in range(2))
        fin = lax.fori_loop(0, chunk, step, init, unroll=2)
        for n, j in enumerate(js):
            h_ref[2 * j] = fin[2 * n]
            h_ref[2 * j + 1] = fin[2 * n + 1]

    for j in range(nj):
        yj = _dot(buf_ref[j].astype(BF16), c_ref[0, j])
        y_ref[0, :, :, j * LANES:(j + 1) * LANES] = yj.reshape(chunk, SUBLANES, LANES).astype(y_ref.dtype)


def s5_scan(u_tm, a5, bm, cm, bsz, seq, chunk=64):
    chunk = min(chunk, seq)
    nc = seq // chunk
    nj = S5_GROUPS // SUBLANES
    sw = SUBLANES * S5_STATE

    def tchunk(d, i):
        return i + d * (nc - 1 - 2 * i)

    return pl.pallas_call(
        functools.partial(_s5_kernel, chunk=chunk),
        grid=(bsz // SUBLANES, 2, nc),
        in_specs=[pl.BlockSpec((chunk, SUBLANES, HALF), lambda b, d, i: (tchunk(d, i), b, 0)),
                  pl.BlockSpec((1, nj, 2, SUBLANES, sw), lambda b, d, i: (d, 0, 0, 0, 0)),
                  pl.BlockSpec((1, nj, LANES, 2 * sw), lambda b, d, i: (d, 0, 0, 0)),
                  pl.BlockSpec((1, nj, 2 * sw, LANES), lambda b, d, i: (d, 0, 0, 0))],
        out_specs=pl.BlockSpec((1, chunk, SUBLANES, HALF), lambda b, d, i: (d, tchunk(d, i), b, 0)),
        out_shape=jax.ShapeDtypeStruct((2, seq, bsz, HALF), F32),
        scratch_shapes=[pltpu.VMEM((nj, chunk * SUBLANES, 2 * sw), F32),
                        pltpu.VMEM((2 * nj, SUBLANES, sw), F32)],
        compiler_params=_cparams("parallel", "arbitrary", "arbitrary"),
        name="s5_scan",
    )(u_tm, a5, bm, cm)


def _gelu_tanh(x):
    return 0.5 * x * (1.0 + jnp.tanh(math.sqrt(2.0 / math.pi) * (x + 0.044715 * (x * x * x))))


def _out_proj_cd_kernel(x_ref, u_ref, yf_ref, yb_ref, mla_ref, d_ref, gw_ref, gb_ref, w_ref, o_ref):
    y = u_ref[...] * d_ref[...] + yf_ref[0] + yb_ref[0]
    g = _gelu_tanh(y)
    z = _dot(g.astype(BF16), gw_ref[...]) + gb_ref[...]
    y_s5 = g * jax.nn.sigmoid(z)
    acc = _dot(y_s5.astype(BF16), w_ref[0:HALF, :]) + _dot(mla_ref[...], w_ref[HALF:, :])
    o_ref[...] = x_ref[...] + acc


def out_proj_cd(x2d, u_tm2, y_tm2, y_mla, d, glu_w, glu_b, w, bsz, seq, tm=512):
    tm = min(tm, seq)
    nl = seq // tm
    row = lambda b, i: (b * nl + i, 0)
    const = lambda b, i: (0, 0)
    return pl.pallas_call(
        _out_proj_cd_kernel,
        grid=(bsz, nl),
        in_specs=[pl.BlockSpec((tm, D_MODEL), row),
                  pl.BlockSpec((tm, HALF), lambda b, i: (i, b)),
                  pl.BlockSpec((1, tm, HALF), lambda b, i: (0, i, b)),
                  pl.BlockSpec((1, tm, HALF), lambda b, i: (1, i, b)),
                  pl.BlockSpec((tm, HALF), row),
                  pl.BlockSpec((1, HALF), const),
                  pl.BlockSpec((HALF, HALF), const),
                  pl.BlockSpec((1, HALF), const),
                  pl.BlockSpec((D_MODEL, D_MODEL), const)],
        out_specs=pl.BlockSpec((tm, D_MODEL), row),
        out_shape=jax.ShapeDtypeStruct(x2d.shape, F32),
        compiler_params=_cparams("parallel", "parallel"),
        name="out_proj_cd",
    )(x2d, u_tm2, y_tm2, y_tm2, y_mla, d.reshape(1, HALF), glu_w, glu_b.reshape(1, HALF), w)


MLA_HP = 128


def mla_weights(w_uq, w_ukv):
    rq = w_uq.shape[0]
    wq = w_uq.astype(F32).reshape(rq, MLA_HEADS, MLA_NOPE + MLA_ROPE)
    half = MLA_ROPE // 2
    x1, x2 = wq[..., MLA_NOPE:MLA_NOPE + half], wq[..., MLA_NOPE + half:]
    zpad = jnp.zeros((rq, MLA_HEADS, MLA_HP - MLA_NOPE - MLA_ROPE), F32)
    wq1 = jnp.concatenate([wq, zpad], axis=-1).reshape(rq, MLA_HEADS * MLA_HP)
    wq2 = jnp.concatenate([jnp.zeros((rq, MLA_HEADS, MLA_NOPE), F32), -x2, x1, zpad], axis=-1)
    wq2 = wq2.reshape(rq, MLA_HEADS * MLA_HP)
    rk = w_ukv.shape[0]
    wkv = w_ukv.astype(F32).reshape(rk, MLA_HEADS, MLA_NOPE + MLA_V)
    wk = jnp.concatenate([wkv[..., :MLA_NOPE], jnp.zeros((rk, MLA_HEADS, MLA_HP - MLA_NOPE), F32)], axis=-1)
    wk = wk.reshape(rk, MLA_HEADS * MLA_HP)
    wv = wkv[..., MLA_NOPE:].reshape(rk, MLA_HEADS * MLA_V)
    return wq1.astype(BF16), wq2.astype(BF16), wk.astype(BF16), wv.astype(BF16)


def rope_tables(seq):
    inv = 1.0 / (ROPE_THETA ** (jnp.arange(0, MLA_ROPE, 2, dtype=F32) / MLA_ROPE))
    ang = jnp.arange(seq, dtype=F32)[:, None] * inv[None, :]
    c, s = jnp.cos(ang), jnp.sin(ang)
    ones = jnp.ones((seq, MLA_NOPE), F32)
    zpad = jnp.zeros((seq, MLA_HP - MLA_NOPE - MLA_ROPE), F32)
    cos_t = jnp.concatenate([ones, c, c, zpad], axis=-1)
    sin_t = jnp.concatenate([0.0 * ones, s, s, zpad], axis=-1)
    return cos_t, sin_t


def _mla_prep_kernel(r_ref, qg_ref, kg_ref, wq1_ref, wq2_ref, wk_ref, wv_ref, cos_ref, sin_ref,
                     q_ref, k_ref, v_ref):
    scale = (MLA_NOPE + MLA_ROPE) ** -0.5 * LOG2E
    cq = _rms(r_ref[:, 0:MLA_Q_RANK].astype(F32), qg_ref[...]).astype(BF16)
    o1 = MLA_Q_RANK + MLA_KV_RANK
    ckv = _rms(r_ref[:, MLA_Q_RANK:o1].astype(F32), kg_ref[...]).astype(BF16)
    cos_t, sin_t = cos_ref[...], sin_ref[...]
    kr = r_ref[:, o1:o1 + LANES].astype(F32) * cos_t + r_ref[:, o1 + LANES:o1 + 2 * LANES].astype(F32) * sin_t
    v_ref[...] = _dot(ckv, wv_ref[...]).astype(v_ref.dtype)
    for h in range(MLA_HEADS):
        sl = slice(h * MLA_HP, (h + 1) * MLA_HP)
        qh = _dot(cq, wq1_ref[:, sl]) * cos_t + _dot(cq, wq2_ref[:, sl]) * sin_t
        q_ref[:, sl] = (qh * scale).astype(q_ref.dtype)
        k_ref[:, sl] = (_dot(ckv, wk_ref[:, sl]) + kr).astype(k_ref.dtype)


def mla_prep(rest, q_norm, kv_norm, wq1, wq2, wk, wv, cos_t, sin_t, bsz, seq, tm=512):
    tm = min(tm, seq)
    nl = seq // tm
    t = bsz * seq
    wr = rest.shape[1]
    row = lambda b, i: (b * nl + i, 0)
    const = lambda b, i: (0, 0)
    qk = MLA_HEADS * MLA_HP
    return pl.pallas_call(
        _mla_prep_kernel,
        grid=(bsz, nl),
        in_specs=[pl.BlockSpec((tm, wr), row),
                  pl.BlockSpec((1, MLA_Q_RANK), const),
                  pl.BlockSpec((1, MLA_KV_RANK), const),
                  pl.BlockSpec((MLA_Q_RANK, qk), const),
                  pl.BlockSpec((MLA_Q_RANK, qk), const),
                  pl.BlockSpec((MLA_KV_RANK, qk), const),
                  pl.BlockSpec((MLA_KV_RANK, MLA_HEADS * MLA_V), const),
                  pl.BlockSpec((tm, MLA_HP), lambda b, i: (i, 0)),
                  pl.BlockSpec((tm, MLA_HP), lambda b, i: (i, 0))],
        out_specs=[pl.BlockSpec((tm, qk), row), pl.BlockSpec((tm, qk), row),
                   pl.BlockSpec((tm, MLA_HEADS * MLA_V), row)],
        out_shape=[jax.ShapeDtypeStruct((t, qk), BF16), jax.ShapeDtypeStruct((t, qk), BF16),
                   jax.ShapeDtypeStruct((t, MLA_HEADS * MLA_V), BF16)],
        compiler_params=_cparams("parallel", "parallel"),
        name="mla_prep",
    )(rest, q_norm.reshape(1, -1), kv_norm.reshape(1, -1), wq1, wq2, wk, wv, cos_t, sin_t)


def _mla_attn_kernel(q_ref, k_ref, v_ref, o_ref):
    v = v_ref[...]
    outs = []
    for h in range(2):
        sl = slice(h * MLA_HP, (h + 1) * MLA_HP)
        s = _dot_nt(q_ref[:, sl], k_ref[:, sl])
        m = jnp.max(s, axis=-1, keepdims=True)
        p = jnp.exp2(s - m)
        den = jnp.sum(p, axis=-1, keepdims=True)
        outs.append(_dot(p.astype(BF16), v) / den)
    lane = lax.broadcasted_iota(jnp.int32, outs[0].shape, 1)
    o_ref[...] = jnp.where(lane < MLA_V, outs[0], outs[1]).astype(o_ref.dtype)


def mla_attn(q, k, v, bsz, seq, tq=512):
    tq = min(tq, seq)
    nq = seq // tq
    return pl.pallas_call(
        _mla_attn_kernel,
        grid=(bsz, MLA_HEADS // 2, nq),
        in_specs=[pl.BlockSpec((tq, 2 * MLA_HP), lambda b, p, i: (b * nq + i, p)),
                  pl.BlockSpec((seq, 2 * MLA_HP), lambda b, p, i: (b, p)),
                  pl.BlockSpec((seq, 2 * MLA_V), lambda b, p, i: (b, p))],
        out_specs=pl.BlockSpec((tq, 2 * MLA_V), lambda b, p, i: (b * nq + i, p)),
        out_shape=jax.ShapeDtypeStruct((bsz * seq, MLA_HEADS * MLA_V), BF16),
        compiler_params=_cparams("parallel", "parallel", "parallel"),
        name="mla_attn",
    )(q, k, v)


def _cross_router_kernel(x_ref, kv_ref, gc_ref, wq_ref, wo_ref, gf_ref, wr_ref, xo_ref, hn_ref, aff_ref):
    x = x_ref[...]
    h = _rms(x, gc_ref[...]).astype(BF16)
    q = (_dot(h, wq_ref[...]) * (CA_HEAD_DIM ** -0.5 * LOG2E)).astype(BF16)
    hd = CA_HEADS * CA_HEAD_DIM
    outs = []
    for a in range(CA_HEADS):
        sl = slice(a * CA_HEAD_DIM, (a + 1) * CA_HEAD_DIM)
        s = _dot_nt(q[:, sl], kv_ref[:, sl])
        m = jnp.max(s, axis=-1, keepdims=True)
        p = jnp.exp2(s - m)
        den = jnp.sum(p, axis=-1, keepdims=True)
        outs.append((_dot(p.astype(BF16), kv_ref[:, hd + a * CA_HEAD_DIM:hd + (a + 1) * CA_HEAD_DIM]) / den))
    o = jnp.concatenate(outs, axis=-1).astype(BF16)
    xn = x + _dot(o, wo_ref[...])
    xo_ref[...] = xn
    hf = _rms(xn, gf_ref[...])
    hb = hf.astype(BF16)
    hn_ref[...] = hb
    lo = (hf - hb.astype(F32)).astype(BF16)
    logits = _dot(hb, wr_ref[0]) + (_dot(lo, wr_ref[0]) + _dot(hb, wr_ref[1]))
    lane = lax.broadcasted_iota(jnp.int32, logits.shape, 1)
    logits = jnp.where(lane < N_EXPERTS, logits, NEG)
    m = jnp.max(logits, axis=-1, keepdims=True)
    e = jnp.exp(logits - m)
    aff = e / jnp.sum(e, axis=-1, keepdims=True)
    aff_ref[...] = aff[:, 0:N_EXPERTS]


def cross_router(x2d, kv, ln_cross, w_q, w_o, ln_ffn, w_router, bsz, seq, tq=256):
    tq = min(tq, seq)
    nq = seq // tq
    t = bsz * seq
    mem = kv.shape[0] // bsz
    hd = CA_HEADS * CA_HEAD_DIM
    wr = jnp.pad(w_router.astype(F32), ((0, 0), (0, LANES - N_EXPERTS)))
    wr_hi = wr.astype(BF16)
    wr2 = jnp.stack([wr_hi, (wr - wr_hi.astype(F32)).astype(BF16)])
    row = lambda b, i: (b * nq + i, 0)
    const = lambda b, i: (0, 0)
    return pl.pallas_call(
        _cross_router_kernel,
        grid=(bsz, nq),
        in_specs=[pl.BlockSpec((tq, D_MODEL), row),
                  pl.BlockSpec((mem, 2 * hd), lambda b, i: (b, 0)),
                  pl.BlockSpec((1, D_MODEL), const),
                  pl.BlockSpec((D_MODEL, hd), const),
                  pl.BlockSpec((hd, D_MODEL), const),
                  pl.BlockSpec((1, D_MODEL), const),
                  pl.BlockSpec((2, D_MODEL, LANES), lambda b, i: (0, 0, 0))],
        out_specs=[pl.BlockSpec((tq, D_MODEL), row), pl.BlockSpec((tq, D_MODEL), row),
                   pl.BlockSpec((tq, N_EXPERTS), row)],
        out_shape=[jax.ShapeDtypeStruct((t, D_MODEL), F32), jax.ShapeDtypeStruct((t, D_MODEL), BF16),
                   jax.ShapeDtypeStruct((t, N_EXPERTS), F32)],
        compiler_params=_cparams("parallel", "parallel"),
        name="cross_router",
    )(x2d, kv, ln_cross.reshape(1, -1), w_q, w_o, ln_ffn.reshape(1, -1), wr2)


def _expert_ffn_kernel(x_ref, g_ref, wg_ref, wu_ref, wd_ref, o_ref, hid_ref, *, tf):
    x = x_ref[0]
    for c0 in range(0, D_EXPERT, tf):
        a = _dot(x, wg_ref[0, :, c0:c0 + tf])
        u = _dot(x, wu_ref[0, :, c0:c0 + tf])
        hid_ref[:, c0:c0 + tf] = (a * jax.nn.sigmoid(a) * u).astype(BF16)
    o_ref[0] = (_dot(hid_ref[...], wd_ref[0]) * g_ref[0]).astype(o_ref.dtype)


def expert_ffn(xe, gate, w_gate, w_up, w_down, tm=1024, tf=512):
    e, cap, _ = xe.shape
    tm = min(tm, cap)
    return pl.pallas_call(
        functools.partial(_expert_ffn_kernel, tf=tf),
        grid=(e, cap // tm),
        in_specs=[pl.BlockSpec((1, tm, D_MODEL), lambda e, m: (e, m, 0)),
                  pl.BlockSpec((1, tm, 1), lambda e, m: (e, m, 0)),
                  pl.BlockSpec((1, D_MODEL, D_EXPERT), lambda e, m: (e, 0, 0)),
                  pl.BlockSpec((1, D_MODEL, D_EXPERT), lambda e, m: (e, 0, 0)),
                  pl.BlockSpec((1, D_EXPERT, D_MODEL), lambda e, m: (e, 0, 0))],
        out_specs=pl.BlockSpec((1, tm, D_MODEL), lambda e, m: (e, m, 0)),
        out_shape=jax.ShapeDtypeStruct((e, cap, D_MODEL), BF16),
        scratch_shapes=[pltpu.VMEM((tm, D_EXPERT), BF16)],
        compiler_params=_cparams("parallel", "arbitrary"),
        name="expert_ffn",
    )(xe, gate, w_gate, w_up, w_down)


ROUTE_GROUP = SUBLANES


def _route_thr_kernel(a_ref, thr_ref, *, cap):
    bits = pltpu.bitcast(a_ref[...], jnp.int32)

    def body(i, lo):
        cand = lo | jnp.left_shift(jnp.int32(1), 30 - i)
        cnt = jnp.sum(jnp.where(bits >= cand, 1.0, 0.0), axis=1, keepdims=True)
        return jnp.where(cnt >= cap, cand, lo)

    thr_ref[...] = lax.fori_loop(0, 31, body, jnp.zeros(thr_ref.shape, jnp.int32))


def _prefix_rows(m, upper, lower):
    mb = m.astype(BF16)
    incl = _dot(mb, upper)
    tot = jnp.broadcast_to(incl[:, LANES - 1:LANES], incl.shape).astype(BF16)
    return incl - m + _dot(lower, tot)


def _route_mask_kernel(thr_ref, a_ref, upper_ref, lower_ref, sel_ref, pos_ref, *, cap):
    e = pl.program_id(0)
    thr = thr_ref[e]
    bits = pltpu.bitcast(a_ref[0], jnp.int32)
    gt = jnp.where(bits > thr, 1.0, 0.0)
    eq = jnp.where(bits == thr, 1.0, 0.0)
    need = cap - jnp.sum(jnp.sum(gt, axis=1, keepdims=True), axis=0, keepdims=True)
    eq_rank = _prefix_rows(eq, upper_ref[...], lower_ref[...])
    sel = gt + jnp.where(eq_rank < need, eq, 0.0)
    sel_ref[0] = sel
    pos_ref[0] = _prefix_rows(sel, upper_ref[...], lower_ref[...]).astype(jnp.int32)


def _route_compact_kernel(glo_ref, ghi_ref, a_ref, pos_ref, idx_ref, gate_ref):
    e = pl.program_id(0)
    nc = idx_ref.shape[1]
    gw = ROUTE_GROUP * LANES
    slot0 = lax.broadcasted_iota(jnp.int32, (LANES, LANES), 0)
    lane = lax.broadcasted_iota(jnp.int32, (1, gw), 1)
    zeros = jnp.zeros((2 * SUBLANES - 5, gw), F32)

    ngroups = pos_ref.shape[1] // ROUTE_GROUP
    never = jnp.int32(1 << 30)

    def terms(g, slot):
        r0 = pl.multiple_of(g * ROUTE_GROUP, ROUTE_GROUP)
        pos = pos_ref[0, pl.ds(r0, ROUTE_GROUP), :]
        aff = a_ref[0, pl.ds(r0, ROUTE_GROUP), :]
        hit = jnp.concatenate(
            [jnp.where(pos[j:j + 1] == slot, 1.0, 0.0) for j in range(ROUTE_GROUP)],
            axis=1).astype(BF16)
        arow = jnp.concatenate([aff[j:j + 1] for j in range(ROUTE_GROUP)], axis=1)
        tok = lane + g * gw
        g0 = arow.astype(BF16).astype(F32)
        r1 = arow - g0
        g1 = r1.astype(BF16).astype(F32)
        g2 = r1 - g1
        lhs = jnp.concatenate([(tok >> 8).astype(F32), (tok & 255).astype(F32), g0, g1, g2, zeros], axis=0)
        return lhs.astype(BF16), hit

    def chunk(c, carry):
        slot = slot0 + c * LANES
        g_first = glo_ref[e * nc + c]
        g_last = ghi_ref[e * nc + c]
        g_second = jnp.minimum(g_first + 1, ngroups - 1)
        lhs_a, hit_a = terms(g_first, slot)
        lhs_b, hit_b = terms(g_second, slot + jnp.where(g_first + 1 <= g_last, 0, never))
        acc = _dot_nt(jnp.concatenate([lhs_a, lhs_b], axis=1), jnp.concatenate([hit_a, hit_b], axis=1))

        def group(g, acc):
            lhs, hit = terms(g, slot)
            return acc + _dot_nt(lhs, hit)

        acc = lax.fori_loop(g_first + 2, g_last + 1, group, acc)
        idx_ref[0, pl.ds(c, 1), :] = (acc[0:1] * 256.0 + acc[1:2]).astype(jnp.int32)
        gate_ref[0, pl.ds(c, 1), :] = (acc[2:3] + acc[3:4]) + acc[4:5]
        return carry

    lax.fori_loop(0, nc, chunk, 0)


def route_tokens(aff, cap):
    t, ne = aff.shape
    rows = t // LANES
    nc = cap // LANES
    aff_t = aff.T
    thr = pl.pallas_call(
        functools.partial(_route_thr_kernel, cap=cap),
        out_shape=jax.ShapeDtypeStruct((ne, 1), jnp.int32),
        compiler_params=pltpu.CompilerParams(vmem_limit_bytes=V7X_VMEM_LIMIT_BYTES),
        name="route_threshold",
    )(aff_t)
    aff3 = aff_t.reshape(ne, rows, LANES)
    ii = jnp.arange(LANES)
    upper = (ii[:, None] <= ii[None, :]).astype(BF16)
    rr = jnp.arange(rows)
    lower = (rr[None, :] < rr[:, None]).astype(BF16)
    blk = pl.BlockSpec((1, rows, LANES), lambda e: (e, 0, 0))
    sel, pos = pl.pallas_call(
        functools.partial(_route_mask_kernel, cap=cap),
        grid=(ne,),
        in_specs=[pl.BlockSpec(memory_space=pltpu.SMEM), blk,
                  pl.BlockSpec((LANES, LANES), lambda e: (0, 0)),
                  pl.BlockSpec((rows, rows), lambda e: (0, 0))],
        out_specs=[blk, blk],
        out_shape=[jax.ShapeDtypeStruct((ne, rows, LANES), F32), jax.ShapeDtypeStruct((ne, rows, LANES), jnp.int32)],
        compiler_params=_cparams("parallel"),
        name="route_mask",
    )(thr.reshape(ne), aff3, upper, lower)
    gsz = ROUTE_GROUP
    first = pos[:, ::gsz, 0]
    starts = jnp.arange(nc, dtype=jnp.int32) * LANES
    glo = jnp.sum(first[:, None, :] <= starts[None, :, None], axis=-1, dtype=jnp.int32) - 1
    ghi = jnp.sum(first[:, None, :] <= (starts + (LANES - 1))[None, :, None], axis=-1, dtype=jnp.int32) - 1
    posm = jnp.where(sel > 0, pos, -1)
    out_blk = pl.BlockSpec((1, nc, LANES), lambda e, *_: (e, 0, 0))
    grid_spec = pltpu.PrefetchScalarGridSpec(
        num_scalar_prefetch=2,
        grid=(ne,),
        in_specs=[pl.BlockSpec((1, rows, LANES), lambda e, *_: (e, 0, 0))] * 2,
        out_specs=[out_blk, out_blk],
    )
    idx, gate = pl.pallas_call(
        _route_compact_kernel,
        grid_spec=grid_spec,
        out_shape=[jax.ShapeDtypeStruct((ne, nc, LANES), jnp.int32), jax.ShapeDtypeStruct((ne, nc, LANES), F32)],
        compiler_params=_cparams("parallel"),
        name="route_compact",
    )(glo.reshape(-1), ghi.reshape(-1), aff3, posm)
    return idx.reshape(ne, cap), gate.reshape(ne, cap), pos.reshape(ne, t), posm.reshape(ne, t)


COMBINE_ROWS = 128
COMBINE_ALIGN = 16


def _combine_kernel(wst_ref, nex_ref, x_ref, pos_ref, ye_hbm, o_ref, ybuf, sem, xbuf, xsem, *, tb, cap):
    b = pl.program_id(0)
    nb = pl.num_programs(0)
    ne = N_EXPERTS
    w = COMBINE_ROWS
    slot = b % 2
    lane = lax.broadcasted_iota(jnp.int32, (1, w), 1)

    def window(bb, sl, e):
        st = pl.multiple_of(wst_ref[bb * ne + e], COMBINE_ALIGN)
        return pltpu.make_async_copy(ye_hbm.at[e, pl.ds(st, w), :], ybuf.at[sl, pl.ds(e * w, w), :], sem.at[sl, e])

    def start_all(bb, sl):
        for e in range(ne):
            window(bb, sl, e).start()

    @pl.when(b == 0)
    def _():
        start_all(b, slot)

    @pl.when(b + 1 < nb)
    def _():
        start_all(b + 1, 1 - slot)

    pos = pos_ref[...]
    cols = [jnp.broadcast_to(pos[:, e:e + 1], (tb, w)) for e in range(ne)]
    onehot = jnp.concatenate(
        [jnp.where(cols[e] - wst_ref[b * ne + e] == lane, 1.0, 0.0) for e in range(ne)], axis=1).astype(BF16)
    for e in range(ne):
        window(b, slot, e).wait()
    o_ref[...] = x_ref[...] + _dot(onehot, ybuf[slot])

    for e in range(ne):
        def extra(k, carry, e=e):
            first = wst_ref[b * ne + e] + k * w
            st = pl.multiple_of(jnp.minimum(first, cap - w), COMBINE_ALIGN)
            cp = pltpu.make_async_copy(ye_hbm.at[e, pl.ds(st, w), :], xbuf, xsem.at[0])
            cp.start()
            cp.wait()
            hit = jnp.logical_and(cols[e] - st == lane, cols[e] >= first)
            o_ref[...] += _dot(jnp.where(hit, 1.0, 0.0).astype(BF16), xbuf[...])
            return carry

        lax.fori_loop(1, nex_ref[b * ne + e] + 1, extra, 0)


def moe_combine(x2d, ye, pos, posm, tb=512):
    t = x2d.shape[0]
    ne, cap = ye.shape[0], ye.shape[1]
    tb = min(tb, t)
    nb = t // tb
    w = COMBINE_ROWS
    lo = pos[:, ::tb]
    hi = jnp.concatenate([lo[:, 1:], jnp.full((ne, 1), cap, jnp.int32)], axis=1)
    wst = jnp.minimum((lo // COMBINE_ALIGN) * COMBINE_ALIGN, cap - w)
    nex = jnp.maximum(hi - (wst + w) + (w - 1), 0) // w
    posm = posm.T
    grid_spec = pltpu.PrefetchScalarGridSpec(
        num_scalar_prefetch=2,
        grid=(nb,),
        in_specs=[pl.BlockSpec((tb, D_MODEL), lambda b, *_: (b, 0)),
                  pl.BlockSpec((tb, ne), lambda b, *_: (b, 0)),
                  pl.BlockSpec(memory_space=pl.ANY)],
        out_specs=pl.BlockSpec((tb, D_MODEL), lambda b, *_: (b, 0)),
        scratch_shapes=[pltpu.VMEM((2, ne * w, D_MODEL), BF16),
                        pltpu.SemaphoreType.DMA((2, ne)),
                        pltpu.VMEM((w, D_MODEL), BF16),
                        pltpu.SemaphoreType.DMA((1,))],
    )
    return pl.pallas_call(
        functools.partial(_combine_kernel, tb=tb, cap=cap),
        grid_spec=grid_spec,
        out_shape=jax.ShapeDtypeStruct(x2d.shape, F32),
        compiler_params=_cparams("arbitrary"),
        name="moe_combine",
    )(wst.T.reshape(-1), nex.T.reshape(-1), x2d, posm, ye)


def _final_norm_kernel(x_ref, g_ref, o_ref):
    o_ref[...] = _rms(x_ref[...], g_ref[...])


def final_norm(x2d, g, tm=1024):
    t = x2d.shape[0]
    tm = min(tm, t)
    return pl.pallas_call(
        _final_norm_kernel,
        grid=(t // tm,),
        in_specs=[pl.BlockSpec((tm, D_MODEL), lambda i: (i, 0)), pl.BlockSpec((1, D_MODEL), lambda i: (0, 0))],
        out_specs=pl.BlockSpec((tm, D_MODEL), lambda i: (i, 0)),
        out_shape=jax.ShapeDtypeStruct(x2d.shape, F32),
        compiler_params=_cparams("parallel"),
        name="final_norm",
    )(x2d, g.reshape(1, -1))


def mixer_ab(x2d, e, p, shared, bsz, seq):
    proj = norm_proj(x2d, p['ln_mix_l'], p['w_in_ab'][e], bsz, seq, [(0, AB_IN, BF16, False)])[0]
    vx, x0 = hyena_pre(proj, p['hy_conv_w'][e], p['hy_conv_b'][e], bsz, seq)
    kp, kq, kp2 = shared['hy_spec'][e]
    y_hy = hyena_conv(vx, x0, shared['dft_fwd'], shared['dft_inv'], kp, kq, kp2, p['hy_d'][e], bsz, seq)
    y_wa = window_attn(proj, p['attn_sink'][e], shared['wa_bias'], bsz, seq)
    return out_proj_ab(x2d, y_hy, y_wa, p['w_out_ab'][e])


def mixer_cd(x2d, o, p, shared, bsz, seq):
    u_tm2, rest = norm_proj(x2d, p['ln_mix_l'], shared['w_in_cd'][o], bsz, seq,
                            [(0, HALF, F32, True), (HALF, CD_PAD - HALF, BF16, False)])
    a5, bm, cm = shared['s5'][o]
    y_tm = s5_scan(u_tm2.reshape(seq, bsz, HALF), a5, bm, cm, bsz, seq)
    wq1, wq2, wk, wv = shared['mla_w'][o]
    q, k, v = mla_prep(rest, p['mla_q_norm'][o], p['mla_kv_norm'][o], wq1, wq2, wk, wv,
                       shared['rope_cos'], shared['rope_sin'], bsz, seq)
    y_mla = mla_attn(q, k, v, bsz, seq)
    return out_proj_cd(x2d, u_tm2, y_tm.reshape(2, seq, bsz * HALF), y_mla, p['s5_d'][o],
                       p['s5_glu_w'][o], p['s5_glu_b'][o], p['w_out_cd'][o], bsz, seq)


def ec_moe(x2d, hn, aff, w_gate, w_up, w_down):
    t = x2d.shape[0]
    cap = EC_CAPACITY_FACTOR * t // N_EXPERTS
    idx, gate, pos, posm = route_tokens(aff, cap)
    xe = hn[idx]
    ye = expert_ffn(xe, gate[..., None], w_gate, w_up, w_down)
    return moe_combine(x2d, ye, pos, posm)


def prepare_shared(p, seq):
    sh = {}
    sh['wa_bias'] = window_bias_mask(p['rel_bias'])
    sh['dft_fwd'], sh['dft_inv'] = dft_matrices(seq)
    sh['rope_cos'], sh['rope_sin'] = rope_tables(seq)
    specs = []
    for e in range(p['w_in_ab'].shape[0]):
        h = hyena_filters(seq, p['hy_filt_w1'][e], p['hy_filt_b1'][e], p['hy_filt_w2'][e], p['hy_filt_b2'][e],
                          p['hy_filt_w3'][e], p['hy_filt_freq'][e])
        h_fwd, h_bwd = h[:, :HY_WIDTH], h[:, HY_WIDTH:]
        k = jnp.concatenate([h_fwd, jnp.zeros_like(h_fwd[:1]), h_bwd[:0:-1]], axis=0)
        kf = kernel_spectrum(sh['dft_fwd'], k)
        k_r, k_s = kf[:seq], kf[seq:]
        specs.append((k_r, k_s.at[0].set(0.0), k_r.at[0].set(k_s[0])))
    sh['hy_spec'] = specs
    s5, mla_w, w_in_cd = [], [], []
    for o in range(p['w_in_cd'].shape[0]):
        s5.append(s5_discretise(p['s5_a_re'][o], p['s5_a_im'][o], p['s5_log_dt'][o], p['s5_b_re'][o],
                                p['s5_b_im'][o], p['s5_c_re'][o], p['s5_c_im'][o]))
        mla_w.append(mla_weights(p['mla_w_uq'][o], p['mla_w_ukv'][o]))
        w = p['w_in_cd'][o].astype(F32)
        o2 = HALF + MLA_Q_RANK + MLA_KV_RANK
        kr = w[:, o2:o2 + MLA_ROPE]
        half = MLA_ROPE // 2
        kr_rot = jnp.concatenate([-kr[:, half:], kr[:, :half]], axis=1)
        z64 = jnp.zeros((D_MODEL, MLA_NOPE), F32)
        z32 = jnp.zeros((D_MODEL, MLA_HP - MLA_NOPE - MLA_ROPE), F32)
        w_in_cd.append(jnp.concatenate([w[:, :o2], z64, kr, z32, z64, kr_rot, z32], axis=1).astype(BF16))
    sh['s5'], sh['mla_w'], sh['w_in_cd'] = s5, mla_w, w_in_cd
    return sh


def run_trunk(x, mem, p, shared):
    bsz, seq, _ = x.shape
    x2d = x.reshape(bsz * seq, D_MODEL)
    mem2d = mem.reshape(bsz * mem.shape[1], D_MODEL)
    for layer in range(DEPTH):
        pl_ = dict(p, ln_mix_l=p['ln_mix'][layer])
        if layer % 2 == 0:
            x2d = mixer_ab(x2d, layer // 2, pl_, shared, bsz, seq)
        else:
            x2d = mixer_cd(x2d, layer // 2, pl_, shared, bsz, seq)
        kv = norm_proj(mem2d, p['ln_mem'][layer], p['ca_w_kv'][layer], bsz, mem.shape[1],
                       [(0, 2 * CA_HEADS * CA_HEAD_DIM, BF16, False)])[0]
        x2d, hn, aff = cross_router(x2d, kv, p['ln_cross'][layer], p['ca_w_q'][layer], p['ca_w_o'][layer],
                                    p['ln_ffn'][layer], p['moe_w_router'][layer], bsz, seq)
        x2d = ec_moe(x2d, hn, aff, p['moe_w_gate'][layer], p['moe_w_up'][layer], p['moe_w_down'][layer])
    return final_norm(x2d, p['ln_final']).reshape(bsz, seq, D_MODEL)


_BF16_WEIGHTS = ('w_in_ab', 'w_out_ab', 'w_out_cd', 's5_glu_w', 'ca_w_q', 'ca_w_kv', 'ca_w_o',
                 'moe_w_gate', 'moe_w_up', 'moe_w_down')


def kernel(x_prompt, x_sample, mem_prompt, mem_sample, ln_mix, ln_cross, ln_mem, ln_ffn, ln_final, rel_bias, w_in_ab, w_out_ab, hy_conv_w, hy_conv_b, hy_filt_w1, hy_filt_b1, hy_filt_w2, hy_filt_b2, hy_filt_w3, hy_filt_freq, hy_d, attn_sink, w_in_cd, w_out_cd, s5_a_re, s5_a_im, s5_log_dt, s5_b_re, s5_b_im, s5_c_re, s5_c_im, s5_d, s5_glu_w, s5_glu_b, mla_q_norm, mla_w_uq, mla_kv_norm, mla_w_ukv, ca_w_q, ca_w_kv, ca_w_o, moe_w_router, moe_w_gate, moe_w_up, moe_w_down):
    p = dict(ln_mix=ln_mix, ln_cross=ln_cross, ln_mem=ln_mem, ln_ffn=ln_ffn, ln_final=ln_final,
             rel_bias=rel_bias, w_in_ab=w_in_ab, w_out_ab=w_out_ab, hy_conv_w=hy_conv_w,
             hy_conv_b=hy_conv_b, hy_filt_w1=hy_filt_w1, hy_filt_b1=hy_filt_b1,
             hy_filt_w2=hy_filt_w2, hy_filt_b2=hy_filt_b2, hy_filt_w3=hy_filt_w3,
             hy_filt_freq=hy_filt_freq, hy_d=hy_d, attn_sink=attn_sink, w_in_cd=w_in_cd,
             w_out_cd=w_out_cd, s5_a_re=s5_a_re, s5_a_im=s5_a_im, s5_log_dt=s5_log_dt,
             s5_b_re=s5_b_re, s5_b_im=s5_b_im, s5_c_re=s5_c_re, s5_c_im=s5_c_im, s5_d=s5_d,
             s5_glu_w=s5_glu_w, s5_glu_b=s5_glu_b, mla_q_norm=mla_q_norm, mla_w_uq=mla_w_uq,
             mla_kv_norm=mla_kv_norm, mla_w_ukv=mla_w_ukv, ca_w_q=ca_w_q, ca_w_kv=ca_w_kv,
             ca_w_o=ca_w_o, moe_w_router=moe_w_router, moe_w_gate=moe_w_gate,
             moe_w_up=moe_w_up, moe_w_down=moe_w_down)
    assert x_prompt.shape[1] == x_sample.shape[1]
    shared = prepare_shared(p, x_prompt.shape[1])
    for name in _BF16_WEIGHTS:
        p[name] = p[name].astype(BF16)
    y_prompt = run_trunk(x_prompt, mem_prompt, p, shared)
    y_sample = run_trunk(x_sample, mem_sample, p, shared)
    return (y_prompt, y_sample)
```

```python
import functools
import math

import jax
import jax.numpy as jnp
import numpy as np
from jax import lax
from jax.experimental import pallas as pl
from jax.experimental.pallas import tpu as pltpu

D_MODEL = 1024
DEPTH = 4
HALF = 512
HEAD_DIM = 64
EPS = 1e-6
NEG = -1e30

HY_WIDTH = HALF
HY_EMB = 33
HY_BANDS = (HY_EMB - 1) // 2
HY_FILT_HIDDEN = 64
HY_DECAY_TARGET = 1e-2
HY_FAST = 0.3
HY_SLOW = 1.5
HY_MIN_DECAY = math.log(HY_DECAY_TARGET) / HY_SLOW
HY_MAX_DECAY = math.log(HY_DECAY_TARGET) / HY_FAST
HY_SHIFT = 0.05

WA_HEADS = 8
WA_KV_HEADS = 2
WA_REP = 4
WA_WINDOW = 128
WA_BLOCK = 128
REL_BUCKETS = 32
REL_MAX_DIST = 128

S5_GROUP = 16
S5_GROUPS = 32
S5_STATE = 64

MLA_HEADS = 8
MLA_NOPE = 64
MLA_ROPE = 32
MLA_V = 64
MLA_Q_RANK = 256
MLA_KV_RANK = 128
ROPE_THETA = 10000.0

CA_HEADS = 4
CA_HEAD_DIM = 128

N_EXPERTS = 16
EC_CAPACITY_FACTOR = 2
D_EXPERT = 2048

AB_IN = 3 * HY_WIDTH + (WA_HEADS + 2 * WA_KV_HEADS) * HEAD_DIM
CD_PAD = 1152

V7X_VMEM_LIMIT_BYTES = 56 * 1024 * 1024
LANES = 128
SUBLANES = 8
LOG2E = math.log2(math.e)

BF16 = jnp.bfloat16
F32 = jnp.float32


def _cparams(*sem):
    return pltpu.CompilerParams(dimension_semantics=sem, vmem_limit_bytes=V7X_VMEM_LIMIT_BYTES)


def _dot(a, b):
    return jnp.dot(a, b, preferred_element_type=F32)


def _dot_nt(a, b):
    return lax.dot_general(a, b, (((1,), (1,)), ((), ())), preferred_element_type=F32)


def _rms(xf, g):
    return xf * lax.rsqrt(jnp.mean(xf * xf, axis=-1, keepdims=True) + EPS) * g


def _norm_proj_kernel(x_ref, g_ref, w_ref, *out_refs, splits):
    hn = _rms(x_ref[...].astype(F32), g_ref[...]).astype(BF16)
    for o_ref, (start, width) in zip(out_refs, splits):
        for c0 in range(0, width, 512):
            cw = min(512, width - c0)
            o_ref[:, c0:c0 + cw] = _dot(hn, w_ref[:, start + c0:start + c0 + cw]).astype(o_ref.dtype)


def norm_proj(x2d, gain, w, bsz, seq, outs, tm=512):
    tm = min(tm, seq)
    nl = seq // tm
    n = w.shape[1]
    out_shapes, out_specs, splits = [], [], []
    for start, width, dtype, time_major in outs:
        splits.append((start, width))
        if time_major:
            out_shapes.append(jax.ShapeDtypeStruct((seq, bsz * width), dtype))
            out_specs.append(pl.BlockSpec((tm, width), lambda b, i: (i, b)))
        else:
            out_shapes.append(jax.ShapeDtypeStruct((bsz * seq, width), dtype))
            out_specs.append(pl.BlockSpec((tm, width), lambda b, i, nl=nl: (b * nl + i, 0)))
    return pl.pallas_call(
        functools.partial(_norm_proj_kernel, splits=tuple(splits)),
        grid=(bsz, nl),
        in_specs=[pl.BlockSpec((tm, D_MODEL), lambda b, i, nl=nl: (b * nl + i, 0)),
                  pl.BlockSpec((1, D_MODEL), lambda b, i: (0, 0)),
                  pl.BlockSpec((D_MODEL, n), lambda b, i: (0, 0))],
        out_specs=out_specs,
        out_shape=out_shapes,
        compiler_params=_cparams("parallel", "parallel"),
        name="norm_proj",
    )(x2d, gain.reshape(1, D_MODEL), w)


def _hyena_filter_kernel(z_ref, w1_ref, b1_ref, w2_ref, b2_ref, w3_ref, fr_ref, win_ref, o_ref):
    hp = lax.Precision.HIGHEST
    fr = fr_ref[...]
    h = jnp.sin(fr * (jnp.dot(z_ref[...], w1_ref[...], precision=hp, preferred_element_type=F32) + b1_ref[...]))
    h = jnp.sin(fr * (jnp.dot(h, w2_ref[...], precision=hp, preferred_element_type=F32) + b2_ref[...]))
    h = jnp.dot(h, w3_ref[...], precision=hp, preferred_element_type=F32)
    o_ref[...] = h * win_ref[...]


def hyena_filters(seq, w1, b1, w2, b2, w3, freq):
    t = jnp.linspace(0.0, 1.0, seq, dtype=F32)[:, None]
    ang = 2.0 * math.pi * jnp.arange(seq, dtype=F32)[:, None] / seq
    bands = jnp.linspace(1e-4, HY_BANDS - 1, HY_BANDS, dtype=F32)[None, :]
    z = jnp.concatenate([t, jnp.cos(bands * ang), -jnp.sin(bands * ang)], axis=-1)
    zp = jnp.pad(z, ((0, 0), (0, HY_FILT_HIDDEN - HY_EMB)))
    w1p = jnp.pad(w1.astype(F32), ((0, HY_FILT_HIDDEN - HY_EMB), (0, 0)))
    deltas = jnp.abs(jnp.linspace(HY_MIN_DECAY, HY_MAX_DECAY, HY_WIDTH, dtype=F32))
    window = jnp.exp(-t * deltas[None, :]) + HY_SHIFT
    win2 = jnp.concatenate([window, window], axis=-1)
    tl = min(512, seq)
    hh = HY_FILT_HIDDEN
    full = lambda r, c: pl.BlockSpec((r, c), lambda i: (0, 0))
    return pl.pallas_call(
        _hyena_filter_kernel,
        grid=(seq // tl,),
        in_specs=[pl.BlockSpec((tl, hh), lambda i: (i, 0)), full(hh, hh), full(1, hh), full(hh, hh), full(1, hh),
                  full(hh, 2 * HY_WIDTH), full(1, hh), pl.BlockSpec((tl, 2 * HY_WIDTH), lambda i: (i, 0))],
        out_specs=pl.BlockSpec((tl, 2 * HY_WIDTH), lambda i: (i, 0)),
        out_shape=jax.ShapeDtypeStruct((seq, 2 * HY_WIDTH), F32),
        compiler_params=_cparams("parallel"),
        name="hyena_filter",
    )(zp, w1p, b1.reshape(1, hh), w2, b2.reshape(1, hh), w3, freq.reshape(1, hh), win2)


def dft_matrices(seq):
    n = 2 * seq
    r = jnp.arange(seq, dtype=jnp.int32)[:, None]
    t = jnp.arange(n, dtype=jnp.int32)[None, :]
    ang = ((r * t) % n).astype(F32) * (2.0 * math.pi / n)
    c, s = jnp.cos(ang), jnp.sin(ang)
    nyq = jnp.where(t % 2 == 0, 1.0, -1.0).astype(F32)
    fwd = jnp.concatenate([c, jnp.where(r == 0, nyq, -s)], axis=0)
    ct = c[:, :seq].T
    st = s[:, :seq].T
    r_row = r.T
    inv_r = jnp.where(r_row == 0, 1.0, 2.0 * ct) / n
    inv_s = jnp.where(r_row == 0, nyq[:, :seq].T, -2.0 * st) / n
    inv = jnp.concatenate([inv_r, inv_s], axis=1)
    return fwd.astype(BF16), inv.astype(BF16)


def _kernel_dft_kernel(a_ref, khi_ref, klo_ref, o_ref):
    o_ref[...] = _dot(a_ref[...], khi_ref[...]) + _dot(a_ref[...], klo_ref[...])


def kernel_spectrum(fwd, k):
    n = fwd.shape[0]
    khi = k.astype(BF16)
    klo = (k - khi.astype(F32)).astype(BF16)
    tf = min(256, n)
    return pl.pallas_call(
        _kernel_dft_kernel,
        grid=(n // tf,),
        in_specs=[pl.BlockSpec((tf, n), lambda i: (i, 0)),
                  pl.BlockSpec((n, HY_WIDTH), lambda i: (0, 0)),
                  pl.BlockSpec((n, HY_WIDTH), lambda i: (0, 0))],
        out_specs=pl.BlockSpec((tf, HY_WIDTH), lambda i: (i, 0)),
        out_shape=jax.ShapeDtypeStruct((n, HY_WIDTH), F32),
        compiler_params=_cparams("parallel"),
        name="hyena_kernel_dft",
    )(fwd, khi, klo)


def _shift_down(u):
    rows = lax.broadcasted_iota(jnp.int32, u.shape, 0)
    return jnp.where(rows == 0, 0.0, pltpu.roll(u, 1, 0))


def _shift_up(u):
    n = u.shape[0]
    rows = lax.broadcasted_iota(jnp.int32, u.shape, 0)
    return jnp.where(rows == n - 1, 0.0, pltpu.roll(u, n - 1, 0))


def _hyena_pre_kernel(u_ref, w_ref, b_ref, vx_ref, x0_ref):
    def conv(c0):
        u = u_ref[:, c0:c0 + LANES].astype(F32)
        w = w_ref[:, c0:c0 + LANES]
        return _shift_down(u) * w[0:1] + u * w[1:2] + _shift_up(u) * w[2:3] + b_ref[:, c0:c0 + LANES]

    for c in range(0, HY_WIDTH, LANES):
        x0_ref[:, c:c + LANES] = conv(c).astype(x0_ref.dtype)
        vx_ref[:, c:c + LANES] = (conv(2 * HY_WIDTH + c) * conv(HY_WIDTH + c)).astype(vx_ref.dtype)


def hyena_pre(proj, conv_w, conv_b, bsz, seq):
    w3 = 3 * HY_WIDTH
    out = jax.ShapeDtypeStruct((bsz * seq, HY_WIDTH), BF16)
    return pl.pallas_call(
        _hyena_pre_kernel,
        grid=(bsz,),
        in_specs=[pl.BlockSpec((seq, w3), lambda b: (b, 0)),
                  pl.BlockSpec((3, w3), lambda b: (0, 0)),
                  pl.BlockSpec((1, w3), lambda b: (0, 0))],
        out_specs=[pl.BlockSpec((seq, HY_WIDTH), lambda b: (b, 0))] * 2,
        out_shape=[out, out],
        compiler_params=_cparams("parallel"),
        name="hyena_pre",
    )(proj, conv_w, conv_b.reshape(1, w3))


def _hyena_conv_kernel(vx_ref, x0_ref, ar_ref, as_ref, inv_ref, kp_ref, kq_ref, kp2_ref, d_ref,
                       o_ref, z_ref, *, nf, tf):
    step = pl.program_id(1)
    half = nf * tf

    @pl.when(step < nf)
    def _():
        vx = vx_ref[...]
        r = _dot(ar_ref[...], vx)
        s = _dot(as_ref[...], vx)
        kq = kq_ref[...]
        f0 = pl.multiple_of(step * tf, tf)
        z_ref[pl.ds(f0, tf), :] = (r * kp_ref[...] - s * kq).astype(BF16)
        z_ref[pl.ds(half + f0, tf), :] = (r * kq + s * kp2_ref[...]).astype(BF16)

    @pl.when(step >= nf)
    def _():
        t0 = pl.multiple_of((step - nf) * tf, tf)
        y = _dot(inv_ref[...], z_ref[...]) + vx_ref[pl.ds(t0, tf), :].astype(F32) * d_ref[...]
        o_ref[pl.ds(t0, tf), :] = (y * x0_ref[pl.ds(t0, tf), :].astype(F32)).astype(o_ref.dtype)


def hyena_conv(vx, x0, fwd, inv, kp, kq, kp2, d, bsz, seq):
    tf = min(512, seq)
    nf = seq // tf
    w = HY_WIDTH
    fwd_tile = lambda b, s: (jnp.minimum(s, nf - 1), 0)
    return pl.pallas_call(
        functools.partial(_hyena_conv_kernel, nf=nf, tf=tf),
        grid=(bsz, 2 * nf),
        in_specs=[pl.BlockSpec((seq, w), lambda b, s: (b, 0)),
                  pl.BlockSpec((seq, w), lambda b, s: (b, 0)),
                  pl.BlockSpec((tf, seq), fwd_tile),
                  pl.BlockSpec((tf, seq), lambda b, s: (nf + jnp.minimum(s, nf - 1), 0)),
                  pl.BlockSpec((tf, 2 * seq), lambda b, s: (jnp.maximum(s - nf, 0), 0)),
                  pl.BlockSpec((tf, w), fwd_tile),
                  pl.BlockSpec((tf, w), fwd_tile),
                  pl.BlockSpec((tf, w), fwd_tile),
                  pl.BlockSpec((1, w), lambda b, s: (0, 0))],
        out_specs=pl.BlockSpec((seq, w), lambda b, s: (b, 0)),
        out_shape=jax.ShapeDtypeStruct((bsz * seq, w), BF16),
        scratch_shapes=[pltpu.VMEM((2 * seq, w), BF16)],
        compiler_params=_cparams("parallel", "arbitrary"),
        name="hyena_conv",
    )(vx, x0, fwd, fwd, inv, kp, kq, kp2, d.reshape(1, w))


def _rel_bucket(rel):
    nb = REL_BUCKETS // 2
    max_exact = nb // 2
    ret = (rel > 0).astype(jnp.int32) * nb
    n = jnp.abs(rel)
    nf = jnp.maximum(n, 1).astype(F32)
    large = max_exact + (jnp.log(nf / max_exact) / math.log(REL_MAX_DIST / max_exact)
                         * (nb - max_exact)).astype(jnp.int32)
    large = jnp.minimum(large, nb - 1)
    return ret + jnp.where(n < max_exact, n, large)


def window_bias_mask(rel_bias):
    j = jnp.arange(WA_BLOCK, dtype=jnp.int32)[:, None]
    s = jnp.arange(3 * WA_BLOCK, dtype=jnp.int32)[None, :]
    rel = (s - WA_BLOCK) - j
    bias = jnp.transpose(rel_bias.astype(F32)[_rel_bucket(rel)], (2, 0, 1))
    band = jnp.abs(rel) <= WA_WINDOW
    return jnp.where(band[None], bias, NEG)


def _window_attn_kernel(sink_ref, q_ref, k_ref, v_ref, bias_ref, o_ref):
    nb = q_ref.shape[0] // WA_BLOCK
    scale = HEAD_DIM ** -0.5 * LOG2E
    rq = WA_REP * WA_BLOCK
    col = lax.broadcasted_iota(jnp.int32, (rq, 3 * WA_BLOCK), 1)
    head_of_row = lax.broadcasted_iota(jnp.int32, (rq, 1), 0) // WA_BLOCK

    def block(i, carry):
        ip = jnp.maximum(i - 1, 0)
        inx = jnp.minimum(i + 1, nb - 1)
        rows = lambda j: pl.ds(pl.multiple_of(j * WA_BLOCK, WA_BLOCK), WA_BLOCK)
        qb = q_ref[rows(i), :]
        kslab = jnp.concatenate([k_ref[rows(ip), :], k_ref[rows(i), :], k_ref[rows(inx), :]], axis=0)
        vslab = jnp.concatenate([v_ref[rows(ip), :], v_ref[rows(i), :], v_ref[rows(inx), :]], axis=0)
        lo = jnp.where(i > 0, 0, WA_BLOCK)
        hi = jnp.where(i < nb - 1, 3 * WA_BLOCK, 2 * WA_BLOCK)
        valid = jnp.logical_and(col >= lo, col < hi)
        outs = []
        for g in range(WA_KV_HEADS):
            heads = range(g * WA_REP, (g + 1) * WA_REP)
            q4 = jnp.concatenate([qb[:, h * HEAD_DIM:(h + 1) * HEAD_DIM] for h in heads], axis=0)
            kg = kslab[:, g * HEAD_DIM:(g + 1) * HEAD_DIM]
            vg = vslab[:, g * HEAD_DIM:(g + 1) * HEAD_DIM]
            s = _dot_nt(q4, kg) * scale + bias_ref[g]
            s = jnp.where(valid, s, NEG)
            sk = jnp.zeros((rq, 1), F32)
            for r, h in enumerate(heads):
                sk = jnp.where(head_of_row == r, sink_ref[h], sk)
            m = jnp.maximum(jnp.max(s, axis=-1, keepdims=True), sk)
            p = jnp.exp2(s - m)
            den = jnp.sum(p, axis=-1, keepdims=True) + jnp.exp2(sk - m)
            o4 = _dot(p.astype(BF16), vg) / den
            outs += [o4[r * WA_BLOCK:(r + 1) * WA_BLOCK] for r in range(WA_REP)]
        o_ref[rows(i), :] = jnp.concatenate(outs, axis=-1).astype(o_ref.dtype)
        return carry

    lax.fori_loop(0, nb, block, 0, unroll=2)


def window_attn(proj, sink, bias_mask, bsz, seq):
    hq = WA_HEADS * HEAD_DIM
    hkv = WA_KV_HEADS * HEAD_DIM
    q_blk = (3 * HY_WIDTH) // hq
    k_blk = (3 * HY_WIDTH + hq) // hkv
    bias2 = (bias_mask * LOG2E).reshape(WA_KV_HEADS, WA_REP * WA_BLOCK, 3 * WA_BLOCK)
    return pl.pallas_call(
        _window_attn_kernel,
        grid=(bsz,),
        in_specs=[pl.BlockSpec(memory_space=pltpu.SMEM),
                  pl.BlockSpec((seq, hq), lambda b: (b, q_blk)),
                  pl.BlockSpec((seq, hkv), lambda b: (b, k_blk)),
                  pl.BlockSpec((seq, hkv), lambda b: (b, k_blk + 1)),
                  pl.BlockSpec((WA_KV_HEADS, WA_REP * WA_BLOCK, 3 * WA_BLOCK), lambda b: (0, 0, 0))],
        out_specs=pl.BlockSpec((seq, hq), lambda b: (b, 0)),
        out_shape=jax.ShapeDtypeStruct((bsz * seq, hq), BF16),
        compiler_params=_cparams("parallel"),
        name="window_attn",
    )(sink.astype(F32) * LOG2E, proj, proj, proj, bias2)


def _out_proj_kernel(x_ref, a_ref, b_ref, w_ref, o_ref):
    acc = _dot(a_ref[...], w_ref[0:HALF, :]) + _dot(b_ref[...], w_ref[HALF:, :])
    o_ref[...] = x_ref[...] + acc


def out_proj_ab(x2d, y_a, y_b, w, tm=512):
    t = x2d.shape[0]
    tm = min(tm, t)
    return pl.pallas_call(
        _out_proj_kernel,
        grid=(t // tm,),
        in_specs=[pl.BlockSpec((tm, D_MODEL), lambda i: (i, 0)),
                  pl.BlockSpec((tm, HALF), lambda i: (i, 0)),
                  pl.BlockSpec((tm, HALF), lambda i: (i, 0)),
                  pl.BlockSpec((D_MODEL, D_MODEL), lambda i: (0, 0))],
        out_specs=pl.BlockSpec((tm, D_MODEL), lambda i: (i, 0)),
        out_shape=jax.ShapeDtypeStruct(x2d.shape, F32),
        compiler_params=_cparams("parallel"),
        name="out_proj_ab",
    )(x2d, y_a, y_b, w)


def s5_discretise(a_re, a_im, log_dt, b_re, b_im, c_re, c_im):
    lam = lax.complex(a_re.astype(F32), a_im.astype(F32))
    dt = jnp.exp(log_dt.astype(F32))[..., None]
    abar = jnp.exp(lam * dt)
    bmat = lax.complex(b_re.astype(F32), b_im.astype(F32))
    bbar = ((abar - 1.0) / lam)[..., None] * bmat
    cmat = lax.complex(c_re.astype(F32), c_im.astype(F32))
    nj, gl = S5_GROUPS // SUBLANES, SUBLANES
    eye = jnp.eye(gl, dtype=F32)
    a5 = jnp.stack([abar.real, abar.imag], axis=1).reshape(2, 2, nj, 1, gl * S5_STATE)
    a5 = jnp.transpose(a5, (0, 2, 1, 3, 4))
    a5 = jnp.broadcast_to(a5, (2, nj, 2, SUBLANES, gl * S5_STATE))

    def pack_b(x):
        x = x.reshape(2, nj, gl, S5_STATE, S5_GROUP)
        y = jnp.einsum('hg,djgpc->djhcgp', eye, x)
        return y.reshape(2, nj, gl * S5_GROUP, gl * S5_STATE)

    def pack_c(x):
        x = x.reshape(2, nj, gl, S5_GROUP, S5_STATE)
        y = jnp.einsum('hg,djgcp->djgphc', eye, x)
        return y.reshape(2, nj, gl * S5_STATE, gl * S5_GROUP)

    bm = jnp.concatenate([pack_b(bbar.real), pack_b(bbar.imag)], axis=-1).astype(BF16)
    cm = jnp.concatenate([pack_c(cmat.real), -pack_c(cmat.imag)], axis=-2).astype(BF16)
    return a5, bm, cm


def _s5_kernel(u_ref, a_ref, b_ref, c_ref, y_ref, buf_ref, h_ref, *, chunk):
    d = pl.program_id(1)
    nj = S5_GROUPS // SUBLANES
    sw = SUBLANES * S5_STATE
    rows = chunk * SUBLANES

    @pl.when(pl.program_id(2) == 0)
    def _():
        h_ref[...] = jnp.zeros_like(h_ref)

    u = u_ref[...].reshape(rows, HALF).astype(BF16)
    for j in range(nj):
        buf_ref[j] = _dot(u[:, j * LANES:(j + 1) * LANES], b_ref[0, j])

    for j0 in range(0, nj, 2):
        js = (j0, j0 + 1)

        def step(s, carry, js=js):
            t = jnp.where(d == 0, s, chunk - 1 - s)
            r0 = pl.multiple_of(t * SUBLANES, SUBLANES)
            new = []
            for n, j in enumerate(js):
                hr, hi = carry[2 * n], carry[2 * n + 1]
                ar = a_ref[0, j, 0]
                ai = a_ref[0, j, 1]
                br = buf_ref[j, pl.ds(r0, SUBLANES), 0:sw]
                bi = buf_ref[j, pl.ds(r0, SUBLANES), sw:2 * sw]
                nr = ar * hr - ai * hi + br
                ni = ar * hi + ai * hr + bi
                buf_ref[j, pl.ds(r0, SUBLANES), 0:sw] = nr
                buf_ref[j, pl.ds(r0, SUBLANES), sw:2 * sw] = ni
                new += [nr, ni]
            return tuple(new)

        init = tuple(h_ref[2 * j + k] for j in js for k in range(2))
        fin = lax.fori_loop(0, chunk, step, init, unroll=2)
        for n, j in enumerate(js):
            h_ref[2 * j] = fin[2 * n]
            h_ref[2 * j + 1] = fin[2 * n + 1]

    for j in range(nj):
        yj = _dot(buf_ref[j].astype(BF16), c_ref[0, j])
        y_ref[0, :, :, j * LANES:(j + 1) * LANES] = yj.reshape(chunk, SUBLANES, LANES).astype(y_ref.dtype)


def s5_scan(u_tm, a5, bm, cm, bsz, seq, chunk=64):
    chunk = min(chunk, seq)
    nc = seq // chunk
    nj = S5_GROUPS // SUBLANES
    sw = SUBLANES * S5_STATE

    def tchunk(d, i):
        return i + d * (nc - 1 - 2 * i)

    return pl.pallas_call(
        functools.partial(_s5_kernel, chunk=chunk),
        grid=(bsz // SUBLANES, 2, nc),
        in_specs=[pl.BlockSpec((chunk, SUBLANES, HALF), lambda b, d, i: (tchunk(d, i), b, 0)),
                  pl.BlockSpec((1, nj, 2, SUBLANES, sw), lambda b, d, i: (d, 0, 0, 0, 0)),
                  pl.BlockSpec((1, nj, LANES, 2 * sw), lambda b, d, i: (d, 0, 0, 0)),
                  pl.BlockSpec((1, nj, 2 * sw, LANES), lambda b, d, i: (d, 0, 0, 0))],
        out_specs=pl.BlockSpec((1, chunk, SUBLANES, HALF), lambda b, d, i: (d, tchunk(d, i), b, 0)),
        out_shape=jax.ShapeDtypeStruct((2, seq, bsz, HALF), F32),
        scratch_shapes=[pltpu.VMEM((nj, chunk * SUBLANES, 2 * sw), F32),
                        pltpu.VMEM((2 * nj, SUBLANES, sw), F32)],
        compiler_params=_cparams("parallel", "arbitrary", "arbitrary"),
        name="s5_scan",
    )(u_tm, a5, bm, cm)


def _gelu_tanh(x):
    return 0.5 * x * (1.0 + jnp.tanh(math.sqrt(2.0 / math.pi) * (x + 0.044715 * (x * x * x))))


def _out_proj_cd_kernel(x_ref, u_ref, yf_ref, yb_ref, mla_ref, d_ref, gw_ref, gb_ref, w_ref, o_ref):
    y = u_ref[...] * d_ref[...] + yf_ref[0] + yb_ref[0]
    g = _gelu_tanh(y)
    z = _dot(g.astype(BF16), gw_ref[...]) + gb_ref[...]
    y_s5 = g * jax.nn.sigmoid(z)
    acc = _dot(y_s5.astype(BF16), w_ref[0:HALF, :]) + _dot(mla_ref[...], w_ref[HALF:, :])
    o_ref[...] = x_ref[...] + acc


def out_proj_cd(x2d, u_tm2, y_tm2, y_mla, d, glu_w, glu_b, w, bsz, seq, tm=512):
    tm = min(tm, seq)
    nl = seq // tm
    row = lambda b, i: (b * nl + i, 0)
    const = lambda b, i: (0, 0)
    return pl.pallas_call(
        _out_proj_cd_kernel,
        grid=(bsz, nl),
        in_specs=[pl.BlockSpec((tm, D_MODEL), row),
                  pl.BlockSpec((tm, HALF), lambda b, i: (i, b)),
                  pl.BlockSpec((1, tm, HALF), lambda b, i: (0, i, b)),
                  pl.BlockSpec((1, tm, HALF), lambda b, i: (1, i, b)),
                  pl.BlockSpec((tm, HALF), row),
                  pl.BlockSpec((1, HALF), const),
                  pl.BlockSpec((HALF, HALF), const),
                  pl.BlockSpec((1, HALF), const),
                  pl.BlockSpec((D_MODEL, D_MODEL), const)],
        out_specs=pl.BlockSpec((tm, D_MODEL), row),
        out_shape=jax.ShapeDtypeStruct(x2d.shape, F32),
        compiler_params=_cparams("parallel", "parallel"),
        name="out_proj_cd",
    )(x2d, u_tm2, y_tm2, y_tm2, y_mla, d.reshape(1, HALF), glu_w, glu_b.reshape(1, HALF), w)


MLA_HP = 128


def mla_weights(w_uq, w_ukv):
    rq = w_uq.shape[0]
    wq = w_uq.astype(F32).reshape(rq, MLA_HEADS, MLA_NOPE + MLA_ROPE)
    half = MLA_ROPE // 2
    x1, x2 = wq[..., MLA_NOPE:MLA_NOPE + half], wq[..., MLA_NOPE + half:]
    zpad = jnp.zeros((rq, MLA_HEADS, MLA_HP - MLA_NOPE - MLA_ROPE), F32)
    wq1 = jnp.concatenate([wq, zpad], axis=-1).reshape(rq, MLA_HEADS * MLA_HP)
    wq2 = jnp.concatenate([jnp.zeros((rq, MLA_HEADS, MLA_NOPE), F32), -x2, x1, zpad], axis=-1)
    wq2 = wq2.reshape(rq, MLA_HEADS * MLA_HP)
    rk = w_ukv.shape[0]
    wkv = w_ukv.astype(F32).reshape(rk, MLA_HEADS, MLA_NOPE + MLA_V)
    wk = jnp.concatenate([wkv[..., :MLA_NOPE], jnp.zeros((rk, MLA_HEADS, MLA_HP - MLA_NOPE), F32)], axis=-1)
    wk = wk.reshape(rk, MLA_HEADS * MLA_HP)
    wv = wkv[..., MLA_NOPE:].reshape(rk, MLA_HEADS * MLA_V)
    return wq1.astype(BF16), wq2.astype(BF16), wk.astype(BF16), wv.astype(BF16)


def rope_tables(seq):
    inv = 1.0 / (ROPE_THETA ** (jnp.arange(0, MLA_ROPE, 2, dtype=F32) / MLA_ROPE))
    ang = jnp.arange(seq, dtype=F32)[:, None] * inv[None, :]
    c, s = jnp.cos(ang), jnp.sin(ang)
    ones = jnp.ones((seq, MLA_NOPE), F32)
    zpad = jnp.zeros((seq, MLA_HP - MLA_NOPE - MLA_ROPE), F32)
    cos_t = jnp.concatenate([ones, c, c, zpad], axis=-1)
    sin_t = jnp.concatenate([0.0 * ones, s, s, zpad], axis=-1)
    return cos_t, sin_t


def _mla_prep_kernel(r_ref, qg_ref, kg_ref, wq1_ref, wq2_ref, wk_ref, wv_ref, cos_ref, sin_ref,
                     q_ref, k_ref, v_ref):
    scale = (MLA_NOPE + MLA_ROPE) ** -0.5 * LOG2E
    cq = _rms(r_ref[:, 0:MLA_Q_RANK].astype(F32), qg_ref[...]).astype(BF16)
    o1 = MLA_Q_RANK + MLA_KV_RANK
    ckv = _rms(r_ref[:, MLA_Q_RANK:o1].astype(F32), kg_ref[...]).astype(BF16)
    cos_t, sin_t = cos_ref[...], sin_ref[...]
    kr = r_ref[:, o1:o1 + LANES].astype(F32) * cos_t + r_ref[:, o1 + LANES:o1 + 2 * LANES].astype(F32) * sin_t
    v_ref[...] = _dot(ckv, wv_ref[...]).astype(v_ref.dtype)
    for h in range(MLA_HEADS):
        sl = slice(h * MLA_HP, (h + 1) * MLA_HP)
        qh = _dot(cq, wq1_ref[:, sl]) * cos_t + _dot(cq, wq2_ref[:, sl]) * sin_t
        q_ref[:, sl] = (qh * scale).astype(q_ref.dtype)
        k_ref[:, sl] = (_dot(ckv, wk_ref[:, sl]) + kr).astype(k_ref.dtype)


def mla_prep(rest, q_norm, kv_norm, wq1, wq2, wk, wv, cos_t, sin_t, bsz, seq, tm=512):
    tm = min(tm, seq)
    nl = seq // tm
    t = bsz * seq
    wr = rest.shape[1]
    row = lambda b, i: (b * nl + i, 0)
    const = lambda b, i: (0, 0)
    qk = MLA_HEADS * MLA_HP
    return pl.pallas_call(
        _mla_prep_kernel,
        grid=(bsz, nl),
        in_specs=[pl.BlockSpec((tm, wr), row),
                  pl.BlockSpec((1, MLA_Q_RANK), const),
                  pl.BlockSpec((1, MLA_KV_RANK), const),
                  pl.BlockSpec((MLA_Q_RANK, qk), const),
                  pl.BlockSpec((MLA_Q_RANK, qk), const),
                  pl.BlockSpec((MLA_KV_RANK, qk), const),
                  pl.BlockSpec((MLA_KV_RANK, MLA_HEADS * MLA_V), const),
                  pl.BlockSpec((tm, MLA_HP), lambda b, i: (i, 0)),
                  pl.BlockSpec((tm, MLA_HP), lambda b, i: (i, 0))],
        out_specs=[pl.BlockSpec((tm, qk), row), pl.BlockSpec((tm, qk), row),
                   pl.BlockSpec((tm, MLA_HEADS * MLA_V), row)],
        out_shape=[jax.ShapeDtypeStruct((t, qk), BF16), jax.ShapeDtypeStruct((t, qk), BF16),
                   jax.ShapeDtypeStruct((t, MLA_HEADS * MLA_V), BF16)],
        compiler_params=_cparams("parallel", "parallel"),
        name="mla_prep",
    )(rest, q_norm.reshape(1, -1), kv_norm.reshape(1, -1), wq1, wq2, wk, wv, cos_t, sin_t)


def _mla_attn_kernel(q_ref, k_ref, v_ref, o_ref, *, kb):
    nk = k_ref.shape[0] // kb
    outs = []
    for h in range(2):
        sl = slice(h * MLA_HP, (h + 1) * MLA_HP)
        q = q_ref[:, sl]
        m = den = acc = None
        for j in range(nk):
            ks = slice(j * kb, (j + 1) * kb)
            s = _dot_nt(q, k_ref[ks, sl])
            mj = jnp.max(s, axis=-1, keepdims=True)
            if j == 0:
                m = mj
                p = jnp.exp2(s - m)
                den = jnp.sum(p, axis=-1, keepdims=True)
                acc = _dot(p.astype(BF16), v_ref[ks, :])
            else:
                m_new = jnp.maximum(m, mj)
                alpha = jnp.exp2(m - m_new)
                p = jnp.exp2(s - m_new)
                den = alpha * den + jnp.sum(p, axis=-1, keepdims=True)
                acc = alpha * acc + _dot(p.astype(BF16), v_ref[ks, :])
                m = m_new
        outs.append(acc / den)
    lane = lax.broadcasted_iota(jnp.int32, outs[0].shape, 1)
    o_ref[...] = jnp.where(lane < MLA_V, outs[0], outs[1]).astype(o_ref.dtype)


def mla_attn(q, k, v, bsz, seq, tq=512, kb=256):
    tq = min(tq, seq)
    nq = seq // tq
    return pl.pallas_call(
        functools.partial(_mla_attn_kernel, kb=min(kb, seq)),
        grid=(bsz, MLA_HEADS // 2, nq),
        in_specs=[pl.BlockSpec((tq, 2 * MLA_HP), lambda b, p, i: (b * nq + i, p)),
                  pl.BlockSpec((seq, 2 * MLA_HP), lambda b, p, i: (b, p)),
                  pl.BlockSpec((seq, 2 * MLA_V), lambda b, p, i: (b, p))],
        out_specs=pl.BlockSpec((tq, 2 * MLA_V), lambda b, p, i: (b * nq + i, p)),
        out_shape=jax.ShapeDtypeStruct((bsz * seq, MLA_HEADS * MLA_V), BF16),
        compiler_params=_cparams("parallel", "parallel", "parallel"),
        name="mla_attn",
    )(q, k, v)


def _cross_router_kernel(x_ref, kv_ref, gc_ref, wq_ref, wo_ref, gf_ref, wr_ref, xo_ref, hn_ref, aff_ref):
    x = x_ref[...]
    h = _rms(x, gc_ref[...]).astype(BF16)
    q = (_dot(h, wq_ref[...]) * (CA_HEAD_DIM ** -0.5 * LOG2E)).astype(BF16)
    hd = CA_HEADS * CA_HEAD_DIM
    outs = []
    for a in range(CA_HEADS):
        sl = slice(a * CA_HEAD_DIM, (a + 1) * CA_HEAD_DIM)
        s = _dot_nt(q[:, sl], kv_ref[:, sl])
        m = jnp.max(s, axis=-1, keepdims=True)
        p = jnp.exp2(s - m)
        den = jnp.sum(p, axis=-1, keepdims=True)
        outs.append((_dot(p.astype(BF16), kv_ref[:, hd + a * CA_HEAD_DIM:hd + (a + 1) * CA_HEAD_DIM]) / den))
    o = jnp.concatenate(outs, axis=-1).astype(BF16)
    xn = x + _dot(o, wo_ref[...])
    xo_ref[...] = xn
    hf = _rms(xn, gf_ref[...])
    hb = hf.astype(BF16)
    hn_ref[...] = hb
    lo = (hf - hb.astype(F32)).astype(BF16)
    hw = _dot(hb, wr_ref[...])
    logits = hw[:, 0:LANES] + (_dot(lo, wr_ref[:, 0:LANES]) + hw[:, LANES:])
    lane = lax.broadcasted_iota(jnp.int32, logits.shape, 1)
    logits = jnp.where(lane < N_EXPERTS, logits, NEG)
    m = jnp.max(logits, axis=-1, keepdims=True)
    e = jnp.exp(logits - m)
    aff = e / jnp.sum(e, axis=-1, keepdims=True)
    aff_ref[...] = aff[:, 0:N_EXPERTS]


def cross_router(x2d, kv, ln_cross, w_q, w_o, ln_ffn, w_router, bsz, seq, tq=256):
    tq = min(tq, seq)
    nq = seq // tq
    t = bsz * seq
    mem = kv.shape[0] // bsz
    hd = CA_HEADS * CA_HEAD_DIM
    wr = jnp.pad(w_router.astype(F32), ((0, 0), (0, LANES - N_EXPERTS)))
    wr_hi = wr.astype(BF16)
    wr2 = jnp.concatenate([wr_hi, (wr - wr_hi.astype(F32)).astype(BF16)], axis=1)
    row = lambda b, i: (b * nq + i, 0)
    const = lambda b, i: (0, 0)
    return pl.pallas_call(
        _cross_router_kernel,
        grid=(bsz, nq),
        in_specs=[pl.BlockSpec((tq, D_MODEL), row),
                  pl.BlockSpec((mem, 2 * hd), lambda b, i: (b, 0)),
                  pl.BlockSpec((1, D_MODEL), const),
                  pl.BlockSpec((D_MODEL, hd), const),
                  pl.BlockSpec((hd, D_MODEL), const),
                  pl.BlockSpec((1, D_MODEL), const),
                  pl.BlockSpec((D_MODEL, 2 * LANES), const)],
        out_specs=[pl.BlockSpec((tq, D_MODEL), row), pl.BlockSpec((tq, D_MODEL), row),
                   pl.BlockSpec((tq, N_EXPERTS), row)],
        out_shape=[jax.ShapeDtypeStruct((t, D_MODEL), F32), jax.ShapeDtypeStruct((t, D_MODEL), BF16),
                   jax.ShapeDtypeStruct((t, N_EXPERTS), F32)],
        compiler_params=_cparams("parallel", "parallel"),
        name="cross_router",
    )(x2d, kv, ln_cross.reshape(1, -1), w_q, w_o, ln_ffn.reshape(1, -1), wr2)


def _expert_ffn_kernel(x_ref, g_ref, wg_ref, wu_ref, wd_ref, o_ref, hid_ref, *, tf):
    x = x_ref[0]
    for c0 in range(0, D_EXPERT, tf):
        a = _dot(x, wg_ref[0, :, c0:c0 + tf])
        u = _dot(x, wu_ref[0, :, c0:c0 + tf])
        hid_ref[:, c0:c0 + tf] = (a * jax.nn.sigmoid(a) * u).astype(BF16)
    o_ref[0] = (_dot(hid_ref[...], wd_ref[0]) * g_ref[0]).astype(o_ref.dtype)


def expert_ffn(xe, gate, w_gate, w_up, w_down, tm=1024, tf=512):
    e, cap, _ = xe.shape
    tm = min(tm, cap)
    return pl.pallas_call(
        functools.partial(_expert_ffn_kernel, tf=tf),
        grid=(e, cap // tm),
        in_specs=[pl.BlockSpec((1, tm, D_MODEL), lambda e, m: (e, m, 0)),
                  pl.BlockSpec((1, tm, 1), lambda e, m: (e, m, 0)),
                  pl.BlockSpec((1, D_MODEL, D_EXPERT), lambda e, m: (e, 0, 0)),
                  pl.BlockSpec((1, D_MODEL, D_EXPERT), lambda e, m: (e, 0, 0)),
                  pl.BlockSpec((1, D_EXPERT, D_MODEL), lambda e, m: (e, 0, 0))],
        out_specs=pl.BlockSpec((1, tm, D_MODEL), lambda e, m: (e, m, 0)),
        out_shape=jax.ShapeDtypeStruct((e, cap, D_MODEL), BF16),
        scratch_shapes=[pltpu.VMEM((tm, D_EXPERT), BF16)],
        compiler_params=_cparams("parallel", "arbitrary"),
        name="expert_ffn",
    )(xe, gate, w_gate, w_up, w_down)


ROUTE_GROUP = SUBLANES


def _route_thr_kernel(a_ref, thr_ref, *, cap):
    bits = pltpu.bitcast(a_ref[...], jnp.int32)

    def body(i, lo):
        cand = lo | jnp.left_shift(jnp.int32(1), 30 - i)
        cnt = jnp.sum(jnp.where(bits >= cand, 1.0, 0.0), axis=1, keepdims=True)
        return jnp.where(cnt >= cap, cand, lo)

    thr_ref[...] = lax.fori_loop(0, 31, body, jnp.zeros(thr_ref.shape, jnp.int32))


def _prefix_rows(m, upper, lower):
    mb = m.astype(BF16)
    incl = _dot(mb, upper)
    tot = jnp.broadcast_to(incl[:, LANES - 1:LANES], incl.shape).astype(BF16)
    return incl - m + _dot(lower, tot)


def _route_mask_kernel(thr_ref, a_ref, upper_ref, lower_ref, sel_ref, pos_ref, *, cap):
    e = pl.program_id(0)
    thr = thr_ref[e]
    bits = pltpu.bitcast(a_ref[0], jnp.int32)
    gt = jnp.where(bits > thr, 1.0, 0.0)
    eq = jnp.where(bits == thr, 1.0, 0.0)
    need = cap - jnp.sum(jnp.sum(gt, axis=1, keepdims=True), axis=0, keepdims=True)
    eq_rank = _prefix_rows(eq, upper_ref[...], lower_ref[...])
    sel = gt + jnp.where(eq_rank < need, eq, 0.0)
    sel_ref[0] = sel
    pos_ref[0] = _prefix_rows(sel, upper_ref[...], lower_ref[...]).astype(jnp.int32)


def _route_compact_kernel(glo_ref, ghi_ref, a_ref, pos_ref, idx_ref, gate_ref):
    e = pl.program_id(0)
    nc = idx_ref.shape[1]
    gw = ROUTE_GROUP * LANES
    slot0 = lax.broadcasted_iota(jnp.int32, (LANES, LANES), 0)
    lane = lax.broadcasted_iota(jnp.int32, (1, gw), 1)
    zeros = jnp.zeros((2 * SUBLANES - 5, gw), F32)

    ngroups = pos_ref.shape[1] // ROUTE_GROUP
    never = jnp.int32(1 << 30)

    def terms(g, slot):
        r0 = pl.multiple_of(g * ROUTE_GROUP, ROUTE_GROUP)
        pos = pos_ref[0, pl.ds(r0, ROUTE_GROUP), :]
        aff = a_ref[0, pl.ds(r0, ROUTE_GROUP), :]
        hit = jnp.concatenate(
            [jnp.where(pos[j:j + 1] == slot, 1.0, 0.0) for j in range(ROUTE_GROUP)],
            axis=1).astype(BF16)
        arow = jnp.concatenate([aff[j:j + 1] for j in range(ROUTE_GROUP)], axis=1)
        tok = lane + g * gw
        g0 = arow.astype(BF16).astype(F32)
        r1 = arow - g0
        g1 = r1.astype(BF16).astype(F32)
        g2 = r1 - g1
        lhs = jnp.concatenate([(tok >> 8).astype(F32), (tok & 255).astype(F32), g0, g1, g2, zeros], axis=0)
        return lhs.astype(BF16), hit

    def chunk(c, carry):
        slot = slot0 + c * LANES
        g_first = glo_ref[e * nc + c]
        g_last = ghi_ref[e * nc + c]
        g_second = jnp.minimum(g_first + 1, ngroups - 1)
        lhs_a, hit_a = terms(g_first, slot)
        lhs_b, hit_b = terms(g_second, slot + jnp.where(g_first + 1 <= g_last, 0, never))
        acc = _dot_nt(jnp.concatenate([lhs_a, lhs_b], axis=1), jnp.concatenate([hit_a, hit_b], axis=1))

        def group(g, acc):
            lhs, hit = terms(g, slot)
            return acc + _dot_nt(lhs, hit)

        acc = lax.fori_loop(g_first + 2, g_last + 1, group, acc)
        idx_ref[0, pl.ds(c, 1), :] = (acc[0:1] * 256.0 + acc[1:2]).astype(jnp.int32)
        gate_ref[0, pl.ds(c, 1), :] = (acc[2:3] + acc[3:4]) + acc[4:5]
        return carry

    lax.fori_loop(0, nc, chunk, 0)


def route_tokens(aff, cap):
    t, ne = aff.shape
    rows = t // LANES
    nc = cap // LANES
    aff_t = aff.T
    thr = pl.pallas_call(
        functools.partial(_route_thr_kernel, cap=cap),
        out_shape=jax.ShapeDtypeStruct((ne, 1), jnp.int32),
        compiler_params=pltpu.CompilerParams(vmem_limit_bytes=V7X_VMEM_LIMIT_BYTES),
        name="route_threshold",
    )(aff_t)
    aff3 = aff_t.reshape(ne, rows, LANES)
    ii = jnp.arange(LANES)
    upper = (ii[:, None] <= ii[None, :]).astype(BF16)
    rr = jnp.arange(rows)
    lower = (rr[None, :] < rr[:, None]).astype(BF16)
    blk = pl.BlockSpec((1, rows, LANES), lambda e: (e, 0, 0))
    sel, pos = pl.pallas_call(
        functools.partial(_route_mask_kernel, cap=cap),
        grid=(ne,),
        in_specs=[pl.BlockSpec(memory_space=pltpu.SMEM), blk,
                  pl.BlockSpec((LANES, LANES), lambda e: (0, 0)),
                  pl.BlockSpec((rows, rows), lambda e: (0, 0))],
        out_specs=[blk, blk],
        out_shape=[jax.ShapeDtypeStruct((ne, rows, LANES), F32), jax.ShapeDtypeStruct((ne, rows, LANES), jnp.int32)],
        compiler_params=_cparams("parallel"),
        name="route_mask",
    )(thr.reshape(ne), aff3, upper, lower)
    gsz = ROUTE_GROUP
    first = pos[:, ::gsz, 0]
    starts = jnp.arange(nc, dtype=jnp.int32) * LANES
    glo = jnp.sum(first[:, None, :] <= starts[None, :, None], axis=-1, dtype=jnp.int32) - 1
    ghi = jnp.sum(first[:, None, :] <= (starts + (LANES - 1))[None, :, None], axis=-1, dtype=jnp.int32) - 1
    posm = jnp.where(sel > 0, pos, -1)
    out_blk = pl.BlockSpec((1, nc, LANES), lambda e, *_: (e, 0, 0))
    grid_spec = pltpu.PrefetchScalarGridSpec(
        num_scalar_prefetch=2,
        grid=(ne,),
        in_specs=[pl.BlockSpec((1, rows, LANES), lambda e, *_: (e, 0, 0))] * 2,
        out_specs=[out_blk, out_blk],
    )
    idx, gate = pl.pallas_call(
        _route_compact_kernel,
        grid_spec=grid_spec,
        out_shape=[jax.ShapeDtypeStruct((ne, nc, LANES), jnp.int32), jax.ShapeDtypeStruct((ne, nc, LANES), F32)],
        compiler_params=_cparams("parallel"),
        name="route_compact",
    )(glo.reshape(-1), ghi.reshape(-1), aff3, posm)
    return idx.reshape(ne, cap), gate.reshape(ne, cap), pos.reshape(ne, t), posm.reshape(ne, t)


COMBINE_ROWS = 64
COMBINE_XROWS = 128
COMBINE_ALIGN = 16
COMBINE_SUB = 256


def _combine_kernel(wst_ref, nex_ref, x_ref, pos_ref, rel_ref, spread_ref, ye_hbm, o_ref, ybuf, sem, xbuf, xsem,
                    *, nsub, cap):
    b = pl.program_id(0)
    nb = pl.num_programs(0)
    ne = N_EXPERTS
    w = COMBINE_ROWS
    xw = COMBINE_XROWS
    sub = COMBINE_SUB
    slot = b % 2
    lane = lax.broadcasted_iota(jnp.int32, (1, LANES), 1)
    row_in_window = (lax.broadcasted_iota(jnp.int32, (1, ne * w), 1) % w).astype(F32)

    def window(bb, sl, u, e):
        st = pl.multiple_of(wst_ref[(bb * nsub + u) * ne + e], COMBINE_ALIGN)
        return pltpu.make_async_copy(ye_hbm.at[e, pl.ds(st, w), :], ybuf.at[sl, u, pl.ds(e * w, w), :],
                                     sem.at[sl, u, e])

    def start_all(bb, sl):
        for u in range(nsub):
            for e in range(ne):
                window(bb, sl, u, e).start()

    @pl.when(b == 0)
    def _():
        start_all(b, slot)

    @pl.when(b + 1 < nb)
    def _():
        start_all(b + 1, 1 - slot)

    for u in range(nsub):
        rows = slice(u * sub, (u + 1) * sub)
        base = (b * nsub + u) * ne
        spread = _dot(rel_ref[rows, :], spread_ref[...])
        onehot = jnp.where(spread == row_in_window, 1.0, 0.0).astype(BF16)
        for e in range(ne):
            window(b, slot, u, e).wait()
        o_ref[rows, :] = x_ref[rows, :] + _dot(onehot, ybuf[slot, u])

        for e in range(ne):
            def extra(k, carry, e=e, rows=rows, base=base):
                first = wst_ref[base + e] + w + (k - 1) * xw
                st = pl.multiple_of(jnp.minimum(first, cap - xw), COMBINE_ALIGN)
                cp = pltpu.make_async_copy(ye_hbm.at[e, pl.ds(st, xw), :], xbuf, xsem.at[0])
                cp.start()
                cp.wait()
                col = jnp.broadcast_to(pos_ref[rows, e:e + 1], (sub, xw))
                hit = jnp.logical_and(col - st == lane, col >= first)
                o_ref[rows, :] += _dot(jnp.where(hit, 1.0, 0.0).astype(BF16), xbuf[...])
                return carry

            lax.fori_loop(1, nex_ref[base + e] + 1, extra, 0)


def moe_combine(x2d, ye, pos, posm, nsub=2):
    t = x2d.shape[0]
    ne, cap = ye.shape[0], ye.shape[1]
    sub = COMBINE_SUB
    nsub = min(nsub, t // sub)
    tb = sub * nsub
    nb = t // tb
    w = COMBINE_ROWS
    lo = pos[:, ::sub]
    hi = jnp.concatenate([lo[:, 1:], jnp.full((ne, 1), cap, jnp.int32)], axis=1)
    wst = jnp.minimum((lo // COMBINE_ALIGN) * COMBINE_ALIGN, cap - COMBINE_XROWS)
    nex = jnp.maximum(hi - (wst + w) + (COMBINE_XROWS - 1), 0) // COMBINE_XROWS
    posm = posm.T
    rel = posm - jnp.repeat(wst.T, sub, axis=0)
    rel = jnp.where(posm >= 0, jnp.minimum(rel, 2 * LANES - 1), -1).astype(BF16)
    spread = (jnp.arange(ne * w, dtype=jnp.int32)[None, :] // w == jnp.arange(ne, dtype=jnp.int32)[:, None])
    grid_spec = pltpu.PrefetchScalarGridSpec(
        num_scalar_prefetch=2,
        grid=(nb,),
        in_specs=[pl.BlockSpec((tb, D_MODEL), lambda b, *_: (b, 0)),
                  pl.BlockSpec((tb, ne), lambda b, *_: (b, 0)),
                  pl.BlockSpec((tb, ne), lambda b, *_: (b, 0)),
                  pl.BlockSpec((ne, ne * w), lambda b, *_: (0, 0)),
                  pl.BlockSpec(memory_space=pl.ANY)],
        out_specs=pl.BlockSpec((tb, D_MODEL), lambda b, *_: (b, 0)),
        scratch_shapes=[pltpu.VMEM((2, nsub, ne * w, D_MODEL), BF16),
                        pltpu.SemaphoreType.DMA((2, nsub, ne)),
                        pltpu.VMEM((COMBINE_XROWS, D_MODEL), BF16),
                        pltpu.SemaphoreType.DMA((1,))],
    )
    return pl.pallas_call(
        functools.partial(_combine_kernel, nsub=nsub, cap=cap),
        grid_spec=grid_spec,
        out_shape=jax.ShapeDtypeStruct(x2d.shape, F32),
        compiler_params=_cparams("arbitrary"),
        name="moe_combine",
    )(wst.T.reshape(-1), nex.T.reshape(-1), x2d, posm, rel, spread.astype(BF16), ye)


def _final_norm_kernel(x_ref, g_ref, o_ref):
    o_ref[...] = _rms(x_ref[...], g_ref[...])


def final_norm(x2d, g, tm=1024):
    t = x2d.shape[0]
    tm = min(tm, t)
    return pl.pallas_call(
        _final_norm_kernel,
        grid=(t // tm,),
        in_specs=[pl.BlockSpec((tm, D_MODEL), lambda i: (i, 0)), pl.BlockSpec((1, D_MODEL), lambda i: (0, 0))],
        out_specs=pl.BlockSpec((tm, D_MODEL), lambda i: (i, 0)),
        out_shape=jax.ShapeDtypeStruct(x2d.shape, F32),
        compiler_params=_cparams("parallel"),
        name="final_norm",
    )(x2d, g.reshape(1, -1))


def mixer_ab(x2d, e, p, shared, bsz, seq):
    proj = norm_proj(x2d, p['ln_mix_l'], p['w_in_ab'][e], bsz, seq, [(0, AB_IN, BF16, False)])[0]
    vx, x0 = hyena_pre(proj, p['hy_conv_w'][e], p['hy_conv_b'][e], bsz, seq)
    kp, kq, kp2 = shared['hy_spec'][e]
    y_hy = hyena_conv(vx, x0, shared['dft_fwd'], shared['dft_inv'], kp, kq, kp2, p['hy_d'][e], bsz, seq)
    y_wa = window_attn(proj, p['attn_sink'][e], shared['wa_bias'], bsz, seq)
    return out_proj_ab(x2d, y_hy, y_wa, p['w_out_ab'][e])


def mixer_cd(x2d, o, p, shared, bsz, seq):
    u_tm2, rest = norm_proj(x2d, p['ln_mix_l'], shared['w_in_cd'][o], bsz, seq,
                            [(0, HALF, F32, True), (HALF, CD_PAD - HALF, BF16, False)])
    a5, bm, cm = shared['s5'][o]
    y_tm = s5_scan(u_tm2.reshape(seq, bsz, HALF), a5, bm, cm, bsz, seq)
    wq1, wq2, wk, wv = shared['mla_w'][o]
    q, k, v = mla_prep(rest, p['mla_q_norm'][o], p['mla_kv_norm'][o], wq1, wq2, wk, wv,
                       shared['rope_cos'], shared['rope_sin'], bsz, seq)
    y_mla = mla_attn(q, k, v, bsz, seq)
    return out_proj_cd(x2d, u_tm2, y_tm.reshape(2, seq, bsz * HALF), y_mla, p['s5_d'][o],
                       p['s5_glu_w'][o], p['s5_glu_b'][o], p['w_out_cd'][o], bsz, seq)


def ec_moe(x2d, hn, aff, w_gate, w_up, w_down):
    t = x2d.shape[0]
    cap = EC_CAPACITY_FACTOR * t // N_EXPERTS
    idx, gate, pos, posm = route_tokens(aff, cap)
    xe = hn[idx]
    ye = expert_ffn(xe, gate[..., None], w_gate, w_up, w_down)
    return moe_combine(x2d, ye, pos, posm)


def prepare_shared(p, seq):
    sh = {}
    sh['wa_bias'] = window_bias_mask(p['rel_bias'])
    sh['dft_fwd'], sh['dft_inv'] = dft_matrices(seq)
    sh['rope_cos'], sh['rope_sin'] = rope_tables(seq)
    specs = []
    for e in range(p['w_in_ab'].shape[0]):
        h = hyena_filters(seq, p['hy_filt_w1'][e], p['hy_filt_b1'][e], p['hy_filt_w2'][e], p['hy_filt_b2'][e],
                          p['hy_filt_w3'][e], p['hy_filt_freq'][e])
        h_fwd, h_bwd = h[:, :HY_WIDTH], h[:, HY_WIDTH:]
        k = jnp.concatenate([h_fwd, jnp.zeros_like(h_fwd[:1]), h_bwd[:0:-1]], axis=0)
        kf = kernel_spectrum(sh['dft_fwd'], k)
        k_r, k_s = kf[:seq], kf[seq:]
        specs.append((k_r, k_s.at[0].set(0.0), k_r.at[0].set(k_s[0])))
    sh['hy_spec'] = specs
    s5, mla_w, w_in_cd = [], [], []
    for o in range(p['w_in_cd'].shape[0]):
        s5.append(s5_discretise(p['s5_a_re'][o], p['s5_a_im'][o], p['s5_log_dt'][o], p['s5_b_re'][o],
                                p['s5_b_im'][o], p['s5_c_re'][o], p['s5_c_im'][o]))
        mla_w.append(mla_weights(p['mla_w_uq'][o], p['mla_w_ukv'][o]))
        w = p['w_in_cd'][o].astype(F32)
        o2 = HALF + MLA_Q_RANK + MLA_KV_RANK
        kr = w[:, o2:o2 + MLA_ROPE]
        half = MLA_ROPE // 2
        kr_rot = jnp.concatenate([-kr[:, half:], kr[:, :half]], axis=1)
        z64 = jnp.zeros((D_MODEL, MLA_NOPE), F32)
        z32 = jnp.zeros((D_MODEL, MLA_HP - MLA_NOPE - MLA_ROPE), F32)
        w_in_cd.append(jnp.concatenate([w[:, :o2], z64, kr, z32, z64, kr_rot, z32], axis=1).astype(BF16))
    sh['s5'], sh['mla_w'], sh['w_in_cd'] = s5, mla_w, w_in_cd
    return sh


def run_trunk(x, mem, p, shared):
    bsz, seq, _ = x.shape
    x2d = x.reshape(bsz * seq, D_MODEL)
    mem2d = mem.reshape(bsz * mem.shape[1], D_MODEL)
    for layer in range(DEPTH):
        pl_ = dict(p, ln_mix_l=p['ln_mix'][layer])
        if layer % 2 == 0:
            x2d = mixer_ab(x2d, layer // 2, pl_, shared, bsz, seq)
        else:
            x2d = mixer_cd(x2d, layer // 2, pl_, shared, bsz, seq)
        kv = norm_proj(mem2d, p['ln_mem'][layer], p['ca_w_kv'][layer], bsz, mem.shape[1],
                       [(0, 2 * CA_HEADS * CA_HEAD_DIM, BF16, False)])[0]
        x2d, hn, aff = cross_router(x2d, kv, p['ln_cross'][layer], p['ca_w_q'][layer], p['ca_w_o'][layer],
                                    p['ln_ffn'][layer], p['moe_w_router'][layer], bsz, seq)
        x2d = ec_moe(x2d, hn, aff, p['moe_w_gate'][layer], p['moe_w_up'][layer], p['moe_w_down'][layer])
    return final_norm(x2d, p['ln_final']).reshape(bsz, seq, D_MODEL)


_BF16_WEIGHTS = ('w_in_ab', 'w_out_ab', 'w_out_cd', 's5_glu_w', 'ca_w_q', 'ca_w_kv', 'ca_w_o',
                 'moe_w_gate', 'moe_w_up', 'moe_w_down')


def kernel(x_prompt, x_sample, mem_prompt, mem_sample, ln_mix, ln_cross, ln_mem, ln_ffn, ln_final, rel_bias, w_in_ab, w_out_ab, hy_conv_w, hy_conv_b, hy_filt_w1, hy_filt_b1, hy_filt_w2, hy_filt_b2, hy_filt_w3, hy_filt_freq, hy_d, attn_sink, w_in_cd, w_out_cd, s5_a_re, s5_a_im, s5_log_dt, s5_b_re, s5_b_im, s5_c_re, s5_c_im, s5_d, s5_glu_w, s5_glu_b, mla_q_norm, mla_w_uq, mla_kv_norm, mla_w_ukv, ca_w_q, ca_w_kv, ca_w_o, moe_w_router, moe_w_gate, moe_w_up, moe_w_down):
    p = dict(ln_mix=ln_mix, ln_cross=ln_cross, ln_mem=ln_mem, ln_ffn=ln_ffn, ln_final=ln_final,
             rel_bias=rel_bias, w_in_ab=w_in_ab, w_out_ab=w_out_ab, hy_conv_w=hy_conv_w,
             hy_conv_b=hy_conv_b, hy_filt_w1=hy_filt_w1, hy_filt_b1=hy_filt_b1,
             hy_filt_w2=hy_filt_w2, hy_filt_b2=hy_filt_b2, hy_filt_w3=hy_filt_w3,
             hy_filt_freq=hy_filt_freq, hy_d=hy_d, attn_sink=attn_sink, w_in_cd=w_in_cd,
             w_out_cd=w_out_cd, s5_a_re=s5_a_re, s5_a_im=s5_a_im, s5_log_dt=s5_log_dt,
             s5_b_re=s5_b_re, s5_b_im=s5_b_im, s5_c_re=s5_c_re, s5_c_im=s5_c_im, s5_d=s5_d,
             s5_glu_w=s5_glu_w, s5_glu_b=s5_glu_b, mla_q_norm=mla_q_norm, mla_w_uq=mla_w_uq,
             mla_kv_norm=mla_kv_norm, mla_w_ukv=mla_w_ukv, ca_w_q=ca_w_q, ca_w_kv=ca_w_kv,
             ca_w_o=ca_w_o, moe_w_router=moe_w_router, moe_w_gate=moe_w_gate,
             moe_w_up=moe_w_up, moe_w_down=moe_w_down)
    assert x_prompt.shape[1] == x_sample.shape[1]
    shared = prepare_shared(p, x_prompt.shape[1])
    for name in _BF16_WEIGHTS:
        p[name] = p[name].astype(BF16)
    y_prompt = run_trunk(x_prompt, mem_prompt, p, shared)
    y_sample = run_trunk(x_sample, mem_sample, p, shared)
    return (y_prompt, y_sample)
```

```python
import functools
import math

import jax
import jax.numpy as jnp
import numpy as np
from jax import lax
from jax.experimental import pallas as pl
from jax.experimental.pallas import tpu as pltpu

D_MODEL = 1024
DEPTH = 4
HALF = 512
HEAD_DIM = 64
EPS = 1e-6
NEG = -1e30

HY_WIDTH = HALF
HY_EMB = 33
HY_BANDS = (HY_EMB - 1) // 2
HY_FILT_HIDDEN = 64
HY_DECAY_TARGET = 1e-2
HY_FAST = 0.3
HY_SLOW = 1.5
HY_MIN_DECAY = math.log(HY_DECAY_TARGET) / HY_SLOW
HY_MAX_DECAY = math.log(HY_DECAY_TARGET) / HY_FAST
HY_SHIFT = 0.05

WA_HEADS = 8
WA_KV_HEADS = 2
WA_REP = 4
WA_WINDOW = 128
WA_BLOCK = 128
REL_BUCKETS = 32
REL_MAX_DIST = 128

S5_GROUP = 16
S5_GROUPS = 32
S5_STATE = 64

MLA_HEADS = 8
MLA_NOPE = 64
MLA_ROPE = 32
MLA_V = 64
MLA_Q_RANK = 256
MLA_KV_RANK = 128
ROPE_THETA = 10000.0

CA_HEADS = 4
CA_HEAD_DIM = 128

N_EXPERTS = 16
EC_CAPACITY_FACTOR = 2
D_EXPERT = 2048

AB_IN = 3 * HY_WIDTH + (WA_HEADS + 2 * WA_KV_HEADS) * HEAD_DIM
CD_PAD = 1152

V7X_VMEM_LIMIT_BYTES = 56 * 1024 * 1024
LANES = 128
SUBLANES = 8
LOG2E = math.log2(math.e)

BF16 = jnp.bfloat16
F32 = jnp.float32


def _cparams(*sem):
    return pltpu.CompilerParams(dimension_semantics=sem, vmem_limit_bytes=V7X_VMEM_LIMIT_BYTES)


def _dot(a, b):
    return jnp.dot(a, b, preferred_element_type=F32)


def _dot_nt(a, b):
    return lax.dot_general(a, b, (((1,), (1,)), ((), ())), preferred_element_type=F32)


def _rms(xf, g):
    return xf * lax.rsqrt(jnp.mean(xf * xf, axis=-1, keepdims=True) + EPS) * g


def _norm_proj_kernel(x_ref, g_ref, w_ref, *out_refs, splits):
    hn = _rms(x_ref[...].astype(F32), g_ref[...]).astype(BF16)
    for o_ref, (start, width) in zip(out_refs, splits):
        for c0 in range(0, width, 512):
            cw = min(512, width - c0)
            o_ref[:, c0:c0 + cw] = _dot(hn, w_ref[:, start + c0:start + c0 + cw]).astype(o_ref.dtype)


def norm_proj(x2d, gain, w, bsz, seq, outs, tm=512):
    tm = min(tm, seq)
    nl = seq // tm
    n = w.shape[1]
    out_shapes, out_specs, splits = [], [], []
    for start, width, dtype, time_major in outs:
        splits.append((start, width))
        if time_major:
            out_shapes.append(jax.ShapeDtypeStruct((seq, bsz * width), dtype))
            out_specs.append(pl.BlockSpec((tm, width), lambda b, i: (i, b)))
        else:
            out_shapes.append(jax.ShapeDtypeStruct((bsz * seq, width), dtype))
            out_specs.append(pl.BlockSpec((tm, width), lambda b, i, nl=nl: (b * nl + i, 0)))
    return pl.pallas_call(
        functools.partial(_norm_proj_kernel, splits=tuple(splits)),
        grid=(bsz, nl),
        in_specs=[pl.BlockSpec((tm, D_MODEL), lambda b, i, nl=nl: (b * nl + i, 0)),
                  pl.BlockSpec((1, D_MODEL), lambda b, i: (0, 0)),
                  pl.BlockSpec((D_MODEL, n), lambda b, i: (0, 0))],
        out_specs=out_specs,
        out_shape=out_shapes,
        compiler_params=_cparams("parallel", "parallel"),
        name="norm_proj",
    )(x2d, gain.reshape(1, D_MODEL), w)


def _hyena_filter_kernel(z_ref, w1_ref, b1_ref, w2_ref, b2_ref, w3_ref, fr_ref, win_ref, o_ref):
    hp = lax.Precision.HIGHEST
    fr = fr_ref[...]
    h = jnp.sin(fr * (jnp.dot(z_ref[...], w1_ref[...], precision=hp, preferred_element_type=F32) + b1_ref[...]))
    h = jnp.sin(fr * (jnp.dot(h, w2_ref[...], precision=hp, preferred_element_type=F32) + b2_ref[...]))
    h = jnp.dot(h, w3_ref[...], precision=hp, preferred_element_type=F32)
    o_ref[...] = h * win_ref[...]


def hyena_filters(seq, w1, b1, w2, b2, w3, freq):
    t = jnp.linspace(0.0, 1.0, seq, dtype=F32)[:, None]
    ang = 2.0 * math.pi * jnp.arange(seq, dtype=F32)[:, None] / seq
    bands = jnp.linspace(1e-4, HY_BANDS - 1, HY_BANDS, dtype=F32)[None, :]
    z = jnp.concatenate([t, jnp.cos(bands * ang), -jnp.sin(bands * ang)], axis=-1)
    zp = jnp.pad(z, ((0, 0), (0, HY_FILT_HIDDEN - HY_EMB)))
    w1p = jnp.pad(w1.astype(F32), ((0, HY_FILT_HIDDEN - HY_EMB), (0, 0)))
    deltas = jnp.abs(jnp.linspace(HY_MIN_DECAY, HY_MAX_DECAY, HY_WIDTH, dtype=F32))
    window = jnp.exp(-t * deltas[None, :]) + HY_SHIFT
    win2 = jnp.concatenate([window, window], axis=-1)
    tl = min(512, seq)
    hh = HY_FILT_HIDDEN
    full = lambda r, c: pl.BlockSpec((r, c), lambda i: (0, 0))
    return pl.pallas_call(
        _hyena_filter_kernel,
        grid=(seq // tl,),
        in_specs=[pl.BlockSpec((tl, hh), lambda i: (i, 0)), full(hh, hh), full(1, hh), full(hh, hh), full(1, hh),
                  full(hh, 2 * HY_WIDTH), full(1, hh), pl.BlockSpec((tl, 2 * HY_WIDTH), lambda i: (i, 0))],
        out_specs=pl.BlockSpec((tl, 2 * HY_WIDTH), lambda i: (i, 0)),
        out_shape=jax.ShapeDtypeStruct((seq, 2 * HY_WIDTH), F32),
        compiler_params=_cparams("parallel"),
        name="hyena_filter",
    )(zp, w1p, b1.reshape(1, hh), w2, b2.reshape(1, hh), w3, freq.reshape(1, hh), win2)


def dft_matrices(seq):
    n = 2 * seq
    r = jnp.arange(seq, dtype=jnp.int32)[:, None]
    t = jnp.arange(n, dtype=jnp.int32)[None, :]
    ang = ((r * t) % n).astype(F32) * (2.0 * math.pi / n)
    c, s = jnp.cos(ang), jnp.sin(ang)
    nyq = jnp.where(t % 2 == 0, 1.0, -1.0).astype(F32)
    fwd = jnp.concatenate([c, jnp.where(r == 0, nyq, -s)], axis=0)
    ct = c[:, :seq].T
    st = s[:, :seq].T
    r_row = r.T
    inv_r = jnp.where(r_row == 0, 1.0, 2.0 * ct) / n
    inv_s = jnp.where(r_row == 0, nyq[:, :seq].T, -2.0 * st) / n
    inv = jnp.concatenate([inv_r, inv_s], axis=1)
    return fwd.astype(BF16), inv.astype(BF16)


def _kernel_dft_kernel(a_ref, khi_ref, klo_ref, o_ref):
    o_ref[...] = _dot(a_ref[...], khi_ref[...]) + _dot(a_ref[...], klo_ref[...])


def kernel_spectrum(fwd, k):
    n = fwd.shape[0]
    khi = k.astype(BF16)
    klo = (k - khi.astype(F32)).astype(BF16)
    tf = min(256, n)
    return pl.pallas_call(
        _kernel_dft_kernel,
        grid=(n // tf,),
        in_specs=[pl.BlockSpec((tf, n), lambda i: (i, 0)),
                  pl.BlockSpec((n, HY_WIDTH), lambda i: (0, 0)),
                  pl.BlockSpec((n, HY_WIDTH), lambda i: (0, 0))],
        out_specs=pl.BlockSpec((tf, HY_WIDTH), lambda i: (i, 0)),
        out_shape=jax.ShapeDtypeStruct((n, HY_WIDTH), F32),
        compiler_params=_cparams("parallel"),
        name="hyena_kernel_dft",
    )(fwd, khi, klo)


def _shift_down(u):
    rows = lax.broadcasted_iota(jnp.int32, u.shape, 0)
    return jnp.where(rows == 0, 0.0, pltpu.roll(u, 1, 0))


def _shift_up(u):
    n = u.shape[0]
    rows = lax.broadcasted_iota(jnp.int32, u.shape, 0)
    return jnp.where(rows == n - 1, 0.0, pltpu.roll(u, n - 1, 0))


def _hyena_pre_kernel(u_ref, w_ref, b_ref, vx_ref, x0_ref):
    def conv(c0):
        u = u_ref[:, c0:c0 + LANES].astype(F32)
        w = w_ref[:, c0:c0 + LANES]
        return _shift_down(u) * w[0:1] + u * w[1:2] + _shift_up(u) * w[2:3] + b_ref[:, c0:c0 + LANES]

    for c in range(0, HY_WIDTH, LANES):
        x0_ref[:, c:c + LANES] = conv(c).astype(x0_ref.dtype)
        vx_ref[:, c:c + LANES] = (conv(2 * HY_WIDTH + c) * conv(HY_WIDTH + c)).astype(vx_ref.dtype)


def hyena_pre(proj, conv_w, conv_b, bsz, seq):
    w3 = 3 * HY_WIDTH
    out = jax.ShapeDtypeStruct((bsz * seq, HY_WIDTH), BF16)
    return pl.pallas_call(
        _hyena_pre_kernel,
        grid=(bsz,),
        in_specs=[pl.BlockSpec((seq, w3), lambda b: (b, 0)),
                  pl.BlockSpec((3, w3), lambda b: (0, 0)),
                  pl.BlockSpec((1, w3), lambda b: (0, 0))],
        out_specs=[pl.BlockSpec((seq, HY_WIDTH), lambda b: (b, 0))] * 2,
        out_shape=[out, out],
        compiler_params=_cparams("parallel"),
        name="hyena_pre",
    )(proj, conv_w, conv_b.reshape(1, w3))


def _hyena_conv_kernel(vx_ref, x0_ref, ar_ref, as_ref, inv_ref, kp_ref, kq_ref, kp2_ref, d_ref,
                       o_ref, z_ref, *, nf, tf):
    step = pl.program_id(1)
    half = nf * tf

    @pl.when(step < nf)
    def _():
        vx = vx_ref[...]
        r = _dot(ar_ref[...], vx)
        s = _dot(as_ref[...], vx)
        kq = kq_ref[...]
        f0 = pl.multiple_of(step * tf, tf)
        z_ref[pl.ds(f0, tf), :] = (r * kp_ref[...] - s * kq).astype(BF16)
        z_ref[pl.ds(half + f0, tf), :] = (r * kq + s * kp2_ref[...]).astype(BF16)

    @pl.when(step >= nf)
    def _():
        t0 = pl.multiple_of((step - nf) * tf, tf)
        y = _dot(inv_ref[...], z_ref[...]) + vx_ref[pl.ds(t0, tf), :].astype(F32) * d_ref[...]
        o_ref[pl.ds(t0, tf), :] = (y * x0_ref[pl.ds(t0, tf), :].astype(F32)).astype(o_ref.dtype)


def hyena_conv(vx, x0, fwd, inv, kp, kq, kp2, d, bsz, seq):
    tf = min(512, seq)
    nf = seq // tf
    w = HY_WIDTH
    fwd_tile = lambda b, s: (jnp.minimum(s, nf - 1), 0)
    return pl.pallas_call(
        functools.partial(_hyena_conv_kernel, nf=nf, tf=tf),
        grid=(bsz, 2 * nf),
        in_specs=[pl.BlockSpec((seq, w), lambda b, s: (b, 0)),
                  pl.BlockSpec((seq, w), lambda b, s: (b, 0)),
                  pl.BlockSpec((tf, seq), fwd_tile),
                  pl.BlockSpec((tf, seq), lambda b, s: (nf + jnp.minimum(s, nf - 1), 0)),
                  pl.BlockSpec((tf, 2 * seq), lambda b, s: (jnp.maximum(s - nf, 0), 0)),
                  pl.BlockSpec((tf, w), fwd_tile),
                  pl.BlockSpec((tf, w), fwd_tile),
                  pl.BlockSpec((tf, w), fwd_tile),
                  pl.BlockSpec((1, w), lambda b, s: (0, 0))],
        out_specs=pl.BlockSpec((seq, w), lambda b, s: (b, 0)),
        out_shape=jax.ShapeDtypeStruct((bsz * seq, w), BF16),
        scratch_shapes=[pltpu.VMEM((2 * seq, w), BF16)],
        compiler_params=_cparams("parallel", "arbitrary"),
        name="hyena_conv",
    )(vx, x0, fwd, fwd, inv, kp, kq, kp2, d.reshape(1, w))


def _rel_bucket(rel):
    nb = REL_BUCKETS // 2
    max_exact = nb // 2
    ret = (rel > 0).astype(jnp.int32) * nb
    n = jnp.abs(rel)
    nf = jnp.maximum(n, 1).astype(F32)
    large = max_exact + (jnp.log(nf / max_exact) / math.log(REL_MAX_DIST / max_exact)
                         * (nb - max_exact)).astype(jnp.int32)
    large = jnp.minimum(large, nb - 1)
    return ret + jnp.where(n < max_exact, n, large)


def window_bias_mask(rel_bias):
    j = jnp.arange(WA_BLOCK, dtype=jnp.int32)[:, None]
    s = jnp.arange(3 * WA_BLOCK, dtype=jnp.int32)[None, :]
    rel = (s - WA_BLOCK) - j
    bias = jnp.transpose(rel_bias.astype(F32)[_rel_bucket(rel)], (2, 0, 1))
    band = jnp.abs(rel) <= WA_WINDOW
    return jnp.where(band[None], bias, NEG)


def _window_attn_kernel(sink_ref, q_ref, k_ref, v_ref, bias_ref, o_ref):
    nb = q_ref.shape[0] // WA_BLOCK
    scale = HEAD_DIM ** -0.5 * LOG2E
    rq = WA_REP * WA_BLOCK
    col = lax.broadcasted_iota(jnp.int32, (rq, 3 * WA_BLOCK), 1)
    head_of_row = lax.broadcasted_iota(jnp.int32, (rq, 1), 0) // WA_BLOCK

    def block(i, carry):
        ip = jnp.maximum(i - 1, 0)
        inx = jnp.minimum(i + 1, nb - 1)
        rows = lambda j: pl.ds(pl.multiple_of(j * WA_BLOCK, WA_BLOCK), WA_BLOCK)
        qb = q_ref[rows(i), :]
        kslab = jnp.concatenate([k_ref[rows(ip), :], k_ref[rows(i), :], k_ref[rows(inx), :]], axis=0)
        vslab = jnp.concatenate([v_ref[rows(ip), :], v_ref[rows(i), :], v_ref[rows(inx), :]], axis=0)
        lo = jnp.where(i > 0, 0, WA_BLOCK)
        hi = jnp.where(i < nb - 1, 3 * WA_BLOCK, 2 * WA_BLOCK)
        valid = jnp.logical_and(col >= lo, col < hi)
        outs = []
        for g in range(WA_KV_HEADS):
            heads = range(g * WA_REP, (g + 1) * WA_REP)
            q4 = jnp.concatenate([qb[:, h * HEAD_DIM:(h + 1) * HEAD_DIM] for h in heads], axis=0)
            kg = kslab[:, g * HEAD_DIM:(g + 1) * HEAD_DIM]
            vg = vslab[:, g * HEAD_DIM:(g + 1) * HEAD_DIM]
            s = _dot_nt(q4, kg) * scale + bias_ref[g]
            s = jnp.where(valid, s, NEG)
            sk = jnp.zeros((rq, 1), F32)
            for r, h in enumerate(heads):
                sk = jnp.where(head_of_row == r, sink_ref[h], sk)
            m = jnp.maximum(jnp.max(s, axis=-1, keepdims=True), sk)
            p = jnp.exp2(s - m)
            den = jnp.sum(p, axis=-1, keepdims=True) + jnp.exp2(sk - m)
            o4 = _dot(p.astype(BF16), vg) / den
            outs += [o4[r * WA_BLOCK:(r + 1) * WA_BLOCK] for r in range(WA_REP)]
        o_ref[rows(i), :] = jnp.concatenate(outs, axis=-1).astype(o_ref.dtype)
        return carry

    lax.fori_loop(0, nb, block, 0, unroll=2)


def window_attn(proj, sink, bias_mask, bsz, seq):
    hq = WA_HEADS * HEAD_DIM
    hkv = WA_KV_HEADS * HEAD_DIM
    q_blk = (3 * HY_WIDTH) // hq
    k_blk = (3 * HY_WIDTH + hq) // hkv
    bias2 = (bias_mask * LOG2E).reshape(WA_KV_HEADS, WA_REP * WA_BLOCK, 3 * WA_BLOCK)
    return pl.pallas_call(
        _window_attn_kernel,
        grid=(bsz,),
        in_specs=[pl.BlockSpec(memory_space=pltpu.SMEM),
                  pl.BlockSpec((seq, hq), lambda b: (b, q_blk)),
                  pl.BlockSpec((seq, hkv), lambda b: (b, k_blk)),
                  pl.BlockSpec((seq, hkv), lambda b: (b, k_blk + 1)),
                  pl.BlockSpec((WA_KV_HEADS, WA_REP * WA_BLOCK, 3 * WA_BLOCK), lambda b: (0, 0, 0))],
        out_specs=pl.BlockSpec((seq, hq), lambda b: (b, 0)),
        out_shape=jax.ShapeDtypeStruct((bsz * seq, hq), BF16),
        compiler_params=_cparams("parallel"),
        name="window_attn",
    )(sink.astype(F32) * LOG2E, proj, proj, proj, bias2)


def _out_proj_kernel(x_ref, a_ref, b_ref, w_ref, o_ref):
    acc = _dot(a_ref[...], w_ref[0:HALF, :]) + _dot(b_ref[...], w_ref[HALF:, :])
    o_ref[...] = x_ref[...] + acc


def out_proj_ab(x2d, y_a, y_b, w, tm=512):
    t = x2d.shape[0]
    tm = min(tm, t)
    return pl.pallas_call(
        _out_proj_kernel,
        grid=(t // tm,),
        in_specs=[pl.BlockSpec((tm, D_MODEL), lambda i: (i, 0)),
                  pl.BlockSpec((tm, HALF), lambda i: (i, 0)),
                  pl.BlockSpec((tm, HALF), lambda i: (i, 0)),
                  pl.BlockSpec((D_MODEL, D_MODEL), lambda i: (0, 0))],
        out_specs=pl.BlockSpec((tm, D_MODEL), lambda i: (i, 0)),
        out_shape=jax.ShapeDtypeStruct(x2d.shape, F32),
        compiler_params=_cparams("parallel"),
        name="out_proj_ab",
    )(x2d, y_a, y_b, w)


def s5_discretise(a_re, a_im, log_dt, b_re, b_im, c_re, c_im):
    lam = lax.complex(a_re.astype(F32), a_im.astype(F32))
    dt = jnp.exp(log_dt.astype(F32))[..., None]
    abar = jnp.exp(lam * dt)
    bmat = lax.complex(b_re.astype(F32), b_im.astype(F32))
    bbar = ((abar - 1.0) / lam)[..., None] * bmat
    cmat = lax.complex(c_re.astype(F32), c_im.astype(F32))
    nj, gl = S5_GROUPS // SUBLANES, SUBLANES
    eye = jnp.eye(gl, dtype=F32)
    a5 = jnp.stack([abar.real, abar.imag], axis=1).reshape(2, 2, nj, 1, gl * S5_STATE)
    a5 = jnp.transpose(a5, (0, 2, 1, 3, 4))
    a5 = jnp.broadcast_to(a5, (2, nj, 2, SUBLANES, gl * S5_STATE))

    def pack_b(x):
        x = x.reshape(2, nj, gl, S5_STATE, S5_GROUP)
        y = jnp.einsum('hg,djgpc->djhcgp', eye, x)
        return y.reshape(2, nj, gl * S5_GROUP, gl * S5_STATE)

    def pack_c(x):
        x = x.reshape(2, nj, gl, S5_GROUP, S5_STATE)
        y = jnp.einsum('hg,djgcp->djgphc', eye, x)
        return y.reshape(2, nj, gl * S5_STATE, gl * S5_GROUP)

    bm = jnp.concatenate([pack_b(bbar.real), pack_b(bbar.imag)], axis=-1).astype(BF16)
    cm = jnp.concatenate([pack_c(cmat.real), -pack_c(cmat.imag)], axis=-2).astype(BF16)
    return a5, bm, cm


def _s5_kernel(u_ref, a_ref, b_ref, c_ref, y_ref, buf_ref, h_ref, *, chunk):
    d = pl.program_id(1)
    nj = S5_GROUPS // SUBLANES
    sw = SUBLANES * S5_STATE
    rows = chunk * SUBLANES

    @pl.when(pl.program_id(2) == 0)
    def _():
        h_ref[...] = jnp.zeros_like(h_ref)

    u = pltpu.einshape("btc->(tb)c", u_ref[...]).astype(BF16)
    for j in range(nj):
        buf_ref[j] = _dot(u[:, j * LANES:(j + 1) * LANES], b_ref[0, j])

    for j0 in range(0, nj, 2):
        js = (j0, j0 + 1)

        def step(s, carry, js=js):
            t = jnp.where(d == 0, s, chunk - 1 - s)
            r0 = pl.multiple_of(t * SUBLANES, SUBLANES)
            new = []
            for n, j in enumerate(js):
                hr, hi = carry[2 * n], carry[2 * n + 1]
                ar = a_ref[0, j, 0]
                ai = a_ref[0, j, 1]
                br = buf_ref[j, pl.ds(r0, SUBLANES), 0:sw]
                bi = buf_ref[j, pl.ds(r0, SUBLANES), sw:2 * sw]
                nr = ar * hr - ai * hi + br
                ni = ar * hi + ai * hr + bi
                buf_ref[j, pl.ds(r0, SUBLANES), 0:sw] = nr
                buf_ref[j, pl.ds(r0, SUBLANES), sw:2 * sw] = ni
                new += [nr, ni]
            return tuple(new)

        init = tuple(h_ref[2 * j + k] for j in js for k in range(2))
        fin = lax.fori_loop(0, chunk, step, init, unroll=2)
        for n, j in enumerate(js):
            h_ref[2 * j] = fin[2 * n]
            h_ref[2 * j + 1] = fin[2 * n + 1]

    for j in range(nj):
        yj = _dot(buf_ref[j].astype(BF16), c_ref[0, j])
        y_ref[0, :, :, j * LANES:(j + 1) * LANES] = pltpu.einshape(
            "(tb)c->btc", yj, b=SUBLANES).astype(y_ref.dtype)


def s5_scan(u3, a5, bm, cm, bsz, seq, chunk=64):
    chunk = min(chunk, seq)
    nc = seq // chunk
    nj = S5_GROUPS // SUBLANES
    sw = SUBLANES * S5_STATE

    def tchunk(d, i):
        return i + d * (nc - 1 - 2 * i)

    return pl.pallas_call(
        functools.partial(_s5_kernel, chunk=chunk),
        grid=(bsz // SUBLANES, 2, nc),
        in_specs=[pl.BlockSpec((SUBLANES, chunk, HALF), lambda b, d, i: (b, tchunk(d, i), 0)),
                  pl.BlockSpec((1, nj, 2, SUBLANES, sw), lambda b, d, i: (d, 0, 0, 0, 0)),
                  pl.BlockSpec((1, nj, LANES, 2 * sw), lambda b, d, i: (d, 0, 0, 0)),
                  pl.BlockSpec((1, nj, 2 * sw, LANES), lambda b, d, i: (d, 0, 0, 0))],
        out_specs=pl.BlockSpec((1, SUBLANES, chunk, HALF), lambda b, d, i: (d, b, tchunk(d, i), 0)),
        out_shape=jax.ShapeDtypeStruct((2, bsz, seq, HALF), F32),
        scratch_shapes=[pltpu.VMEM((nj, chunk * SUBLANES, 2 * sw), F32),
                        pltpu.VMEM((2 * nj, SUBLANES, sw), F32)],
        compiler_params=_cparams("parallel", "arbitrary", "arbitrary"),
        name="s5_scan",
    )(u3, a5, bm, cm)


def _gelu_tanh(x):
    return 0.5 * x * (1.0 + jnp.tanh(math.sqrt(2.0 / math.pi) * (x + 0.044715 * (x * x * x))))


def _out_proj_cd_kernel(x_ref, u_ref, yf_ref, yb_ref, mla_ref, d_ref, gw_ref, gb_ref, w_ref, o_ref):
    y = u_ref[...] * d_ref[...] + yf_ref[0] + yb_ref[0]
    g = _gelu_tanh(y)
    z = _dot(g.astype(BF16), gw_ref[...]) + gb_ref[...]
    y_s5 = g * jax.nn.sigmoid(z)
    acc = _dot(y_s5.astype(BF16), w_ref[0:HALF, :]) + _dot(mla_ref[...], w_ref[HALF:, :])
    o_ref[...] = x_ref[...] + acc


def out_proj_cd(x2d, u2d, y2, y_mla, d, glu_w, glu_b, w, bsz, seq, tm=512):
    tm = min(tm, seq)
    nl = seq // tm
    row = lambda b, i: (b * nl + i, 0)
    const = lambda b, i: (0, 0)
    return pl.pallas_call(
        _out_proj_cd_kernel,
        grid=(bsz, nl),
        in_specs=[pl.BlockSpec((tm, D_MODEL), row),
                  pl.BlockSpec((tm, HALF), row),
                  pl.BlockSpec((1, tm, HALF), lambda b, i: (0, b * nl + i, 0)),
                  pl.BlockSpec((1, tm, HALF), lambda b, i: (1, b * nl + i, 0)),
                  pl.BlockSpec((tm, HALF), row),
                  pl.BlockSpec((1, HALF), const),
                  pl.BlockSpec((HALF, HALF), const),
                  pl.BlockSpec((1, HALF), const),
                  pl.BlockSpec((D_MODEL, D_MODEL), const)],
        out_specs=pl.BlockSpec((tm, D_MODEL), row),
        out_shape=jax.ShapeDtypeStruct(x2d.shape, F32),
        compiler_params=_cparams("parallel", "parallel"),
        name="out_proj_cd",
    )(x2d, u2d, y2, y2, y_mla, d.reshape(1, HALF), glu_w, glu_b.reshape(1, HALF), w)


MLA_HP = 128


def mla_weights(w_uq, w_ukv):
    rq = w_uq.shape[0]
    wq = w_uq.astype(F32).reshape(rq, MLA_HEADS, MLA_NOPE + MLA_ROPE)
    half = MLA_ROPE // 2
    x1, x2 = wq[..., MLA_NOPE:MLA_NOPE + half], wq[..., MLA_NOPE + half:]
    zpad = jnp.zeros((rq, MLA_HEADS, MLA_HP - MLA_NOPE - MLA_ROPE), F32)
    wq1 = jnp.concatenate([wq, zpad], axis=-1).reshape(rq, MLA_HEADS * MLA_HP)
    wq2 = jnp.concatenate([jnp.zeros((rq, MLA_HEADS, MLA_NOPE), F32), -x2, x1, zpad], axis=-1)
    wq2 = wq2.reshape(rq, MLA_HEADS * MLA_HP)
    rk = w_ukv.shape[0]
    wkv = w_ukv.astype(F32).reshape(rk, MLA_HEADS, MLA_NOPE + MLA_V)
    wk = jnp.concatenate([wkv[..., :MLA_NOPE], jnp.zeros((rk, MLA_HEADS, MLA_HP - MLA_NOPE), F32)], axis=-1)
    wk = wk.reshape(rk, MLA_HEADS * MLA_HP)
    wv = wkv[..., MLA_NOPE:].reshape(rk, MLA_HEADS * MLA_V)
    return wq1.astype(BF16), wq2.astype(BF16), wk.astype(BF16), wv.astype(BF16)


def rope_tables(seq):
    inv = 1.0 / (ROPE_THETA ** (jnp.arange(0, MLA_ROPE, 2, dtype=F32) / MLA_ROPE))
    ang = jnp.arange(seq, dtype=F32)[:, None] * inv[None, :]
    c, s = jnp.cos(ang), jnp.sin(ang)
    ones = jnp.ones((seq, MLA_NOPE), F32)
    zpad = jnp.zeros((seq, MLA_HP - MLA_NOPE - MLA_ROPE), F32)
    cos_t = jnp.concatenate([ones, c, c, zpad], axis=-1)
    sin_t = jnp.concatenate([0.0 * ones, s, s, zpad], axis=-1)
    return cos_t, sin_t


def _mla_prep_kernel(r_ref, qg_ref, kg_ref, wq1_ref, wq2_ref, wk_ref, wv_ref, cos_ref, sin_ref,
                     q_ref, k_ref, v_ref):
    scale = (MLA_NOPE + MLA_ROPE) ** -0.5 * LOG2E
    cq = _rms(r_ref[:, 0:MLA_Q_RANK].astype(F32), qg_ref[...]).astype(BF16)
    o1 = MLA_Q_RANK + MLA_KV_RANK
    ckv = _rms(r_ref[:, MLA_Q_RANK:o1].astype(F32), kg_ref[...]).astype(BF16)
    cos_t, sin_t = cos_ref[...], sin_ref[...]
    kr = r_ref[:, o1:o1 + LANES].astype(F32) * cos_t + r_ref[:, o1 + LANES:o1 + 2 * LANES].astype(F32) * sin_t
    v_ref[...] = _dot(ckv, wv_ref[...]).astype(v_ref.dtype)
    for h in range(MLA_HEADS):
        sl = slice(h * MLA_HP, (h + 1) * MLA_HP)
        qh = _dot(cq, wq1_ref[:, sl]) * cos_t + _dot(cq, wq2_ref[:, sl]) * sin_t
        q_ref[:, sl] = (qh * scale).astype(q_ref.dtype)
        k_ref[:, sl] = (_dot(ckv, wk_ref[:, sl]) + kr).astype(k_ref.dtype)


def mla_prep(rest, q_norm, kv_norm, wq1, wq2, wk, wv, cos_t, sin_t, bsz, seq, tm=512):
    tm = min(tm, seq)
    nl = seq // tm
    t = bsz * seq
    wr = rest.shape[1]
    row = lambda b, i: (b * nl + i, 0)
    const = lambda b, i: (0, 0)
    qk = MLA_HEADS * MLA_HP
    return pl.pallas_call(
        _mla_prep_kernel,
        grid=(bsz, nl),
        in_specs=[pl.BlockSpec((tm, wr), row),
                  pl.BlockSpec((1, MLA_Q_RANK), const),
                  pl.BlockSpec((1, MLA_KV_RANK), const),
                  pl.BlockSpec((MLA_Q_RANK, qk), const),
                  pl.BlockSpec((MLA_Q_RANK, qk), const),
                  pl.BlockSpec((MLA_KV_RANK, qk), const),
                  pl.BlockSpec((MLA_KV_RANK, MLA_HEADS * MLA_V), const),
                  pl.BlockSpec((tm, MLA_HP), lambda b, i: (i, 0)),
                  pl.BlockSpec((tm, MLA_HP), lambda b, i: (i, 0))],
        out_specs=[pl.BlockSpec((tm, qk), row), pl.BlockSpec((tm, qk), row),
                   pl.BlockSpec((tm, MLA_HEADS * MLA_V), row)],
        out_shape=[jax.ShapeDtypeStruct((t, qk), BF16), jax.ShapeDtypeStruct((t, qk), BF16),
                   jax.ShapeDtypeStruct((t, MLA_HEADS * MLA_V), BF16)],
        compiler_params=_cparams("parallel", "parallel"),
        name="mla_prep",
    )(rest, q_norm.reshape(1, -1), kv_norm.reshape(1, -1), wq1, wq2, wk, wv, cos_t, sin_t)


def _mla_attn_kernel(q_ref, k_ref, v_ref, o_ref, *, kb):
    nk = k_ref.shape[0] // kb
    outs = []
    for h in range(2):
        sl = slice(h * MLA_HP, (h + 1) * MLA_HP)
        q = q_ref[:, sl]
        m = den = acc = None
        for j in range(nk):
            ks = slice(j * kb, (j + 1) * kb)
            s = _dot_nt(q, k_ref[ks, sl])
            mj = jnp.max(s, axis=-1, keepdims=True)
            if j == 0:
                m = mj
                p = jnp.exp2(s - m)
                den = jnp.sum(p, axis=-1, keepdims=True)
                acc = _dot(p.astype(BF16), v_ref[ks, :])
            else:
                m_new = jnp.maximum(m, mj)
                alpha = jnp.exp2(m - m_new)
                p = jnp.exp2(s - m_new)
                den = alpha * den + jnp.sum(p, axis=-1, keepdims=True)
                acc = alpha * acc + _dot(p.astype(BF16), v_ref[ks, :])
                m = m_new
        outs.append(acc / den)
    lane = lax.broadcasted_iota(jnp.int32, outs[0].shape, 1)
    o_ref[...] = jnp.where(lane < MLA_V, outs[0], outs[1]).astype(o_ref.dtype)


def mla_attn(q, k, v, bsz, seq, tq=512, kb=256):
    tq = min(tq, seq)
    nq = seq // tq
    return pl.pallas_call(
        functools.partial(_mla_attn_kernel, kb=min(kb, seq)),
        grid=(bsz, MLA_HEADS // 2, nq),
        in_specs=[pl.BlockSpec((tq, 2 * MLA_HP), lambda b, p, i: (b * nq + i, p)),
                  pl.BlockSpec((seq, 2 * MLA_HP), lambda b, p, i: (b, p)),
                  pl.BlockSpec((seq, 2 * MLA_V), lambda b, p, i: (b, p))],
        out_specs=pl.BlockSpec((tq, 2 * MLA_V), lambda b, p, i: (b * nq + i, p)),
        out_shape=jax.ShapeDtypeStruct((bsz * seq, MLA_HEADS * MLA_V), BF16),
        compiler_params=_cparams("parallel", "parallel", "parallel"),
        name="mla_attn",
    )(q, k, v)


def _cross_router_kernel(x_ref, kv_ref, gc_ref, wq_ref, wo_ref, gf_ref, wr_ref, xo_ref, hn_ref, aff_ref):
    x = x_ref[...]
    h = _rms(x, gc_ref[...]).astype(BF16)
    q = (_dot(h, wq_ref[...]) * (CA_HEAD_DIM ** -0.5 * LOG2E)).astype(BF16)
    hd = CA_HEADS * CA_HEAD_DIM
    outs = []
    for a in range(CA_HEADS):
        sl = slice(a * CA_HEAD_DIM, (a + 1) * CA_HEAD_DIM)
        s = _dot_nt(q[:, sl], kv_ref[:, sl])
        m = jnp.max(s, axis=-1, keepdims=True)
        p = jnp.exp2(s - m)
        den = jnp.sum(p, axis=-1, keepdims=True)
        outs.append((_dot(p.astype(BF16), kv_ref[:, hd + a * CA_HEAD_DIM:hd + (a + 1) * CA_HEAD_DIM]) / den))
    o = jnp.concatenate(outs, axis=-1).astype(BF16)
    xn = x + _dot(o, wo_ref[...])
    xo_ref[...] = xn
    hf = _rms(xn, gf_ref[...])
    hb = hf.astype(BF16)
    hn_ref[...] = hb
    lo = (hf - hb.astype(F32)).astype(BF16)
    hw = _dot(hb, wr_ref[...])
    logits = hw[:, 0:LANES] + (_dot(lo, wr_ref[:, 0:LANES]) + hw[:, LANES:])
    lane = lax.broadcasted_iota(jnp.int32, logits.shape, 1)
    logits = jnp.where(lane < N_EXPERTS, logits, NEG)
    m = jnp.max(logits, axis=-1, keepdims=True)
    e = jnp.exp(logits - m)
    aff = e / jnp.sum(e, axis=-1, keepdims=True)
    aff_ref[...] = aff[:, 0:N_EXPERTS]


def cross_router(x2d, kv, ln_cross, w_q, w_o, ln_ffn, w_router, bsz, seq, tq=256):
    tq = min(tq, seq)
    nq = seq // tq
    t = bsz * seq
    mem = kv.shape[0] // bsz
    hd = CA_HEADS * CA_HEAD_DIM
    wr = jnp.pad(w_router.astype(F32), ((0, 0), (0, LANES - N_EXPERTS)))
    wr_hi = wr.astype(BF16)
    wr2 = jnp.concatenate([wr_hi, (wr - wr_hi.astype(F32)).astype(BF16)], axis=1)
    row = lambda b, i: (b * nq + i, 0)
    const = lambda b, i: (0, 0)
    return pl.pallas_call(
        _cross_router_kernel,
        grid=(bsz, nq),
        in_specs=[pl.BlockSpec((tq, D_MODEL), row),
                  pl.BlockSpec((mem, 2 * hd), lambda b, i: (b, 0)),
                  pl.BlockSpec((1, D_MODEL), const),
                  pl.BlockSpec((D_MODEL, hd), const),
                  pl.BlockSpec((hd, D_MODEL), const),
                  pl.BlockSpec((1, D_MODEL), const),
                  pl.BlockSpec((D_MODEL, 2 * LANES), const)],
        out_specs=[pl.BlockSpec((tq, D_MODEL), row), pl.BlockSpec((tq, D_MODEL), row),
                   pl.BlockSpec((tq, N_EXPERTS), row)],
        out_shape=[jax.ShapeDtypeStruct((t, D_MODEL), F32), jax.ShapeDtypeStruct((t, D_MODEL), BF16),
                   jax.ShapeDtypeStruct((t, N_EXPERTS), F32)],
        compiler_params=_cparams("parallel", "parallel"),
        name="cross_router",
    )(x2d, kv, ln_cross.reshape(1, -1), w_q, w_o, ln_ffn.reshape(1, -1), wr2)


def _expert_ffn_kernel(x_ref, g_ref, wg_ref, wu_ref, wd_ref, o_ref, hid_ref, *, tf):
    x = x_ref[0]
    for c0 in range(0, D_EXPERT, tf):
        a = _dot(x, wg_ref[0, :, c0:c0 + tf])
        u = _dot(x, wu_ref[0, :, c0:c0 + tf])
        hid_ref[:, c0:c0 + tf] = (a * jax.nn.sigmoid(a) * u).astype(BF16)
    o_ref[0] = (_dot(hid_ref[...], wd_ref[0]) * g_ref[0]).astype(o_ref.dtype)


def expert_ffn(xe, gate, w_gate, w_up, w_down, tm=1024, tf=512):
    e, cap, _ = xe.shape
    tm = min(tm, cap)
    return pl.pallas_call(
        functools.partial(_expert_ffn_kernel, tf=tf),
        grid=(e, cap // tm),
        in_specs=[pl.BlockSpec((1, tm, D_MODEL), lambda e, m: (e, m, 0)),
                  pl.BlockSpec((1, tm, 1), lambda e, m: (e, m, 0)),
                  pl.BlockSpec((1, D_MODEL, D_EXPERT), lambda e, m: (e, 0, 0)),
                  pl.BlockSpec((1, D_MODEL, D_EXPERT), lambda e, m: (e, 0, 0)),
                  pl.BlockSpec((1, D_EXPERT, D_MODEL), lambda e, m: (e, 0, 0))],
        out_specs=pl.BlockSpec((1, tm, D_MODEL), lambda e, m: (e, m, 0)),
        out_shape=jax.ShapeDtypeStruct((e, cap, D_MODEL), BF16),
        scratch_shapes=[pltpu.VMEM((tm, D_EXPERT), BF16)],
        compiler_params=_cparams("parallel", "arbitrary"),
        name="expert_ffn",
    )(xe, gate, w_gate, w_up, w_down)


ROUTE_GROUP = SUBLANES


def _route_thr_kernel(a_ref, thr_ref, *, cap):
    bits = pltpu.bitcast(a_ref[...], jnp.int32)

    def body(i, lo):
        cand = lo | jnp.left_shift(jnp.int32(1), 30 - i)
        cnt = jnp.sum(jnp.where(bits >= cand, 1.0, 0.0), axis=1, keepdims=True)
        return jnp.where(cnt >= cap, cand, lo)

    thr_ref[...] = lax.fori_loop(0, 31, body, jnp.zeros(thr_ref.shape, jnp.int32))


def _prefix_rows(m, upper, lower):
    mb = m.astype(BF16)
    incl = _dot(mb, upper)
    tot = jnp.broadcast_to(incl[:, LANES - 1:LANES], incl.shape).astype(BF16)
    return incl - m + _dot(lower, tot)


def _route_mask_kernel(thr_ref, a_ref, upper_ref, lower_ref, sel_ref, pos_ref, *, cap):
    e = pl.program_id(0)
    thr = thr_ref[e]
    bits = pltpu.bitcast(a_ref[0], jnp.int32)
    gt = jnp.where(bits > thr, 1.0, 0.0)
    eq = jnp.where(bits == thr, 1.0, 0.0)
    need = cap - jnp.sum(jnp.sum(gt, axis=1, keepdims=True), axis=0, keepdims=True)
    eq_rank = _prefix_rows(eq, upper_ref[...], lower_ref[...])
    sel = gt + jnp.where(eq_rank < need, eq, 0.0)
    sel_ref[0] = sel
    pos_ref[0] = _prefix_rows(sel, upper_ref[...], lower_ref[...]).astype(jnp.int32)


def _route_compact_kernel(glo_ref, ghi_ref, lhs_ref, pos_ref, idx_ref, gate_ref):
    e = pl.program_id(0)
    nc = idx_ref.shape[1]
    gw = ROUTE_GROUP * LANES
    slot0 = lax.broadcasted_iota(jnp.int32, (LANES, LANES), 0)

    ngroups = pos_ref.shape[1] // ROUTE_GROUP
    never = jnp.int32(1 << 30)

    def terms(g, slot):
        r0 = pl.multiple_of(g * ROUTE_GROUP, ROUTE_GROUP)
        pos = pos_ref[0, pl.ds(r0, ROUTE_GROUP), :]
        hit = jnp.concatenate(
            [jnp.where(pos[j:j + 1] == slot, 1.0, 0.0) for j in range(ROUTE_GROUP)],
            axis=1).astype(BF16)
        return lhs_ref[0, :, pl.ds(pl.multiple_of(g * gw, gw), gw)], hit

    def token_id(acc):
        return (acc[0:1] * 256.0 + acc[1:2]).astype(jnp.int32)

    def affinity(acc):
        return (acc[2:3] + acc[3:4]) + acc[4:5]

    def first_two(c, carry):
        slot = slot0 + c * LANES
        g_first = glo_ref[e * nc + c]
        g_second = jnp.minimum(g_first + 1, ngroups - 1)
        lhs_a, hit_a = terms(g_first, slot)
        lhs_b, hit_b = terms(g_second, slot + jnp.where(g_first + 1 <= ghi_ref[e * nc + c], 0, never))
        acc = _dot_nt(jnp.concatenate([lhs_a, lhs_b], axis=1), jnp.concatenate([hit_a, hit_b], axis=1))
        idx_ref[0, pl.ds(c, 1), :] = token_id(acc)
        gate_ref[0, pl.ds(c, 1), :] = affinity(acc)
        return carry

    lax.fori_loop(0, nc, first_two, 0, unroll=4)

    def further(c, carry):
        g_first = glo_ref[e * nc + c]
        g_last = ghi_ref[e * nc + c]

        @pl.when(g_last >= g_first + 2)
        def _():
            slot = slot0 + c * LANES

            def group(g, acc):
                lhs, hit = terms(g, slot)
                return acc + _dot_nt(lhs, hit)

            acc = lax.fori_loop(g_first + 2, g_last + 1, group, jnp.zeros((2 * SUBLANES, LANES), F32))
            idx_ref[0, pl.ds(c, 1), :] += token_id(acc)
            gate_ref[0, pl.ds(c, 1), :] += affinity(acc)

        return carry

    lax.fori_loop(0, nc, further, 0)


def route_tokens(aff, cap):
    t, ne = aff.shape
    rows = t // LANES
    nc = cap // LANES
    aff_t = aff.T
    thr = pl.pallas_call(
        functools.partial(_route_thr_kernel, cap=cap),
        out_shape=jax.ShapeDtypeStruct((ne, 1), jnp.int32),
        compiler_params=pltpu.CompilerParams(vmem_limit_bytes=V7X_VMEM_LIMIT_BYTES),
        name="route_threshold",
    )(aff_t)
    aff3 = aff_t.reshape(ne, rows, LANES)
    ii = jnp.arange(LANES)
    upper = (ii[:, None] <= ii[None, :]).astype(BF16)
    rr = jnp.arange(rows)
    lower = (rr[None, :] < rr[:, None]).astype(BF16)
    blk = pl.BlockSpec((1, rows, LANES), lambda e: (e, 0, 0))
    sel, pos = pl.pallas_call(
        functools.partial(_route_mask_kernel, cap=cap),
        grid=(ne,),
        in_specs=[pl.BlockSpec(memory_space=pltpu.SMEM), blk,
                  pl.BlockSpec((LANES, LANES), lambda e: (0, 0)),
                  pl.BlockSpec((rows, rows), lambda e: (0, 0))],
        out_specs=[blk, blk],
        out_shape=[jax.ShapeDtypeStruct((ne, rows, LANES), F32), jax.ShapeDtypeStruct((ne, rows, LANES), jnp.int32)],
        compiler_params=_cparams("parallel"),
        name="route_mask",
    )(thr.reshape(ne), aff3, upper, lower)
    gsz = ROUTE_GROUP
    first = pos[:, ::gsz, 0]
    starts = jnp.arange(nc, dtype=jnp.int32) * LANES
    glo = jnp.sum(first[:, None, :] <= starts[None, :, None], axis=-1, dtype=jnp.int32) - 1
    ghi = jnp.sum(first[:, None, :] <= (starts + (LANES - 1))[None, :, None], axis=-1, dtype=jnp.int32) - 1
    posm = jnp.where(sel > 0, pos, -1)
    tok = jnp.broadcast_to(jnp.arange(t, dtype=jnp.int32), (ne, t))
    g0 = aff_t.astype(BF16)
    r1 = aff_t - g0.astype(F32)
    g1 = r1.astype(BF16)
    g2 = (r1 - g1.astype(F32)).astype(BF16)
    zero = jnp.zeros((ne, 2 * SUBLANES - 5, t), BF16)
    lhs = jnp.concatenate([jnp.stack([(tok >> 8).astype(BF16), (tok & 255).astype(BF16), g0, g1, g2], axis=1),
                           zero], axis=1)
    out_blk = pl.BlockSpec((1, nc, LANES), lambda e, *_: (e, 0, 0))
    grid_spec = pltpu.PrefetchScalarGridSpec(
        num_scalar_prefetch=2,
        grid=(ne,),
        in_specs=[pl.BlockSpec((1, 2 * SUBLANES, t), lambda e, *_: (e, 0, 0)),
                  pl.BlockSpec((1, rows, LANES), lambda e, *_: (e, 0, 0))],
        out_specs=[out_blk, out_blk],
    )
    idx, gate = pl.pallas_call(
        _route_compact_kernel,
        grid_spec=grid_spec,
        out_shape=[jax.ShapeDtypeStruct((ne, nc, LANES), jnp.int32), jax.ShapeDtypeStruct((ne, nc, LANES), F32)],
        compiler_params=_cparams("parallel"),
        name="route_compact",
    )(glo.reshape(-1), ghi.reshape(-1), lhs, posm)
    return idx.reshape(ne, cap), gate.reshape(ne, cap), pos.reshape(ne, t), posm.reshape(ne, t)


COMBINE_SUB = 512
COMBINE_ROWS = 96
COMBINE_XROWS = 128
COMBINE_ALIGN = 16


def _combine_kernel(wst_ref, nex_ref, x_ref, pos_ref, rel_ref, spread_ref, rowin_ref, g_ref, ye_hbm, o_ref,
                    ybuf, sem, xbuf, xsem, *, nsub, cap, final_norm):
    b = pl.program_id(0)
    nb = pl.num_programs(0)
    ne = N_EXPERTS
    w = COMBINE_ROWS
    xw = COMBINE_XROWS
    sub = COMBINE_SUB
    slot = b % 2
    lane = lax.broadcasted_iota(jnp.int32, (1, LANES), 1)
    row_in_window = rowin_ref[...]

    def window(bb, sl, u, e):
        st = pl.multiple_of(wst_ref[(bb * nsub + u) * ne + e], COMBINE_ALIGN)
        return pltpu.make_async_copy(ye_hbm.at[e, pl.ds(st, w), :], ybuf.at[sl, u, pl.ds(e * w, w), :],
                                     sem.at[sl, u, e])

    def start_all(bb, sl):
        for u in range(nsub):
            for e in range(ne):
                window(bb, sl, u, e).start()

    @pl.when(b == 0)
    def _():
        start_all(b, slot)

    @pl.when(b + 1 < nb)
    def _():
        start_all(b + 1, 1 - slot)

    for u in range(nsub):
        rows = slice(u * sub, (u + 1) * sub)
        base = (b * nsub + u) * ne
        spread = _dot(rel_ref[rows, :], spread_ref[...])
        onehot = jnp.where(spread == row_in_window, 1.0, 0.0).astype(BF16)
        for e in range(ne):
            window(b, slot, u, e).wait()
        o_ref[rows, :] = x_ref[rows, :] + _dot(onehot, ybuf[slot, u])

        for e in range(ne):
            def extra(k, carry, e=e, rows=rows, base=base):
                first = wst_ref[base + e] + w + (k - 1) * xw
                st = pl.multiple_of(jnp.minimum(first, cap - xw), COMBINE_ALIGN)
                cp = pltpu.make_async_copy(ye_hbm.at[e, pl.ds(st, xw), :], xbuf, xsem.at[0])
                cp.start()
                cp.wait()
                col = jnp.broadcast_to(pos_ref[rows, e:e + 1], (sub, xw))
                hit = jnp.logical_and(col - st == lane, col >= first)
                o_ref[rows, :] += _dot(jnp.where(hit, 1.0, 0.0).astype(BF16), xbuf[...])
                return carry

            lax.fori_loop(1, nex_ref[base + e] + 1, extra, 0)

        if final_norm:
            o_ref[rows, :] = _rms(o_ref[rows, :], g_ref[...])


def moe_combine(x2d, ye, pos, posm, norm_gain=None, nsub=1):
    t = x2d.shape[0]
    ne, cap = ye.shape[0], ye.shape[1]
    sub = COMBINE_SUB
    nsub = min(nsub, t // sub)
    tb = sub * nsub
    nb = t // tb
    w = COMBINE_ROWS
    lo = pos[:, ::sub]
    hi = jnp.concatenate([lo[:, 1:], jnp.full((ne, 1), cap, jnp.int32)], axis=1)
    wst = jnp.minimum((lo // COMBINE_ALIGN) * COMBINE_ALIGN, cap - COMBINE_XROWS)
    nex = jnp.maximum(hi - (wst + w) + (COMBINE_XROWS - 1), 0) // COMBINE_XROWS
    posm = posm.T
    rel = posm - jnp.repeat(wst.T, sub, axis=0)
    rel = jnp.where(posm >= 0, jnp.minimum(rel, 2 * LANES - 1), -1).astype(BF16)
    stacked = jnp.arange(ne * w, dtype=jnp.int32)
    spread = stacked[None, :] // w == jnp.arange(ne, dtype=jnp.int32)[:, None]
    rowin = (stacked % w).astype(F32).reshape(1, ne * w)
    gain = jnp.ones((1, D_MODEL), F32) if norm_gain is None else norm_gain.astype(F32).reshape(1, D_MODEL)
    grid_spec = pltpu.PrefetchScalarGridSpec(
        num_scalar_prefetch=2,
        grid=(nb,),
        in_specs=[pl.BlockSpec((tb, D_MODEL), lambda b, *_: (b, 0)),
                  pl.BlockSpec((tb, ne), lambda b, *_: (b, 0)),
                  pl.BlockSpec((tb, ne), lambda b, *_: (b, 0)),
                  pl.BlockSpec((ne, ne * w), lambda b, *_: (0, 0)),
                  pl.BlockSpec((1, ne * w), lambda b, *_: (0, 0)),
                  pl.BlockSpec((1, D_MODEL), lambda b, *_: (0, 0)),
                  pl.BlockSpec(memory_space=pl.ANY)],
        out_specs=pl.BlockSpec((tb, D_MODEL), lambda b, *_: (b, 0)),
        scratch_shapes=[pltpu.VMEM((2, nsub, ne * w, D_MODEL), BF16),
                        pltpu.SemaphoreType.DMA((2, nsub, ne)),
                        pltpu.VMEM((COMBINE_XROWS, D_MODEL), BF16),
                        pltpu.SemaphoreType.DMA((1,))],
    )
    return pl.pallas_call(
        functools.partial(_combine_kernel, nsub=nsub, cap=cap, final_norm=norm_gain is not None),
        grid_spec=grid_spec,
        out_shape=jax.ShapeDtypeStruct(x2d.shape, F32),
        compiler_params=_cparams("arbitrary"),
        name="moe_combine",
    )(wst.T.reshape(-1), nex.T.reshape(-1), x2d, posm, rel, spread.astype(BF16), rowin, gain, ye)


def mixer_ab(x2d, e, p, shared, bsz, seq):
    proj = norm_proj(x2d, p['ln_mix_l'], p['w_in_ab'][e], bsz, seq, [(0, AB_IN, BF16, False)])[0]
    vx, x0 = hyena_pre(proj, p['hy_conv_w'][e], p['hy_conv_b'][e], bsz, seq)
    kp, kq, kp2 = shared['hy_spec'][e]
    y_hy = hyena_conv(vx, x0, shared['dft_fwd'], shared['dft_inv'], kp, kq, kp2, p['hy_d'][e], bsz, seq)
    y_wa = window_attn(proj, p['attn_sink'][e], shared['wa_bias'], bsz, seq)
    return out_proj_ab(x2d, y_hy, y_wa, p['w_out_ab'][e])


def mixer_cd(x2d, o, p, shared, bsz, seq):
    u2d, rest = norm_proj(x2d, p['ln_mix_l'], shared['w_in_cd'][o], bsz, seq,
                          [(0, HALF, F32, False), (HALF, CD_PAD - HALF, BF16, False)])
    a5, bm, cm = shared['s5'][o]
    y_dirs = s5_scan(u2d.reshape(bsz, seq, HALF), a5, bm, cm, bsz, seq)
    wq1, wq2, wk, wv = shared['mla_w'][o]
    q, k, v = mla_prep(rest, p['mla_q_norm'][o], p['mla_kv_norm'][o], wq1, wq2, wk, wv,
                       shared['rope_cos'], shared['rope_sin'], bsz, seq)
    y_mla = mla_attn(q, k, v, bsz, seq)
    return out_proj_cd(x2d, u2d, y_dirs.reshape(2, bsz * seq, HALF), y_mla, p['s5_d'][o],
                       p['s5_glu_w'][o], p['s5_glu_b'][o], p['w_out_cd'][o], bsz, seq)


def ec_moe(x2d, hn, aff, w_gate, w_up, w_down, norm_gain=None):
    t = x2d.shape[0]
    cap = EC_CAPACITY_FACTOR * t // N_EXPERTS
    idx, gate, pos, posm = route_tokens(aff, cap)
    xe = hn[idx]
    ye = expert_ffn(xe, gate[..., None], w_gate, w_up, w_down)
    return moe_combine(x2d, ye, pos, posm, norm_gain)


def prepare_shared(p, seq):
    sh = {}
    sh['wa_bias'] = window_bias_mask(p['rel_bias'])
    sh['dft_fwd'], sh['dft_inv'] = dft_matrices(seq)
    sh['rope_cos'], sh['rope_sin'] = rope_tables(seq)
    specs = []
    for e in range(p['w_in_ab'].shape[0]):
        h = hyena_filters(seq, p['hy_filt_w1'][e], p['hy_filt_b1'][e], p['hy_filt_w2'][e], p['hy_filt_b2'][e],
                          p['hy_filt_w3'][e], p['hy_filt_freq'][e])
        h_fwd, h_bwd = h[:, :HY_WIDTH], h[:, HY_WIDTH:]
        k = jnp.concatenate([h_fwd, jnp.zeros_like(h_fwd[:1]), h_bwd[:0:-1]], axis=0)
        kf = kernel_spectrum(sh['dft_fwd'], k)
        k_r, k_s = kf[:seq], kf[seq:]
        specs.append((k_r, k_s.at[0].set(0.0), k_r.at[0].set(k_s[0])))
    sh['hy_spec'] = specs
    s5, mla_w, w_in_cd = [], [], []
    for o in range(p['w_in_cd'].shape[0]):
        s5.append(s5_discretise(p['s5_a_re'][o], p['s5_a_im'][o], p['s5_log_dt'][o], p['s5_b_re'][o],
                                p['s5_b_im'][o], p['s5_c_re'][o], p['s5_c_im'][o]))
        mla_w.append(mla_weights(p['mla_w_uq'][o], p['mla_w_ukv'][o]))
        w = p['w_in_cd'][o].astype(F32)
        o2 = HALF + MLA_Q_RANK + MLA_KV_RANK
        kr = w[:, o2:o2 + MLA_ROPE]
        half = MLA_ROPE // 2
        kr_rot = jnp.concatenate([-kr[:, half:], kr[:, :half]], axis=1)
        z64 = jnp.zeros((D_MODEL, MLA_NOPE), F32)
        z32 = jnp.zeros((D_MODEL, MLA_HP - MLA_NOPE - MLA_ROPE), F32)
        w_in_cd.append(jnp.concatenate([w[:, :o2], z64, kr, z32, z64, kr_rot, z32], axis=1).astype(BF16))
    sh['s5'], sh['mla_w'], sh['w_in_cd'] = s5, mla_w, w_in_cd
    return sh


def run_trunk(x, mem, p, shared):
    bsz, seq, _ = x.shape
    x2d = x.reshape(bsz * seq, D_MODEL)
    mem2d = mem.reshape(bsz * mem.shape[1], D_MODEL)
    for layer in range(DEPTH):
        pl_ = dict(p, ln_mix_l=p['ln_mix'][layer])
        if layer % 2 == 0:
            x2d = mixer_ab(x2d, layer // 2, pl_, shared, bsz, seq)
        else:
            x2d = mixer_cd(x2d, layer // 2, pl_, shared, bsz, seq)
        kv = norm_proj(mem2d, p['ln_mem'][layer], p['ca_w_kv'][layer], bsz, mem.shape[1],
                       [(0, 2 * CA_HEADS * CA_HEAD_DIM, BF16, False)])[0]
        x2d, hn, aff = cross_router(x2d, kv, p['ln_cross'][layer], p['ca_w_q'][layer], p['ca_w_o'][layer],
                                    p['ln_ffn'][layer], p['moe_w_router'][layer], bsz, seq)
        x2d = ec_moe(x2d, hn, aff, p['moe_w_gate'][layer], p['moe_w_up'][layer], p['moe_w_down'][layer],
                     norm_gain=p['ln_final'] if layer == DEPTH - 1 else None)
    return x2d.reshape(bsz, seq, D_MODEL)


_BF16_WEIGHTS = ('w_in_ab', 'w_out_ab', 'w_out_cd', 's5_glu_w', 'ca_w_q', 'ca_w_kv', 'ca_w_o',
                 'moe_w_gate', 'moe_w_up', 'moe_w_down')


def kernel(x_prompt, x_sample, mem_prompt, mem_sample, ln_mix, ln_cross, ln_mem, ln_ffn, ln_final, rel_bias, w_in_ab, w_out_ab, hy_conv_w, hy_conv_b, hy_filt_w1, hy_filt_b1, hy_filt_w2, hy_filt_b2, hy_filt_w3, hy_filt_freq, hy_d, attn_sink, w_in_cd, w_out_cd, s5_a_re, s5_a_im, s5_log_dt, s5_b_re, s5_b_im, s5_c_re, s5_c_im, s5_d, s5_glu_w, s5_glu_b, mla_q_norm, mla_w_uq, mla_kv_norm, mla_w_ukv, ca_w_q, ca_w_kv, ca_w_o, moe_w_router, moe_w_gate, moe_w_up, moe_w_down):
    p = dict(ln_mix=ln_mix, ln_cross=ln_cross, ln_mem=ln_mem, ln_ffn=ln_ffn, ln_final=ln_final,
             rel_bias=rel_bias, w_in_ab=w_in_ab, w_out_ab=w_out_ab, hy_conv_w=hy_conv_w,
             hy_conv_b=hy_conv_b, hy_filt_w1=hy_filt_w1, hy_filt_b1=hy_filt_b1,
             hy_filt_w2=hy_filt_w2, hy_filt_b2=hy_filt_b2, hy_filt_w3=hy_filt_w3,
             hy_filt_freq=hy_filt_freq, hy_d=hy_d, attn_sink=attn_sink, w_in_cd=w_in_cd,
             w_out_cd=w_out_cd, s5_a_re=s5_a_re, s5_a_im=s5_a_im, s5_log_dt=s5_log_dt,
             s5_b_re=s5_b_re, s5_b_im=s5_b_im, s5_c_re=s5_c_re, s5_c_im=s5_c_im, s5_d=s5_d,
             s5_glu_w=s5_glu_w, s5_glu_b=s5_glu_b, mla_q_norm=mla_q_norm, mla_w_uq=mla_w_uq,
             mla_kv_norm=mla_kv_norm, mla_w_ukv=mla_w_ukv, ca_w_q=ca_w_q, ca_w_kv=ca_w_kv,
             ca_w_o=ca_w_o, moe_w_router=moe_w_router, moe_w_gate=moe_w_gate,
             moe_w_up=moe_w_up, moe_w_down=moe_w_down)
    assert x_prompt.shape[1] == x_sample.shape[1]
    shared = prepare_shared(p, x_prompt.shape[1])
    for name in _BF16_WEIGHTS:
        p[name] = p[name].astype(BF16)
    y_prompt = run_trunk(x_prompt, mem_prompt, p, shared)
    y_sample = run_trunk(x_sample, mem_sample, p, shared)
    return (y_prompt, y_sample)
```

```python
import functools
import math

import jax
import jax.numpy as jnp
import numpy as np
from jax import lax
from jax.experimental import pallas as pl
from jax.experimental.pallas import tpu as pltpu

D_MODEL = 1024
DEPTH = 4
HALF = 512
HEAD_DIM = 64
EPS = 1e-6
NEG = -1e30

HY_WIDTH = HALF
HY_EMB = 33
HY_BANDS = (HY_EMB - 1) // 2
HY_FILT_HIDDEN = 64
HY_DECAY_TARGET = 1e-2
HY_FAST = 0.3
HY_SLOW = 1.5
HY_MIN_DECAY = math.log(HY_DECAY_TARGET) / HY_SLOW
HY_MAX_DECAY = math.log(HY_DECAY_TARGET) / HY_FAST
HY_SHIFT = 0.05

WA_HEADS = 8
WA_KV_HEADS = 2
WA_REP = 4
WA_WINDOW = 128
WA_BLOCK = 128
REL_BUCKETS = 32
REL_MAX_DIST = 128

S5_GROUP = 16
S5_GROUPS = 32
S5_STATE = 64

MLA_HEADS = 8
MLA_NOPE = 64
MLA_ROPE = 32
MLA_V = 64
MLA_Q_RANK = 256
MLA_KV_RANK = 128
ROPE_THETA = 10000.0

CA_HEADS = 4
CA_HEAD_DIM = 128

N_EXPERTS = 16
EC_CAPACITY_FACTOR = 2
D_EXPERT = 2048

AB_IN = 3 * HY_WIDTH + (WA_HEADS + 2 * WA_KV_HEADS) * HEAD_DIM
CD_PAD = 1152

V7X_VMEM_LIMIT_BYTES = 56 * 1024 * 1024
LANES = 128
SUBLANES = 8
LOG2E = math.log2(math.e)

BF16 = jnp.bfloat16
F32 = jnp.float32


def _cparams(*sem):
    return pltpu.CompilerParams(dimension_semantics=sem, vmem_limit_bytes=V7X_VMEM_LIMIT_BYTES)


def _dot(a, b):
    return jnp.dot(a, b, preferred_element_type=F32)


def _dot_nt(a, b):
    return lax.dot_general(a, b, (((1,), (1,)), ((), ())), preferred_element_type=F32)


def _rms(xf, g):
    return xf * lax.rsqrt(jnp.mean(xf * xf, axis=-1, keepdims=True) + EPS) * g


def _norm_proj_kernel(x_ref, g_ref, w_ref, *out_refs, splits):
    hn = _rms(x_ref[...].astype(F32), g_ref[...]).astype(BF16)
    for o_ref, (start, width) in zip(out_refs, splits):
        for c0 in range(0, width, 512):
            cw = min(512, width - c0)
            o_ref[:, c0:c0 + cw] = _dot(hn, w_ref[:, start + c0:start + c0 + cw]).astype(o_ref.dtype)


def norm_proj(x2d, gain, w, bsz, seq, outs, tm=512):
    tm = min(tm, seq)
    nl = seq // tm
    n = w.shape[1]
    out_shapes, out_specs, splits = [], [], []
    for start, width, dtype, time_major in outs:
        splits.append((start, width))
        if time_major:
            out_shapes.append(jax.ShapeDtypeStruct((seq, bsz * width), dtype))
            out_specs.append(pl.BlockSpec((tm, width), lambda b, i: (i, b)))
        else:
            out_shapes.append(jax.ShapeDtypeStruct((bsz * seq, width), dtype))
            out_specs.append(pl.BlockSpec((tm, width), lambda b, i, nl=nl: (b * nl + i, 0)))
    return pl.pallas_call(
        functools.partial(_norm_proj_kernel, splits=tuple(splits)),
        grid=(bsz, nl),
        in_specs=[pl.BlockSpec((tm, D_MODEL), lambda b, i, nl=nl: (b * nl + i, 0)),
                  pl.BlockSpec((1, D_MODEL), lambda b, i: (0, 0)),
                  pl.BlockSpec((D_MODEL, n), lambda b, i: (0, 0))],
        out_specs=out_specs,
        out_shape=out_shapes,
        compiler_params=_cparams("parallel", "parallel"),
        name="norm_proj",
    )(x2d, gain.reshape(1, D_MODEL), w)


def _hyena_filter_kernel(z_ref, w1_ref, b1_ref, w2_ref, b2_ref, w3_ref, fr_ref, win_ref, o_ref):
    hp = lax.Precision.HIGHEST
    fr = fr_ref[...]
    h = jnp.sin(fr * (jnp.dot(z_ref[...], w1_ref[...], precision=hp, preferred_element_type=F32) + b1_ref[...]))
    h = jnp.sin(fr * (jnp.dot(h, w2_ref[...], precision=hp, preferred_element_type=F32) + b2_ref[...]))
    h = jnp.dot(h, w3_ref[...], precision=hp, preferred_element_type=F32)
    o_ref[...] = h * win_ref[...]


def hyena_filters(seq, w1, b1, w2, b2, w3, freq):
    t = jnp.linspace(0.0, 1.0, seq, dtype=F32)[:, None]
    ang = 2.0 * math.pi * jnp.arange(seq, dtype=F32)[:, None] / seq
    bands = jnp.linspace(1e-4, HY_BANDS - 1, HY_BANDS, dtype=F32)[None, :]
    z = jnp.concatenate([t, jnp.cos(bands * ang), -jnp.sin(bands * ang)], axis=-1)
    zp = jnp.pad(z, ((0, 0), (0, HY_FILT_HIDDEN - HY_EMB)))
    w1p = jnp.pad(w1.astype(F32), ((0, HY_FILT_HIDDEN - HY_EMB), (0, 0)))
    deltas = jnp.abs(jnp.linspace(HY_MIN_DECAY, HY_MAX_DECAY, HY_WIDTH, dtype=F32))
    window = jnp.exp(-t * deltas[None, :]) + HY_SHIFT
    win2 = jnp.concatenate([window, window], axis=-1)
    tl = min(512, seq)
    hh = HY_FILT_HIDDEN
    full = lambda r, c: pl.BlockSpec((r, c), lambda i: (0, 0))
    return pl.pallas_call(
        _hyena_filter_kernel,
        grid=(seq // tl,),
        in_specs=[pl.BlockSpec((tl, hh), lambda i: (i, 0)), full(hh, hh), full(1, hh), full(hh, hh), full(1, hh),
                  full(hh, 2 * HY_WIDTH), full(1, hh), pl.BlockSpec((tl, 2 * HY_WIDTH), lambda i: (i, 0))],
        out_specs=pl.BlockSpec((tl, 2 * HY_WIDTH), lambda i: (i, 0)),
        out_shape=jax.ShapeDtypeStruct((seq, 2 * HY_WIDTH), F32),
        compiler_params=_cparams("parallel"),
        name="hyena_filter",
    )(zp, w1p, b1.reshape(1, hh), w2, b2.reshape(1, hh), w3, freq.reshape(1, hh), win2)


def dft_matrices(seq):
    n = 2 * seq
    r = jnp.arange(seq, dtype=jnp.int32)[:, None]
    t = jnp.arange(n, dtype=jnp.int32)[None, :]
    ang = ((r * t) % n).astype(F32) * (2.0 * math.pi / n)
    c, s = jnp.cos(ang), jnp.sin(ang)
    nyq = jnp.where(t % 2 == 0, 1.0, -1.0).astype(F32)
    fwd = jnp.concatenate([c, jnp.where(r == 0, nyq, -s)], axis=0)
    ct = c[:, :seq].T
    st = s[:, :seq].T
    r_row = r.T
    inv_r = jnp.where(r_row == 0, 1.0, 2.0 * ct) / n
    inv_s = jnp.where(r_row == 0, nyq[:, :seq].T, -2.0 * st) / n
    inv = jnp.concatenate([inv_r, inv_s], axis=1)
    return fwd.astype(BF16), inv.astype(BF16)


def _kernel_dft_kernel(a_ref, khi_ref, klo_ref, o_ref):
    o_ref[...] = _dot(a_ref[...], khi_ref[...]) + _dot(a_ref[...], klo_ref[...])


def kernel_spectrum(fwd, k):
    n = fwd.shape[0]
    khi = k.astype(BF16)
    klo = (k - khi.astype(F32)).astype(BF16)
    tf = min(256, n)
    return pl.pallas_call(
        _kernel_dft_kernel,
        grid=(n // tf,),
        in_specs=[pl.BlockSpec((tf, n), lambda i: (i, 0)),
                  pl.BlockSpec((n, HY_WIDTH), lambda i: (0, 0)),
                  pl.BlockSpec((n, HY_WIDTH), lambda i: (0, 0))],
        out_specs=pl.BlockSpec((tf, HY_WIDTH), lambda i: (i, 0)),
        out_shape=jax.ShapeDtypeStruct((n, HY_WIDTH), F32),
        compiler_params=_cparams("parallel"),
        name="hyena_kernel_dft",
    )(fwd, khi, klo)


def _shift_down(u):
    rows = lax.broadcasted_iota(jnp.int32, u.shape, 0)
    return jnp.where(rows == 0, 0.0, pltpu.roll(u, 1, 0))


def _shift_up(u):
    n = u.shape[0]
    rows = lax.broadcasted_iota(jnp.int32, u.shape, 0)
    return jnp.where(rows == n - 1, 0.0, pltpu.roll(u, n - 1, 0))


def _hyena_pre_kernel(u_ref, w_ref, b_ref, vx_ref, x0_ref):
    def conv(c0):
        u = u_ref[:, c0:c0 + LANES].astype(F32)
        w = w_ref[:, c0:c0 + LANES]
        return _shift_down(u) * w[0:1] + u * w[1:2] + _shift_up(u) * w[2:3] + b_ref[:, c0:c0 + LANES]

    for c in range(0, HY_WIDTH, LANES):
        x0_ref[:, c:c + LANES] = conv(c).astype(x0_ref.dtype)
        vx_ref[:, c:c + LANES] = (conv(2 * HY_WIDTH + c) * conv(HY_WIDTH + c)).astype(vx_ref.dtype)


def hyena_pre(proj, conv_w, conv_b, bsz, seq):
    w3 = 3 * HY_WIDTH
    out = jax.ShapeDtypeStruct((bsz * seq, HY_WIDTH), BF16)
    return pl.pallas_call(
        _hyena_pre_kernel,
        grid=(bsz,),
        in_specs=[pl.BlockSpec((seq, w3), lambda b: (b, 0)),
                  pl.BlockSpec((3, w3), lambda b: (0, 0)),
                  pl.BlockSpec((1, w3), lambda b: (0, 0))],
        out_specs=[pl.BlockSpec((seq, HY_WIDTH), lambda b: (b, 0))] * 2,
        out_shape=[out, out],
        compiler_params=_cparams("parallel"),
        name="hyena_pre",
    )(proj, conv_w, conv_b.reshape(1, w3))


def _hyena_conv_kernel(vx_ref, x0_ref, ar_ref, as_ref, inv_ref, kp_ref, kq_ref, kp2_ref, d_ref,
                       o_ref, z_ref, *, nf, tf):
    step = pl.program_id(1)
    half = nf * tf

    @pl.when(step < nf)
    def _():
        vx = vx_ref[...]
        r = _dot(ar_ref[...], vx)
        s = _dot(as_ref[...], vx)
        kq = kq_ref[...]
        f0 = pl.multiple_of(step * tf, tf)
        z_ref[pl.ds(f0, tf), :] = (r * kp_ref[...] - s * kq).astype(BF16)
        z_ref[pl.ds(half + f0, tf), :] = (r * kq + s * kp2_ref[...]).astype(BF16)

    @pl.when(step >= nf)
    def _():
        t0 = pl.multiple_of((step - nf) * tf, tf)
        y = _dot(inv_ref[...], z_ref[...]) + vx_ref[pl.ds(t0, tf), :].astype(F32) * d_ref[...]
        o_ref[pl.ds(t0, tf), :] = (y * x0_ref[pl.ds(t0, tf), :].astype(F32)).astype(o_ref.dtype)


def hyena_conv(vx, x0, fwd, inv, kp, kq, kp2, d, bsz, seq):
    tf = min(512, seq)
    nf = seq // tf
    w = HY_WIDTH
    fwd_tile = lambda b, s: (jnp.minimum(s, nf - 1), 0)
    return pl.pallas_call(
        functools.partial(_hyena_conv_kernel, nf=nf, tf=tf),
        grid=(bsz, 2 * nf),
        in_specs=[pl.BlockSpec((seq, w), lambda b, s: (b, 0)),
                  pl.BlockSpec((seq, w), lambda b, s: (b, 0)),
                  pl.BlockSpec((tf, seq), fwd_tile),
                  pl.BlockSpec((tf, seq), lambda b, s: (nf + jnp.minimum(s, nf - 1), 0)),
                  pl.BlockSpec((tf, 2 * seq), lambda b, s: (jnp.maximum(s - nf, 0), 0)),
                  pl.BlockSpec((tf, w), fwd_tile),
                  pl.BlockSpec((tf, w), fwd_tile),
                  pl.BlockSpec((tf, w), fwd_tile),
                  pl.BlockSpec((1, w), lambda b, s: (0, 0))],
        out_specs=pl.BlockSpec((seq, w), lambda b, s: (b, 0)),
        out_shape=jax.ShapeDtypeStruct((bsz * seq, w), BF16),
        scratch_shapes=[pltpu.VMEM((2 * seq, w), BF16)],
        compiler_params=_cparams("parallel", "arbitrary"),
        name="hyena_conv",
    )(vx, x0, fwd, fwd, inv, kp, kq, kp2, d.reshape(1, w))


def _rel_bucket(rel):
    nb = REL_BUCKETS // 2
    max_exact = nb // 2
    ret = (rel > 0).astype(jnp.int32) * nb
    n = jnp.abs(rel)
    nf = jnp.maximum(n, 1).astype(F32)
    large = max_exact + (jnp.log(nf / max_exact) / math.log(REL_MAX_DIST / max_exact)
                         * (nb - max_exact)).astype(jnp.int32)
    large = jnp.minimum(large, nb - 1)
    return ret + jnp.where(n < max_exact, n, large)


def window_bias_mask(rel_bias):
    j = jnp.arange(WA_BLOCK, dtype=jnp.int32)[:, None]
    s = jnp.arange(3 * WA_BLOCK, dtype=jnp.int32)[None, :]
    rel = (s - WA_BLOCK) - j
    bias = jnp.transpose(rel_bias.astype(F32)[_rel_bucket(rel)], (2, 0, 1))
    band = jnp.abs(rel) <= WA_WINDOW
    return jnp.where(band[None], bias, NEG)


def _window_attn_kernel(sink_ref, q_ref, k_ref, v_ref, bias_ref, o_ref):
    nb = q_ref.shape[0] // WA_BLOCK
    scale = HEAD_DIM ** -0.5 * LOG2E
    rq = WA_REP * WA_BLOCK
    col = lax.broadcasted_iota(jnp.int32, (rq, 3 * WA_BLOCK), 1)
    head_of_row = lax.broadcasted_iota(jnp.int32, (rq, 1), 0) // WA_BLOCK

    def block(i, carry):
        ip = jnp.maximum(i - 1, 0)
        inx = jnp.minimum(i + 1, nb - 1)
        rows = lambda j: pl.ds(pl.multiple_of(j * WA_BLOCK, WA_BLOCK), WA_BLOCK)
        qb = q_ref[rows(i), :]
        kslab = jnp.concatenate([k_ref[rows(ip), :], k_ref[rows(i), :], k_ref[rows(inx), :]], axis=0)
        vslab = jnp.concatenate([v_ref[rows(ip), :], v_ref[rows(i), :], v_ref[rows(inx), :]], axis=0)
        lo = jnp.where(i > 0, 0, WA_BLOCK)
        hi = jnp.where(i < nb - 1, 3 * WA_BLOCK, 2 * WA_BLOCK)
        valid = jnp.logical_and(col >= lo, col < hi)
        outs = []
        for g in range(WA_KV_HEADS):
            heads = range(g * WA_REP, (g + 1) * WA_REP)
            q4 = jnp.concatenate([qb[:, h * HEAD_DIM:(h + 1) * HEAD_DIM] for h in heads], axis=0)
            kg = kslab[:, g * HEAD_DIM:(g + 1) * HEAD_DIM]
            vg = vslab[:, g * HEAD_DIM:(g + 1) * HEAD_DIM]
            s = _dot_nt(q4, kg) * scale + bias_ref[g]
            s = jnp.where(valid, s, NEG)
            sk = jnp.zeros((rq, 1), F32)
            for r, h in enumerate(heads):
                sk = jnp.where(head_of_row == r, sink_ref[h], sk)
            m = jnp.maximum(jnp.max(s, axis=-1, keepdims=True), sk)
            p = jnp.exp2(s - m)
            den = jnp.sum(p, axis=-1, keepdims=True) + jnp.exp2(sk - m)
            o4 = _dot(p.astype(BF16), vg) / den
            outs += [o4[r * WA_BLOCK:(r + 1) * WA_BLOCK] for r in range(WA_REP)]
        o_ref[rows(i), :] = jnp.concatenate(outs, axis=-1).astype(o_ref.dtype)
        return carry

    lax.fori_loop(0, nb, block, 0, unroll=2)


def window_attn(proj, sink, bias_mask, bsz, seq):
    hq = WA_HEADS * HEAD_DIM
    hkv = WA_KV_HEADS * HEAD_DIM
    q_blk = (3 * HY_WIDTH) // hq
    k_blk = (3 * HY_WIDTH + hq) // hkv
    bias2 = (bias_mask * LOG2E).reshape(WA_KV_HEADS, WA_REP * WA_BLOCK, 3 * WA_BLOCK)
    return pl.pallas_call(
        _window_attn_kernel,
        grid=(bsz,),
        in_specs=[pl.BlockSpec(memory_space=pltpu.SMEM),
                  pl.BlockSpec((seq, hq), lambda b: (b, q_blk)),
                  pl.BlockSpec((seq, hkv), lambda b: (b, k_blk)),
                  pl.BlockSpec((seq, hkv), lambda b: (b, k_blk + 1)),
                  pl.BlockSpec((WA_KV_HEADS, WA_REP * WA_BLOCK, 3 * WA_BLOCK), lambda b: (0, 0, 0))],
        out_specs=pl.BlockSpec((seq, hq), lambda b: (b, 0)),
        out_shape=jax.ShapeDtypeStruct((bsz * seq, hq), BF16),
        compiler_params=_cparams("parallel"),
        name="window_attn",
    )(sink.astype(F32) * LOG2E, proj, proj, proj, bias2)


def _out_proj_kernel(x_ref, a_ref, b_ref, w_ref, o_ref):
    acc = _dot(a_ref[...], w_ref[0:HALF, :]) + _dot(b_ref[...], w_ref[HALF:, :])
    o_ref[...] = x_ref[...] + acc


def out_proj_ab(x2d, y_a, y_b, w, tm=512):
    t = x2d.shape[0]
    tm = min(tm, t)
    return pl.pallas_call(
        _out_proj_kernel,
        grid=(t // tm,),
        in_specs=[pl.BlockSpec((tm, D_MODEL), lambda i: (i, 0)),
                  pl.BlockSpec((tm, HALF), lambda i: (i, 0)),
                  pl.BlockSpec((tm, HALF), lambda i: (i, 0)),
                  pl.BlockSpec((D_MODEL, D_MODEL), lambda i: (0, 0))],
        out_specs=pl.BlockSpec((tm, D_MODEL), lambda i: (i, 0)),
        out_shape=jax.ShapeDtypeStruct(x2d.shape, F32),
        compiler_params=_cparams("parallel"),
        name="out_proj_ab",
    )(x2d, y_a, y_b, w)


def s5_discretise(a_re, a_im, log_dt, b_re, b_im, c_re, c_im):
    lam = lax.complex(a_re.astype(F32), a_im.astype(F32))
    dt = jnp.exp(log_dt.astype(F32))[..., None]
    abar = jnp.exp(lam * dt)
    bmat = lax.complex(b_re.astype(F32), b_im.astype(F32))
    bbar = ((abar - 1.0) / lam)[..., None] * bmat
    cmat = lax.complex(c_re.astype(F32), c_im.astype(F32))
    nj, gl = S5_GROUPS // SUBLANES, SUBLANES
    eye = jnp.eye(gl, dtype=F32)
    a5 = jnp.stack([abar.real, abar.imag], axis=1).reshape(2, 2, nj, 1, gl * S5_STATE)
    a5 = jnp.transpose(a5, (0, 2, 1, 3, 4))
    a5 = jnp.broadcast_to(a5, (2, nj, 2, SUBLANES, gl * S5_STATE))

    def pack_b(x):
        x = x.reshape(2, nj, gl, S5_STATE, S5_GROUP)
        y = jnp.einsum('hg,djgpc->djhcgp', eye, x)
        return y.reshape(2, nj, gl * S5_GROUP, gl * S5_STATE)

    def pack_c(x):
        x = x.reshape(2, nj, gl, S5_GROUP, S5_STATE)
        y = jnp.einsum('hg,djgcp->djgphc', eye, x)
        return y.reshape(2, nj, gl * S5_STATE, gl * S5_GROUP)

    bm = jnp.concatenate([pack_b(bbar.real), pack_b(bbar.imag)], axis=-1).astype(BF16)
    cm = jnp.concatenate([pack_c(cmat.real), -pack_c(cmat.imag)], axis=-2).astype(BF16)
    return a5, bm, cm


def _s5_kernel(u_ref, a_ref, b_ref, c_ref, y_ref, buf_ref, h_ref, *, chunk):
    d = pl.program_id(1)
    nj = S5_GROUPS // SUBLANES
    sw = SUBLANES * S5_STATE
    rows = chunk * SUBLANES

    @pl.when(pl.program_id(2) == 0)
    def _():
        h_ref[...] = jnp.zeros_like(h_ref)

    u = pltpu.einshape("btc->(tb)c", u_ref[...]).astype(BF16)
    for j in range(nj):
        buf_ref[j] = _dot(u[:, j * LANES:(j + 1) * LANES], b_ref[0, j])

    for j0 in range(0, nj, 2):
        js = (j0, j0 + 1)

        def step(s, carry, js=js):
            t = jnp.where(d == 0, s, chunk - 1 - s)
            r0 = pl.multiple_of(t * SUBLANES, SUBLANES)
            new = []
            for n, j in enumerate(js):
                hr, hi = carry[2 * n], carry[2 * n + 1]
                ar = a_ref[0, j, 0]
                ai = a_ref[0, j, 1]
                br = buf_ref[j, pl.ds(r0, SUBLANES), 0:sw]
                bi = buf_ref[j, pl.ds(r0, SUBLANES), sw:2 * sw]
                nr = ar * hr - ai * hi + br
                ni = ar * hi + ai * hr + bi
                buf_ref[j, pl.ds(r0, SUBLANES), 0:sw] = nr
                buf_ref[j, pl.ds(r0, SUBLANES), sw:2 * sw] = ni
                new += [nr, ni]
            return tuple(new)

        init = tuple(h_ref[2 * j + k] for j in js for k in range(2))
        fin = lax.fori_loop(0, chunk, step, init, unroll=2)
        for n, j in enumerate(js):
            h_ref[2 * j] = fin[2 * n]
            h_ref[2 * j + 1] = fin[2 * n + 1]

    for j in range(nj):
        yj = _dot(buf_ref[j].astype(BF16), c_ref[0, j])
        y_ref[0, :, :, j * LANES:(j + 1) * LANES] = pltpu.einshape(
            "(tb)c->btc", yj, b=SUBLANES).astype(y_ref.dtype)


def s5_scan(u3, a5, bm, cm, bsz, seq, chunk=64):
    chunk = min(chunk, seq)
    nc = seq // chunk
    nj = S5_GROUPS // SUBLANES
    sw = SUBLANES * S5_STATE

    def tchunk(d, i):
        return i + d * (nc - 1 - 2 * i)

    return pl.pallas_call(
        functools.partial(_s5_kernel, chunk=chunk),
        grid=(bsz // SUBLANES, 2, nc),
        in_specs=[pl.BlockSpec((SUBLANES, chunk, HALF), lambda b, d, i: (b, tchunk(d, i), 0)),
                  pl.BlockSpec((1, nj, 2, SUBLANES, sw), lambda b, d, i: (d, 0, 0, 0, 0)),
                  pl.BlockSpec((1, nj, LANES, 2 * sw), lambda b, d, i: (d, 0, 0, 0)),
                  pl.BlockSpec((1, nj, 2 * sw, LANES), lambda b, d, i: (d, 0, 0, 0))],
        out_specs=pl.BlockSpec((1, SUBLANES, chunk, HALF), lambda b, d, i: (d, b, tchunk(d, i), 0)),
        out_shape=jax.ShapeDtypeStruct((2, bsz, seq, HALF), F32),
        scratch_shapes=[pltpu.VMEM((nj, chunk * SUBLANES, 2 * sw), F32),
                        pltpu.VMEM((2 * nj, SUBLANES, sw), F32)],
        compiler_params=_cparams("parallel", "arbitrary", "arbitrary"),
        name="s5_scan",
    )(u3, a5, bm, cm)


def _gelu_tanh(x):
    return 0.5 * x * (1.0 + jnp.tanh(math.sqrt(2.0 / math.pi) * (x + 0.044715 * (x * x * x))))


def _out_proj_cd_kernel(x_ref, u_ref, yf_ref, yb_ref, mla_ref, d_ref, gw_ref, gb_ref, w_ref, o_ref):
    y = u_ref[...] * d_ref[...] + yf_ref[0] + yb_ref[0]
    g = _gelu_tanh(y)
    z = _dot(g.astype(BF16), gw_ref[...]) + gb_ref[...]
    y_s5 = g * jax.nn.sigmoid(z)
    acc = _dot(y_s5.astype(BF16), w_ref[0:HALF, :]) + _dot(mla_ref[...], w_ref[HALF:, :])
    o_ref[...] = x_ref[...] + acc


def out_proj_cd(x2d, u2d, y2, y_mla, d, glu_w, glu_b, w, bsz, seq, tm=512):
    tm = min(tm, seq)
    nl = seq // tm
    row = lambda b, i: (b * nl + i, 0)
    const = lambda b, i: (0, 0)
    return pl.pallas_call(
        _out_proj_cd_kernel,
        grid=(bsz, nl),
        in_specs=[pl.BlockSpec((tm, D_MODEL), row),
                  pl.BlockSpec((tm, HALF), row),
                  pl.BlockSpec((1, tm, HALF), lambda b, i: (0, b * nl + i, 0)),
                  pl.BlockSpec((1, tm, HALF), lambda b, i: (1, b * nl + i, 0)),
                  pl.BlockSpec((tm, HALF), row),
                  pl.BlockSpec((1, HALF), const),
                  pl.BlockSpec((HALF, HALF), const),
                  pl.BlockSpec((1, HALF), const),
                  pl.BlockSpec((D_MODEL, D_MODEL), const)],
        out_specs=pl.BlockSpec((tm, D_MODEL), row),
        out_shape=jax.ShapeDtypeStruct(x2d.shape, F32),
        compiler_params=_cparams("parallel", "parallel"),
        name="out_proj_cd",
    )(x2d, u2d, y2, y2, y_mla, d.reshape(1, HALF), glu_w, glu_b.reshape(1, HALF), w)


MLA_HP = 128


def mla_weights(w_uq, w_ukv):
    rq = w_uq.shape[0]
    wq = w_uq.astype(F32).reshape(rq, MLA_HEADS, MLA_NOPE + MLA_ROPE)
    half = MLA_ROPE // 2
    x1, x2 = wq[..., MLA_NOPE:MLA_NOPE + half], wq[..., MLA_NOPE + half:]
    zpad = jnp.zeros((rq, MLA_HEADS, MLA_HP - MLA_NOPE - MLA_ROPE), F32)
    wq1 = jnp.concatenate([wq, zpad], axis=-1).reshape(rq, MLA_HEADS * MLA_HP)
    wq2 = jnp.concatenate([jnp.zeros((rq, MLA_HEADS, MLA_NOPE), F32), -x2, x1, zpad], axis=-1)
    wq2 = wq2.reshape(rq, MLA_HEADS * MLA_HP)
    rk = w_ukv.shape[0]
    wkv = w_ukv.astype(F32).reshape(rk, MLA_HEADS, MLA_NOPE + MLA_V)
    wk = jnp.concatenate([wkv[..., :MLA_NOPE], jnp.zeros((rk, MLA_HEADS, MLA_HP - MLA_NOPE), F32)], axis=-1)
    wk = wk.reshape(rk, MLA_HEADS * MLA_HP)
    wv = wkv[..., MLA_NOPE:].reshape(rk, MLA_HEADS * MLA_V)
    return wq1.astype(BF16), wq2.astype(BF16), wk.astype(BF16), wv.astype(BF16)


def rope_tables(seq):
    inv = 1.0 / (ROPE_THETA ** (jnp.arange(0, MLA_ROPE, 2, dtype=F32) / MLA_ROPE))
    ang = jnp.arange(seq, dtype=F32)[:, None] * inv[None, :]
    c, s = jnp.cos(ang), jnp.sin(ang)
    ones = jnp.ones((seq, MLA_NOPE), F32)
    zpad = jnp.zeros((seq, MLA_HP - MLA_NOPE - MLA_ROPE), F32)
    cos_t = jnp.concatenate([ones, c, c, zpad], axis=-1)
    sin_t = jnp.concatenate([0.0 * ones, s, s, zpad], axis=-1)
    return cos_t, sin_t


def _mla_prep_kernel(r_ref, qg_ref, kg_ref, wq1_ref, wq2_ref, wk_ref, wv_ref, cos_ref, sin_ref,
                     q_ref, k_ref, v_ref):
    scale = (MLA_NOPE + MLA_ROPE) ** -0.5 * LOG2E
    cq = _rms(r_ref[:, 0:MLA_Q_RANK].astype(F32), qg_ref[...]).astype(BF16)
    o1 = MLA_Q_RANK + MLA_KV_RANK
    ckv = _rms(r_ref[:, MLA_Q_RANK:o1].astype(F32), kg_ref[...]).astype(BF16)
    cos_t, sin_t = cos_ref[...], sin_ref[...]
    kr = r_ref[:, o1:o1 + LANES].astype(F32) * cos_t + r_ref[:, o1 + LANES:o1 + 2 * LANES].astype(F32) * sin_t
    v_ref[...] = _dot(ckv, wv_ref[...]).astype(v_ref.dtype)
    for h in range(MLA_HEADS):
        sl = slice(h * MLA_HP, (h + 1) * MLA_HP)
        qh = _dot(cq, wq1_ref[:, sl]) * cos_t + _dot(cq, wq2_ref[:, sl]) * sin_t
        q_ref[:, sl] = (qh * scale).astype(q_ref.dtype)
        k_ref[:, sl] = (_dot(ckv, wk_ref[:, sl]) + kr).astype(k_ref.dtype)


def mla_prep(rest, q_norm, kv_norm, wq1, wq2, wk, wv, cos_t, sin_t, bsz, seq, tm=512):
    tm = min(tm, seq)
    nl = seq // tm
    t = bsz * seq
    wr = rest.shape[1]
    row = lambda b, i: (b * nl + i, 0)
    const = lambda b, i: (0, 0)
    qk = MLA_HEADS * MLA_HP
    return pl.pallas_call(
        _mla_prep_kernel,
        grid=(bsz, nl),
        in_specs=[pl.BlockSpec((tm, wr), row),
                  pl.BlockSpec((1, MLA_Q_RANK), const),
                  pl.BlockSpec((1, MLA_KV_RANK), const),
                  pl.BlockSpec((MLA_Q_RANK, qk), const),
                  pl.BlockSpec((MLA_Q_RANK, qk), const),
                  pl.BlockSpec((MLA_KV_RANK, qk), const),
                  pl.BlockSpec((MLA_KV_RANK, MLA_HEADS * MLA_V), const),
                  pl.BlockSpec((tm, MLA_HP), lambda b, i: (i, 0)),
                  pl.BlockSpec((tm, MLA_HP), lambda b, i: (i, 0))],
        out_specs=[pl.BlockSpec((tm, qk), row), pl.BlockSpec((tm, qk), row),
                   pl.BlockSpec((tm, MLA_HEADS * MLA_V), row)],
        out_shape=[jax.ShapeDtypeStruct((t, qk), BF16), jax.ShapeDtypeStruct((t, qk), BF16),
                   jax.ShapeDtypeStruct((t, MLA_HEADS * MLA_V), BF16)],
        compiler_params=_cparams("parallel", "parallel"),
        name="mla_prep",
    )(rest, q_norm.reshape(1, -1), kv_norm.reshape(1, -1), wq1, wq2, wk, wv, cos_t, sin_t)


def _mla_attn_kernel(q_ref, k_ref, v_ref, o_ref, *, kb):
    nk = k_ref.shape[0] // kb
    outs = []
    for h in range(2):
        sl = slice(h * MLA_HP, (h + 1) * MLA_HP)
        q = q_ref[:, sl]
        m = den = acc = None
        for j in range(nk):
            ks = slice(j * kb, (j + 1) * kb)
            s = _dot_nt(q, k_ref[ks, sl])
            mj = jnp.max(s, axis=-1, keepdims=True)
            if j == 0:
                m = mj
                p = jnp.exp2(s - m)
                den = jnp.sum(p, axis=-1, keepdims=True)
                acc = _dot(p.astype(BF16), v_ref[ks, :])
            else:
                m_new = jnp.maximum(m, mj)
                alpha = jnp.exp2(m - m_new)
                p = jnp.exp2(s - m_new)
                den = alpha * den + jnp.sum(p, axis=-1, keepdims=True)
                acc = alpha * acc + _dot(p.astype(BF16), v_ref[ks, :])
                m = m_new
        outs.append(acc / den)
    lane = lax.broadcasted_iota(jnp.int32, outs[0].shape, 1)
    o_ref[...] = jnp.where(lane < MLA_V, outs[0], outs[1]).astype(o_ref.dtype)


def mla_attn(q, k, v, bsz, seq, tq=512, kb=256):
    tq = min(tq, seq)
    nq = seq // tq
    return pl.pallas_call(
        functools.partial(_mla_attn_kernel, kb=min(kb, seq)),
        grid=(bsz, MLA_HEADS // 2, nq),
        in_specs=[pl.BlockSpec((tq, 2 * MLA_HP), lambda b, p, i: (b * nq + i, p)),
                  pl.BlockSpec((seq, 2 * MLA_HP), lambda b, p, i: (b, p)),
                  pl.BlockSpec((seq, 2 * MLA_V), lambda b, p, i: (b, p))],
        out_specs=pl.BlockSpec((tq, 2 * MLA_V), lambda b, p, i: (b * nq + i, p)),
        out_shape=jax.ShapeDtypeStruct((bsz * seq, MLA_HEADS * MLA_V), BF16),
        compiler_params=_cparams("parallel", "parallel", "parallel"),
        name="mla_attn",
    )(q, k, v)


def _cross_router_kernel(x_ref, kv_ref, gc_ref, wq_ref, wo_ref, gf_ref, wr_ref, xo_ref, hn_ref, aff_ref):
    x = x_ref[...]
    h = _rms(x, gc_ref[...]).astype(BF16)
    q = (_dot(h, wq_ref[...]) * (CA_HEAD_DIM ** -0.5 * LOG2E)).astype(BF16)
    hd = CA_HEADS * CA_HEAD_DIM
    outs = []
    for a in range(CA_HEADS):
        sl = slice(a * CA_HEAD_DIM, (a + 1) * CA_HEAD_DIM)
        s = _dot_nt(q[:, sl], kv_ref[:, sl])
        m = jnp.max(s, axis=-1, keepdims=True)
        p = jnp.exp2(s - m)
        den = jnp.sum(p, axis=-1, keepdims=True)
        outs.append((_dot(p.astype(BF16), kv_ref[:, hd + a * CA_HEAD_DIM:hd + (a + 1) * CA_HEAD_DIM]) / den))
    o = jnp.concatenate(outs, axis=-1).astype(BF16)
    xn = x + _dot(o, wo_ref[...])
    xo_ref[...] = xn
    hf = _rms(xn, gf_ref[...])
    hb = hf.astype(BF16)
    hn_ref[...] = hb
    lo = (hf - hb.astype(F32)).astype(BF16)
    hw = _dot(hb, wr_ref[...])
    logits = hw[:, 0:LANES] + (_dot(lo, wr_ref[:, 0:LANES]) + hw[:, LANES:])
    lane = lax.broadcasted_iota(jnp.int32, logits.shape, 1)
    logits = jnp.where(lane < N_EXPERTS, logits, NEG)
    m = jnp.max(logits, axis=-1, keepdims=True)
    e = jnp.exp(logits - m)
    aff = e / jnp.sum(e, axis=-1, keepdims=True)
    aff_ref[...] = aff[:, 0:N_EXPERTS]


def cross_router(x2d, kv, ln_cross, w_q, w_o, ln_ffn, w_router, bsz, seq, tq=256):
    tq = min(tq, seq)
    nq = seq // tq
    t = bsz * seq
    mem = kv.shape[0] // bsz
    hd = CA_HEADS * CA_HEAD_DIM
    wr = jnp.pad(w_router.astype(F32), ((0, 0), (0, LANES - N_EXPERTS)))
    wr_hi = wr.astype(BF16)
    wr2 = jnp.concatenate([wr_hi, (wr - wr_hi.astype(F32)).astype(BF16)], axis=1)
    row = lambda b, i: (b * nq + i, 0)
    const = lambda b, i: (0, 0)
    return pl.pallas_call(
        _cross_router_kernel,
        grid=(bsz, nq),
        in_specs=[pl.BlockSpec((tq, D_MODEL), row),
                  pl.BlockSpec((mem, 2 * hd), lambda b, i: (b, 0)),
                  pl.BlockSpec((1, D_MODEL), const),
                  pl.BlockSpec((D_MODEL, hd), const),
                  pl.BlockSpec((hd, D_MODEL), const),
                  pl.BlockSpec((1, D_MODEL), const),
                  pl.BlockSpec((D_MODEL, 2 * LANES), const)],
        out_specs=[pl.BlockSpec((tq, D_MODEL), row), pl.BlockSpec((tq, D_MODEL), row),
                   pl.BlockSpec((tq, N_EXPERTS), row)],
        out_shape=[jax.ShapeDtypeStruct((t, D_MODEL), F32), jax.ShapeDtypeStruct((t, D_MODEL), BF16),
                   jax.ShapeDtypeStruct((t, N_EXPERTS), F32)],
        compiler_params=_cparams("parallel", "parallel"),
        name="cross_router",
    )(x2d, kv, ln_cross.reshape(1, -1), w_q, w_o, ln_ffn.reshape(1, -1), wr2)


def _expert_ffn_kernel(x_ref, g_ref, wg_ref, wu_ref, wd_ref, o_ref, hid_ref, *, tf):
    x = x_ref[0]
    for c0 in range(0, D_EXPERT, tf):
        a = _dot(x, wg_ref[0, 0, :, c0:c0 + tf])
        u = _dot(x, wu_ref[0, 0, :, c0:c0 + tf])
        hid_ref[:, c0:c0 + tf] = (a * jax.nn.sigmoid(a) * u).astype(BF16)
    o_ref[0] = (_dot(hid_ref[...], wd_ref[0, 0]) * g_ref[0]).astype(o_ref.dtype)


def expert_ffn(xe, gate, w_gate, w_up, w_down, layer, tm=1024, tf=512):
    e, cap, _ = xe.shape
    tm = min(tm, cap)
    return pl.pallas_call(
        functools.partial(_expert_ffn_kernel, tf=tf),
        grid=(e, cap // tm),
        in_specs=[pl.BlockSpec((1, tm, D_MODEL), lambda e, m: (e, m, 0)),
                  pl.BlockSpec((1, tm, 1), lambda e, m: (e, m, 0)),
                  pl.BlockSpec((1, 1, D_MODEL, D_EXPERT), lambda e, m: (layer, e, 0, 0)),
                  pl.BlockSpec((1, 1, D_MODEL, D_EXPERT), lambda e, m: (layer, e, 0, 0)),
                  pl.BlockSpec((1, 1, D_EXPERT, D_MODEL), lambda e, m: (layer, e, 0, 0))],
        out_specs=pl.BlockSpec((1, tm, D_MODEL), lambda e, m: (e, m, 0)),
        out_shape=jax.ShapeDtypeStruct((e, cap, D_MODEL), BF16),
        scratch_shapes=[pltpu.VMEM((tm, D_EXPERT), BF16)],
        compiler_params=_cparams("parallel", "arbitrary"),
        name="expert_ffn",
    )(xe, gate, w_gate, w_up, w_down)


ROUTE_GROUP = SUBLANES


def _route_thr_kernel(a_ref, thr_ref, *, cap):
    bits = pltpu.bitcast(a_ref[...], jnp.int32)

    def body(i, lo):
        cand = lo | jnp.left_shift(jnp.int32(1), 30 - i)
        cnt = jnp.sum(jnp.where(bits >= cand, 1.0, 0.0), axis=1, keepdims=True)
        return jnp.where(cnt >= cap, cand, lo)

    thr_ref[...] = lax.fori_loop(0, 31, body, jnp.zeros(thr_ref.shape, jnp.int32))


def _prefix_rows(m, upper, lower):
    mb = m.astype(BF16)
    incl = _dot(mb, upper)
    tot = jnp.broadcast_to(incl[:, LANES - 1:LANES], incl.shape).astype(BF16)
    return incl - m + _dot(lower, tot)


def _route_mask_kernel(thr_ref, a_ref, upper_ref, lower_ref, sel_ref, pos_ref, *, cap):
    e = pl.program_id(0)
    thr = thr_ref[e]
    bits = pltpu.bitcast(a_ref[0], jnp.int32)
    gt = jnp.where(bits > thr, 1.0, 0.0)
    eq = jnp.where(bits == thr, 1.0, 0.0)
    need = cap - jnp.sum(jnp.sum(gt, axis=1, keepdims=True), axis=0, keepdims=True)
    eq_rank = _prefix_rows(eq, upper_ref[...], lower_ref[...])
    sel = gt + jnp.where(eq_rank < need, eq, 0.0)
    sel_ref[0] = sel
    pos_ref[0] = _prefix_rows(sel, upper_ref[...], lower_ref[...]).astype(jnp.int32)


def _route_compact_kernel(glo_ref, ghi_ref, a_ref, pos_ref, idx_ref, gate_ref):
    e = pl.program_id(0)
    nc = idx_ref.shape[1]
    gw = ROUTE_GROUP * LANES
    slot0 = lax.broadcasted_iota(jnp.int32, (LANES, LANES), 0)
    lane = lax.broadcasted_iota(jnp.int32, (1, gw), 1)
    zeros = jnp.zeros((2 * SUBLANES - 5, gw), F32)

    ngroups = pos_ref.shape[1] // ROUTE_GROUP
    never = jnp.int32(1 << 30)

    def terms(g, slot):
        r0 = pl.multiple_of(g * ROUTE_GROUP, ROUTE_GROUP)
        pos = pos_ref[0, pl.ds(r0, ROUTE_GROUP), :]
        aff = a_ref[0, pl.ds(r0, ROUTE_GROUP), :]
        hit = jnp.concatenate(
            [jnp.where(pos[j:j + 1] == slot, 1.0, 0.0) for j in range(ROUTE_GROUP)],
            axis=1).astype(BF16)
        arow = jnp.concatenate([aff[j:j + 1] for j in range(ROUTE_GROUP)], axis=1)
        tok = lane + g * gw
        g0 = arow.astype(BF16).astype(F32)
        r1 = arow - g0
        g1 = r1.astype(BF16).astype(F32)
        g2 = r1 - g1
        lhs = jnp.concatenate([(tok >> 8).astype(F32), (tok & 255).astype(F32), g0, g1, g2, zeros], axis=0)
        return lhs.astype(BF16), hit

    def token_id(acc):
        return (acc[0:1] * 256.0 + acc[1:2]).astype(jnp.int32)

    def affinity(acc):
        return (acc[2:3] + acc[3:4]) + acc[4:5]

    def first_two(c, carry):
        slot = slot0 + c * LANES
        g_first = glo_ref[e * nc + c]
        g_second = jnp.minimum(g_first + 1, ngroups - 1)
        lhs_a, hit_a = terms(g_first, slot)
        lhs_b, hit_b = terms(g_second, slot + jnp.where(g_first + 1 <= ghi_ref[e * nc + c], 0, never))
        acc = _dot_nt(jnp.concatenate([lhs_a, lhs_b], axis=1), jnp.concatenate([hit_a, hit_b], axis=1))
        idx_ref[0, pl.ds(c, 1), :] = token_id(acc)
        gate_ref[0, pl.ds(c, 1), :] = affinity(acc)
        return carry

    lax.fori_loop(0, nc, first_two, 0, unroll=4)

    def further(c, carry):
        g_first = glo_ref[e * nc + c]
        g_last = ghi_ref[e * nc + c]

        @pl.when(g_last >= g_first + 2)
        def _():
            slot = slot0 + c * LANES

            def group(g, acc):
                lhs, hit = terms(g, slot)
                return acc + _dot_nt(lhs, hit)

            acc = lax.fori_loop(g_first + 2, g_last + 1, group, jnp.zeros((2 * SUBLANES, LANES), F32))
            idx_ref[0, pl.ds(c, 1), :] += token_id(acc)
            gate_ref[0, pl.ds(c, 1), :] += affinity(acc)

        return carry

    lax.fori_loop(0, nc, further, 0)


def route_tokens(aff, cap):
    t, ne = aff.shape
    rows = t // LANES
    nc = cap // LANES
    aff_t = aff.T
    thr = pl.pallas_call(
        functools.partial(_route_thr_kernel, cap=cap),
        out_shape=jax.ShapeDtypeStruct((ne, 1), jnp.int32),
        compiler_params=pltpu.CompilerParams(vmem_limit_bytes=V7X_VMEM_LIMIT_BYTES),
        name="route_threshold",
    )(aff_t)
    aff3 = aff_t.reshape(ne, rows, LANES)
    ii = jnp.arange(LANES)
    upper = (ii[:, None] <= ii[None, :]).astype(BF16)
    rr = jnp.arange(rows)
    lower = (rr[None, :] < rr[:, None]).astype(BF16)
    blk = pl.BlockSpec((1, rows, LANES), lambda e: (e, 0, 0))
    sel, pos = pl.pallas_call(
        functools.partial(_route_mask_kernel, cap=cap),
        grid=(ne,),
        in_specs=[pl.BlockSpec(memory_space=pltpu.SMEM), blk,
                  pl.BlockSpec((LANES, LANES), lambda e: (0, 0)),
                  pl.BlockSpec((rows, rows), lambda e: (0, 0))],
        out_specs=[blk, blk],
        out_shape=[jax.ShapeDtypeStruct((ne, rows, LANES), F32), jax.ShapeDtypeStruct((ne, rows, LANES), jnp.int32)],
        compiler_params=_cparams("parallel"),
        name="route_mask",
    )(thr.reshape(ne), aff3, upper, lower)
    gsz = ROUTE_GROUP
    first = pos[:, ::gsz, 0]
    starts = jnp.arange(nc, dtype=jnp.int32) * LANES
    glo = jnp.sum(first[:, None, :] <= starts[None, :, None], axis=-1, dtype=jnp.int32) - 1
    ghi = jnp.sum(first[:, None, :] <= (starts + (LANES - 1))[None, :, None], axis=-1, dtype=jnp.int32) - 1
    posm = jnp.where(sel > 0, pos, -1)
    out_blk = pl.BlockSpec((1, nc, LANES), lambda e, *_: (e, 0, 0))
    grid_spec = pltpu.PrefetchScalarGridSpec(
        num_scalar_prefetch=2,
        grid=(ne,),
        in_specs=[pl.BlockSpec((1, rows, LANES), lambda e, *_: (e, 0, 0))] * 2,
        out_specs=[out_blk, out_blk],
    )
    idx, gate = pl.pallas_call(
        _route_compact_kernel,
        grid_spec=grid_spec,
        out_shape=[jax.ShapeDtypeStruct((ne, nc, LANES), jnp.int32), jax.ShapeDtypeStruct((ne, nc, LANES), F32)],
        compiler_params=_cparams("parallel"),
        name="route_compact",
    )(glo.reshape(-1), ghi.reshape(-1), aff3, posm)
    return idx.reshape(ne, cap), gate.reshape(ne, cap), pos.reshape(ne, t), posm.reshape(ne, t)


COMBINE_SUB = 512
COMBINE_ROWS = 96
COMBINE_XROWS = 128
COMBINE_ALIGN = 16


def _combine_kernel(wst_ref, nex_ref, x_ref, pos_ref, rel_ref, spread_ref, rowin_ref, g_ref, ye_hbm, o_ref,
                    ybuf, sem, xbuf, xsem, *, nsub, cap, final_norm):
    b = pl.program_id(0)
    nb = pl.num_programs(0)
    ne = N_EXPERTS
    w = COMBINE_ROWS
    xw = COMBINE_XROWS
    sub = COMBINE_SUB
    slot = b % 2
    lane = lax.broadcasted_iota(jnp.int32, (1, LANES), 1)
    row_in_window = rowin_ref[...]

    def window(bb, sl, u, e):
        st = pl.multiple_of(wst_ref[(bb * nsub + u) * ne + e], COMBINE_ALIGN)
        return pltpu.make_async_copy(ye_hbm.at[e, pl.ds(st, w), :], ybuf.at[sl, u, pl.ds(e * w, w), :],
                                     sem.at[sl, u, e])

    def start_all(bb, sl):
        for u in range(nsub):
            for e in range(ne):
                window(bb, sl, u, e).start()

    @pl.when(b == 0)
    def _():
        start_all(b, slot)

    @pl.when(b + 1 < nb)
    def _():
        start_all(b + 1, 1 - slot)

    for u in range(nsub):
        rows = slice(u * sub, (u + 1) * sub)
        base = (b * nsub + u) * ne
        spread = _dot(rel_ref[rows, :], spread_ref[...])
        onehot = jnp.where(spread == row_in_window, 1.0, 0.0).astype(BF16)
        for e in range(ne):
            window(b, slot, u, e).wait()
        o_ref[rows, :] = x_ref[rows, :] + _dot(onehot, ybuf[slot, u])

        for e in range(ne):
            def extra(k, carry, e=e, rows=rows, base=base):
                first = wst_ref[base + e] + w + (k - 1) * xw
                st = pl.multiple_of(jnp.minimum(first, cap - xw), COMBINE_ALIGN)
                cp = pltpu.make_async_copy(ye_hbm.at[e, pl.ds(st, xw), :], xbuf, xsem.at[0])
                cp.start()
                cp.wait()
                col = jnp.broadcast_to(pos_ref[rows, e:e + 1], (sub, xw))
                hit = jnp.logical_and(col - st == lane, col >= first)
                o_ref[rows, :] += _dot(jnp.where(hit, 1.0, 0.0).astype(BF16), xbuf[...])
                return carry

            lax.fori_loop(1, nex_ref[base + e] + 1, extra, 0)

        if final_norm:
            o_ref[rows, :] = _rms(o_ref[rows, :], g_ref[...])


def moe_combine(x2d, ye, pos, posm, norm_gain=None, nsub=1):
    t = x2d.shape[0]
    ne, cap = ye.shape[0], ye.shape[1]
    sub = COMBINE_SUB
    nsub = min(nsub, t // sub)
    tb = sub * nsub
    nb = t // tb
    w = COMBINE_ROWS
    lo = pos[:, ::sub]
    hi = jnp.concatenate([lo[:, 1:], jnp.full((ne, 1), cap, jnp.int32)], axis=1)
    wst = jnp.minimum((lo // COMBINE_ALIGN) * COMBINE_ALIGN, cap - COMBINE_XROWS)
    nex = jnp.maximum(hi - (wst + w) + (COMBINE_XROWS - 1), 0) // COMBINE_XROWS
    posm = posm.T
    rel = posm - jnp.repeat(wst.T, sub, axis=0)
    rel = jnp.where(posm >= 0, jnp.minimum(rel, 2 * LANES - 1), -1).astype(BF16)
    stacked = jnp.arange(ne * w, dtype=jnp.int32)
    spread = stacked[None, :] // w == jnp.arange(ne, dtype=jnp.int32)[:, None]
    rowin = (stacked % w).astype(F32).reshape(1, ne * w)
    gain = jnp.ones((1, D_MODEL), F32) if norm_gain is None else norm_gain.astype(F32).reshape(1, D_MODEL)
    grid_spec = pltpu.PrefetchScalarGridSpec(
        num_scalar_prefetch=2,
        grid=(nb,),
        in_specs=[pl.BlockSpec((tb, D_MODEL), lambda b, *_: (b, 0)),
                  pl.BlockSpec((tb, ne), lambda b, *_: (b, 0)),
                  pl.BlockSpec((tb, ne), lambda b, *_: (b, 0)),
                  pl.BlockSpec((ne, ne * w), lambda b, *_: (0, 0)),
                  pl.BlockSpec((1, ne * w), lambda b, *_: (0, 0)),
                  pl.BlockSpec((1, D_MODEL), lambda b, *_: (0, 0)),
                  pl.BlockSpec(memory_space=pl.ANY)],
        out_specs=pl.BlockSpec((tb, D_MODEL), lambda b, *_: (b, 0)),
        scratch_shapes=[pltpu.VMEM((2, nsub, ne * w, D_MODEL), BF16),
                        pltpu.SemaphoreType.DMA((2, nsub, ne)),
                        pltpu.VMEM((COMBINE_XROWS, D_MODEL), BF16),
                        pltpu.SemaphoreType.DMA((1,))],
    )
    return pl.pallas_call(
        functools.partial(_combine_kernel, nsub=nsub, cap=cap, final_norm=norm_gain is not None),
        grid_spec=grid_spec,
        out_shape=jax.ShapeDtypeStruct(x2d.shape, F32),
        compiler_params=_cparams("arbitrary"),
        name="moe_combine",
    )(wst.T.reshape(-1), nex.T.reshape(-1), x2d, posm, rel, spread.astype(BF16), rowin, gain, ye)


def mixer_ab(x2d, e, p, shared, bsz, seq):
    proj = norm_proj(x2d, p['ln_mix_l'], p['w_in_ab'][e], bsz, seq, [(0, AB_IN, BF16, False)])[0]
    vx, x0 = hyena_pre(proj, p['hy_conv_w'][e], p['hy_conv_b'][e], bsz, seq)
    kp, kq, kp2 = shared['hy_spec'][e]
    y_hy = hyena_conv(vx, x0, shared['dft_fwd'], shared['dft_inv'], kp, kq, kp2, p['hy_d'][e], bsz, seq)
    y_wa = window_attn(proj, p['attn_sink'][e], shared['wa_bias'], bsz, seq)
    return out_proj_ab(x2d, y_hy, y_wa, p['w_out_ab'][e])


def mixer_cd(x2d, o, p, shared, bsz, seq):
    u2d, rest = norm_proj(x2d, p['ln_mix_l'], shared['w_in_cd'][o], bsz, seq,
                          [(0, HALF, F32, False), (HALF, CD_PAD - HALF, BF16, False)])
    a5, bm, cm = shared['s5'][o]
    y_dirs = s5_scan(u2d.reshape(bsz, seq, HALF), a5, bm, cm, bsz, seq)
    wq1, wq2, wk, wv = shared['mla_w'][o]
    q, k, v = mla_prep(rest, p['mla_q_norm'][o], p['mla_kv_norm'][o], wq1, wq2, wk, wv,
                       shared['rope_cos'], shared['rope_sin'], bsz, seq)
    y_mla = mla_attn(q, k, v, bsz, seq)
    return out_proj_cd(x2d, u2d, y_dirs.reshape(2, bsz * seq, HALF), y_mla, p['s5_d'][o],
                       p['s5_glu_w'][o], p['s5_glu_b'][o], p['w_out_cd'][o], bsz, seq)


def ec_moe(x2d, hn, aff, w_gate, w_up, w_down, layer, norm_gain=None):
    t = x2d.shape[0]
    cap = EC_CAPACITY_FACTOR * t // N_EXPERTS
    idx, gate, pos, posm = route_tokens(aff, cap)
    xe = hn[idx]
    ye = expert_ffn(xe, gate[..., None], w_gate, w_up, w_down, layer)
    return moe_combine(x2d, ye, pos, posm, norm_gain)


def prepare_shared(p, seq):
    sh = {}
    sh['wa_bias'] = window_bias_mask(p['rel_bias'])
    sh['dft_fwd'], sh['dft_inv'] = dft_matrices(seq)
    sh['rope_cos'], sh['rope_sin'] = rope_tables(seq)
    specs = []
    for e in range(p['w_in_ab'].shape[0]):
        h = hyena_filters(seq, p['hy_filt_w1'][e], p['hy_filt_b1'][e], p['hy_filt_w2'][e], p['hy_filt_b2'][e],
                          p['hy_filt_w3'][e], p['hy_filt_freq'][e])
        h_fwd, h_bwd = h[:, :HY_WIDTH], h[:, HY_WIDTH:]
        k = jnp.concatenate([h_fwd, jnp.zeros_like(h_fwd[:1]), h_bwd[:0:-1]], axis=0)
        kf = kernel_spectrum(sh['dft_fwd'], k)
        k_r, k_s = kf[:seq], kf[seq:]
        specs.append((k_r, k_s.at[0].set(0.0), k_r.at[0].set(k_s[0])))
    sh['hy_spec'] = specs
    s5, mla_w, w_in_cd = [], [], []
    for o in range(p['w_in_cd'].shape[0]):
        s5.append(s5_discretise(p['s5_a_re'][o], p['s5_a_im'][o], p['s5_log_dt'][o], p['s5_b_re'][o],
                                p['s5_b_im'][o], p['s5_c_re'][o], p['s5_c_im'][o]))
        mla_w.append(mla_weights(p['mla_w_uq'][o], p['mla_w_ukv'][o]))
        w = p['w_in_cd'][o].astype(F32)
        o2 = HALF + MLA_Q_RANK + MLA_KV_RANK
        kr = w[:, o2:o2 + MLA_ROPE]
        half = MLA_ROPE // 2
        kr_rot = jnp.concatenate([-kr[:, half:], kr[:, :half]], axis=1)
        z64 = jnp.zeros((D_MODEL, MLA_NOPE), F32)
        z32 = jnp.zeros((D_MODEL, MLA_HP - MLA_NOPE - MLA_ROPE), F32)
        w_in_cd.append(jnp.concatenate([w[:, :o2], z64, kr, z32, z64, kr_rot, z32], axis=1).astype(BF16))
    sh['s5'], sh['mla_w'], sh['w_in_cd'] = s5, mla_w, w_in_cd
    return sh


def run_trunk(x, mem, p, shared):
    bsz, seq, _ = x.shape
    x2d = x.reshape(bsz * seq, D_MODEL)
    mem2d = mem.reshape(bsz * mem.shape[1], D_MODEL)
    for layer in range(DEPTH):
        pl_ = dict(p, ln_mix_l=p['ln_mix'][layer])
        if layer % 2 == 0:
            x2d = mixer_ab(x2d, layer // 2, pl_, shared, bsz, seq)
        else:
            x2d = mixer_cd(x2d, layer // 2, pl_, shared, bsz, seq)
        kv = norm_proj(mem2d, p['ln_mem'][layer], p['ca_w_kv'][layer], bsz, mem.shape[1],
                       [(0, 2 * CA_HEADS * CA_HEAD_DIM, BF16, False)])[0]
        x2d, hn, aff = cross_router(x2d, kv, p['ln_cross'][layer], p['ca_w_q'][layer], p['ca_w_o'][layer],
                                    p['ln_ffn'][layer], p['moe_w_router'][layer], bsz, seq)
        x2d = ec_moe(x2d, hn, aff, p['moe_w_gate'], p['moe_w_up'], p['moe_w_down'], layer,
                     norm_gain=p['ln_final'] if layer == DEPTH - 1 else None)
    return x2d.reshape(bsz, seq, D_MODEL)


_BF16_WEIGHTS = ('w_in_ab', 'w_out_ab', 'w_out_cd', 's5_glu_w', 'ca_w_q', 'ca_w_kv', 'ca_w_o',
                 'moe_w_gate', 'moe_w_up', 'moe_w_down')


def kernel(x_prompt, x_sample, mem_prompt, mem_sample, ln_mix, ln_cross, ln_mem, ln_ffn, ln_final, rel_bias, w_in_ab, w_out_ab, hy_conv_w, hy_conv_b, hy_filt_w1, hy_filt_b1, hy_filt_w2, hy_filt_b2, hy_filt_w3, hy_filt_freq, hy_d, attn_sink, w_in_cd, w_out_cd, s5_a_re, s5_a_im, s5_log_dt, s5_b_re, s5_b_im, s5_c_re, s5_c_im, s5_d, s5_glu_w, s5_glu_b, mla_q_norm, mla_w_uq, mla_kv_norm, mla_w_ukv, ca_w_q, ca_w_kv, ca_w_o, moe_w_router, moe_w_gate, moe_w_up, moe_w_down):
    p = dict(ln_mix=ln_mix, ln_cross=ln_cross, ln_mem=ln_mem, ln_ffn=ln_ffn, ln_final=ln_final,
             rel_bias=rel_bias, w_in_ab=w_in_ab, w_out_ab=w_out_ab, hy_conv_w=hy_conv_w,
             hy_conv_b=hy_conv_b, hy_filt_w1=hy_filt_w1, hy_filt_b1=hy_filt_b1,
             hy_filt_w2=hy_filt_w2, hy_filt_b2=hy_filt_b2, hy_filt_w3=hy_filt_w3,
             hy_filt_freq=hy_filt_freq, hy_d=hy_d, attn_sink=attn_sink, w_in_cd=w_in_cd,
             w_out_cd=w_out_cd, s5_a_re=s5_a_re, s5_a_im=s5_a_im, s5_log_dt=s5_log_dt,
             s5_b_re=s5_b_re, s5_b_im=s5_b_im, s5_c_re=s5_c_re, s5_c_im=s5_c_im, s5_d=s5_d,
             s5_glu_w=s5_glu_w, s5_glu_b=s5_glu_b, mla_q_norm=mla_q_norm, mla_w_uq=mla_w_uq,
             mla_kv_norm=mla_kv_norm, mla_w_ukv=mla_w_ukv, ca_w_q=ca_w_q, ca_w_kv=ca_w_kv,
             ca_w_o=ca_w_o, moe_w_router=moe_w_router, moe_w_gate=moe_w_gate,
             moe_w_up=moe_w_up, moe_w_down=moe_w_down)
    assert x_prompt.shape[1] == x_sample.shape[1]
    shared = prepare_shared(p, x_prompt.shape[1])
    for name in _BF16_WEIGHTS:
        p[name] = p[name].astype(BF16)
    y_prompt = run_trunk(x_prompt, mem_prompt, p, shared)
    y_sample = run_trunk(x_sample, mem_sample, p, shared)
    return (y_prompt, y_sample)
```

```python
import functools
import math

import jax
import jax.numpy as jnp
import numpy as np
from jax import lax
from jax.experimental import pallas as pl
from jax.experimental.pallas import tpu as pltpu

D_MODEL = 1024
DEPTH = 4
HALF = 512
HEAD_DIM = 64
EPS = 1e-6
NEG = -1e30

HY_WIDTH = HALF
HY_EMB = 33
HY_BANDS = (HY_EMB - 1) // 2
HY_FILT_HIDDEN = 64
HY_DECAY_TARGET = 1e-2
HY_FAST = 0.3
HY_SLOW = 1.5
HY_MIN_DECAY = math.log(HY_DECAY_TARGET) / HY_SLOW
HY_MAX_DECAY = math.log(HY_DECAY_TARGET) / HY_FAST
HY_SHIFT = 0.05

WA_HEADS = 8
WA_KV_HEADS = 2
WA_REP = 4
WA_WINDOW = 128
WA_BLOCK = 128
REL_BUCKETS = 32
REL_MAX_DIST = 128

S5_GROUP = 16
S5_GROUPS = 32
S5_STATE = 64

MLA_HEADS = 8
MLA_NOPE = 64
MLA_ROPE = 32
MLA_V = 64
MLA_Q_RANK = 256
MLA_KV_RANK = 128
ROPE_THETA = 10000.0

CA_HEADS = 4
CA_HEAD_DIM = 128

N_EXPERTS = 16
EC_CAPACITY_FACTOR = 2
D_EXPERT = 2048

AB_IN = 3 * HY_WIDTH + (WA_HEADS + 2 * WA_KV_HEADS) * HEAD_DIM
CD_PAD = 1152

V7X_VMEM_LIMIT_BYTES = 56 * 1024 * 1024
LANES = 128
SUBLANES = 8
LOG2E = math.log2(math.e)

BF16 = jnp.bfloat16
F32 = jnp.float32


def _cparams(*sem):
    return pltpu.CompilerParams(dimension_semantics=sem, vmem_limit_bytes=V7X_VMEM_LIMIT_BYTES)


def _dot(a, b):
    return jnp.dot(a, b, preferred_element_type=F32)


def _dot_nt(a, b):
    return lax.dot_general(a, b, (((1,), (1,)), ((), ())), preferred_element_type=F32)


def _rms(xf, g):
    return xf * lax.rsqrt(jnp.mean(xf * xf, axis=-1, keepdims=True) + EPS) * g


def _norm_proj_kernel(x_ref, g_ref, w_ref, *out_refs, splits):
    hn = _rms(x_ref[...].astype(F32), g_ref[...]).astype(BF16)
    for o_ref, (start, width) in zip(out_refs, splits):
        for c0 in range(0, width, 512):
            cw = min(512, width - c0)
            o_ref[:, c0:c0 + cw] = _dot(hn, w_ref[:, start + c0:start + c0 + cw]).astype(o_ref.dtype)


def norm_proj(x2d, gain, w, bsz, seq, outs, tm=512):
    tm = min(tm, seq)
    nl = seq // tm
    n = w.shape[1]
    out_shapes, out_specs, splits = [], [], []
    for start, width, dtype, time_major in outs:
        splits.append((start, width))
        if time_major:
            out_shapes.append(jax.ShapeDtypeStruct((seq, bsz * width), dtype))
            out_specs.append(pl.BlockSpec((tm, width), lambda b, i: (i, b)))
        else:
            out_shapes.append(jax.ShapeDtypeStruct((bsz * seq, width), dtype))
            out_specs.append(pl.BlockSpec((tm, width), lambda b, i, nl=nl: (b * nl + i, 0)))
    return pl.pallas_call(
        functools.partial(_norm_proj_kernel, splits=tuple(splits)),
        grid=(bsz, nl),
        in_specs=[pl.BlockSpec((tm, D_MODEL), lambda b, i, nl=nl: (b * nl + i, 0)),
                  pl.BlockSpec((1, D_MODEL), lambda b, i: (0, 0)),
                  pl.BlockSpec((D_MODEL, n), lambda b, i: (0, 0))],
        out_specs=out_specs,
        out_shape=out_shapes,
        compiler_params=_cparams("parallel", "parallel"),
        name="norm_proj",
    )(x2d, gain.reshape(1, D_MODEL), w)


def _hyena_filter_kernel(z_ref, w1_ref, b1_ref, w2_ref, b2_ref, w3_ref, fr_ref, win_ref, o_ref):
    hp = lax.Precision.HIGHEST
    fr = fr_ref[...]
    h = jnp.sin(fr * (jnp.dot(z_ref[...], w1_ref[...], precision=hp, preferred_element_type=F32) + b1_ref[...]))
    h = jnp.sin(fr * (jnp.dot(h, w2_ref[...], precision=hp, preferred_element_type=F32) + b2_ref[...]))
    h = jnp.dot(h, w3_ref[...], precision=hp, preferred_element_type=F32)
    o_ref[...] = h * win_ref[...]


def hyena_filters(seq, w1, b1, w2, b2, w3, freq):
    t = jnp.linspace(0.0, 1.0, seq, dtype=F32)[:, None]
    ang = 2.0 * math.pi * jnp.arange(seq, dtype=F32)[:, None] / seq
    bands = jnp.linspace(1e-4, HY_BANDS - 1, HY_BANDS, dtype=F32)[None, :]
    z = jnp.concatenate([t, jnp.cos(bands * ang), -jnp.sin(bands * ang)], axis=-1)
    zp = jnp.pad(z, ((0, 0), (0, HY_FILT_HIDDEN - HY_EMB)))
    w1p = jnp.pad(w1.astype(F32), ((0, HY_FILT_HIDDEN - HY_EMB), (0, 0)))
    deltas = jnp.abs(jnp.linspace(HY_MIN_DECAY, HY_MAX_DECAY, HY_WIDTH, dtype=F32))
    window = jnp.exp(-t * deltas[None, :]) + HY_SHIFT
    win2 = jnp.concatenate([window, window], axis=-1)
    tl = min(512, seq)
    hh = HY_FILT_HIDDEN
    full = lambda r, c: pl.BlockSpec((r, c), lambda i: (0, 0))
    return pl.pallas_call(
        _hyena_filter_kernel,
        grid=(seq // tl,),
        in_specs=[pl.BlockSpec((tl, hh), lambda i: (i, 0)), full(hh, hh), full(1, hh), full(hh, hh), full(1, hh),
                  full(hh, 2 * HY_WIDTH), full(1, hh), pl.BlockSpec((tl, 2 * HY_WIDTH), lambda i: (i, 0))],
        out_specs=pl.BlockSpec((tl, 2 * HY_WIDTH), lambda i: (i, 0)),
        out_shape=jax.ShapeDtypeStruct((seq, 2 * HY_WIDTH), F32),
        compiler_params=_cparams("parallel"),
        name="hyena_filter",
    )(zp, w1p, b1.reshape(1, hh), w2, b2.reshape(1, hh), w3, freq.reshape(1, hh), win2)


def dft_matrices(seq):
    n = 2 * seq
    r = jnp.arange(seq, dtype=jnp.int32)[:, None]
    t = jnp.arange(n, dtype=jnp.int32)[None, :]
    ang = ((r * t) % n).astype(F32) * (2.0 * math.pi / n)
    c, s = jnp.cos(ang), jnp.sin(ang)
    nyq = jnp.where(t % 2 == 0, 1.0, -1.0).astype(F32)
    fwd = jnp.concatenate([c, jnp.where(r == 0, nyq, -s)], axis=0)
    ct = c[:, :seq].T
    st = s[:, :seq].T
    r_row = r.T
    inv_r = jnp.where(r_row == 0, 1.0, 2.0 * ct) / n
    inv_s = jnp.where(r_row == 0, nyq[:, :seq].T, -2.0 * st) / n
    inv = jnp.concatenate([inv_r, inv_s], axis=1)
    return fwd.astype(BF16), inv.astype(BF16)


def _kernel_dft_kernel(a_ref, khi_ref, klo_ref, o_ref):
    o_ref[...] = _dot(a_ref[...], khi_ref[...]) + _dot(a_ref[...], klo_ref[...])


def kernel_spectrum(fwd, k):
    n = fwd.shape[0]
    khi = k.astype(BF16)
    klo = (k - khi.astype(F32)).astype(BF16)
    tf = min(256, n)
    return pl.pallas_call(
        _kernel_dft_kernel,
        grid=(n // tf,),
        in_specs=[pl.BlockSpec((tf, n), lambda i: (i, 0)),
                  pl.BlockSpec((n, HY_WIDTH), lambda i: (0, 0)),
                  pl.BlockSpec((n, HY_WIDTH), lambda i: (0, 0))],
        out_specs=pl.BlockSpec((tf, HY_WIDTH), lambda i: (i, 0)),
        out_shape=jax.ShapeDtypeStruct((n, HY_WIDTH), F32),
        compiler_params=_cparams("parallel"),
        name="hyena_kernel_dft",
    )(fwd, khi, klo)


def _shift_down(u):
    rows = lax.broadcasted_iota(jnp.int32, u.shape, 0)
    return jnp.where(rows == 0, 0.0, pltpu.roll(u, 1, 0))


def _shift_up(u):
    n = u.shape[0]
    rows = lax.broadcasted_iota(jnp.int32, u.shape, 0)
    return jnp.where(rows == n - 1, 0.0, pltpu.roll(u, n - 1, 0))


def _hyena_pre_kernel(u_ref, w_ref, b_ref, vx_ref, x0_ref):
    def conv(c0):
        u = u_ref[:, c0:c0 + LANES].astype(F32)
        w = w_ref[:, c0:c0 + LANES]
        return _shift_down(u) * w[0:1] + u * w[1:2] + _shift_up(u) * w[2:3] + b_ref[:, c0:c0 + LANES]

    for c in range(0, HY_WIDTH, LANES):
        x0_ref[:, c:c + LANES] = conv(c).astype(x0_ref.dtype)
        vx_ref[:, c:c + LANES] = (conv(2 * HY_WIDTH + c) * conv(HY_WIDTH + c)).astype(vx_ref.dtype)


def hyena_pre(proj, conv_w, conv_b, bsz, seq):
    w3 = 3 * HY_WIDTH
    out = jax.ShapeDtypeStruct((bsz * seq, HY_WIDTH), BF16)
    return pl.pallas_call(
        _hyena_pre_kernel,
        grid=(bsz,),
        in_specs=[pl.BlockSpec((seq, w3), lambda b: (b, 0)),
                  pl.BlockSpec((3, w3), lambda b: (0, 0)),
                  pl.BlockSpec((1, w3), lambda b: (0, 0))],
        out_specs=[pl.BlockSpec((seq, HY_WIDTH), lambda b: (b, 0))] * 2,
        out_shape=[out, out],
        compiler_params=_cparams("parallel"),
        name="hyena_pre",
    )(proj, conv_w, conv_b.reshape(1, w3))


def _hyena_conv_kernel(vx_ref, x0_ref, ar_ref, as_ref, inv_ref, kp_ref, kq_ref, kp2_ref, d_ref,
                       o_ref, z_ref, *, nf, tf):
    step = pl.program_id(1)
    half = nf * tf

    @pl.when(step < nf)
    def _():
        vx = vx_ref[...]
        r = _dot(ar_ref[...], vx)
        s = _dot(as_ref[...], vx)
        kq = kq_ref[...]
        f0 = pl.multiple_of(step * tf, tf)
        z_ref[pl.ds(f0, tf), :] = (r * kp_ref[...] - s * kq).astype(BF16)
        z_ref[pl.ds(half + f0, tf), :] = (r * kq + s * kp2_ref[...]).astype(BF16)

    @pl.when(step >= nf)
    def _():
        t0 = pl.multiple_of((step - nf) * tf, tf)
        y = _dot(inv_ref[...], z_ref[...]) + vx_ref[pl.ds(t0, tf), :].astype(F32) * d_ref[...]
        o_ref[pl.ds(t0, tf), :] = (y * x0_ref[pl.ds(t0, tf), :].astype(F32)).astype(o_ref.dtype)


def hyena_conv(vx, x0, fwd, inv, kp, kq, kp2, d, bsz, seq):
    tf = min(512, seq)
    nf = seq // tf
    w = HY_WIDTH
    fwd_tile = lambda b, s: (jnp.minimum(s, nf - 1), 0)
    return pl.pallas_call(
        functools.partial(_hyena_conv_kernel, nf=nf, tf=tf),
        grid=(bsz, 2 * nf),
        in_specs=[pl.BlockSpec((seq, w), lambda b, s: (b, 0)),
                  pl.BlockSpec((seq, w), lambda b, s: (b, 0)),
                  pl.BlockSpec((tf, seq), fwd_tile),
                  pl.BlockSpec((tf, seq), lambda b, s: (nf + jnp.minimum(s, nf - 1), 0)),
                  pl.BlockSpec((tf, 2 * seq), lambda b, s: (jnp.maximum(s - nf, 0), 0)),
                  pl.BlockSpec((tf, w), fwd_tile),
                  pl.BlockSpec((tf, w), fwd_tile),
                  pl.BlockSpec((tf, w), fwd_tile),
                  pl.BlockSpec((1, w), lambda b, s: (0, 0))],
        out_specs=pl.BlockSpec((seq, w), lambda b, s: (b, 0)),
        out_shape=jax.ShapeDtypeStruct((bsz * seq, w), BF16),
        scratch_shapes=[pltpu.VMEM((2 * seq, w), BF16)],
        compiler_params=_cparams("parallel", "arbitrary"),
        name="hyena_conv",
    )(vx, x0, fwd, fwd, inv, kp, kq, kp2, d.reshape(1, w))


def _rel_bucket(rel):
    nb = REL_BUCKETS // 2
    max_exact = nb // 2
    ret = (rel > 0).astype(jnp.int32) * nb
    n = jnp.abs(rel)
    nf = jnp.maximum(n, 1).astype(F32)
    large = max_exact + (jnp.log(nf / max_exact) / math.log(REL_MAX_DIST / max_exact)
                         * (nb - max_exact)).astype(jnp.int32)
    large = jnp.minimum(large, nb - 1)
    return ret + jnp.where(n < max_exact, n, large)


def window_bias_mask(rel_bias):
    j = jnp.arange(WA_BLOCK, dtype=jnp.int32)[:, None]
    s = jnp.arange(3 * WA_BLOCK, dtype=jnp.int32)[None, :]
    rel = (s - WA_BLOCK) - j
    bias = jnp.transpose(rel_bias.astype(F32)[_rel_bucket(rel)], (2, 0, 1))
    band = jnp.abs(rel) <= WA_WINDOW
    return jnp.where(band[None], bias, NEG)


def _window_attn_kernel(sink_ref, q_ref, k_ref, v_ref, bias_ref, o_ref):
    nb = q_ref.shape[0] // WA_BLOCK
    scale = HEAD_DIM ** -0.5 * LOG2E
    rq = WA_REP * WA_BLOCK
    col = lax.broadcasted_iota(jnp.int32, (rq, 3 * WA_BLOCK), 1)
    head_of_row = lax.broadcasted_iota(jnp.int32, (rq, 1), 0) // WA_BLOCK

    def block(i, carry):
        ip = jnp.maximum(i - 1, 0)
        inx = jnp.minimum(i + 1, nb - 1)
        rows = lambda j: pl.ds(pl.multiple_of(j * WA_BLOCK, WA_BLOCK), WA_BLOCK)
        qb = q_ref[rows(i), :]
        kslab = jnp.concatenate([k_ref[rows(ip), :], k_ref[rows(i), :], k_ref[rows(inx), :]], axis=0)
        vslab = jnp.concatenate([v_ref[rows(ip), :], v_ref[rows(i), :], v_ref[rows(inx), :]], axis=0)
        lo = jnp.where(i > 0, 0, WA_BLOCK)
        hi = jnp.where(i < nb - 1, 3 * WA_BLOCK, 2 * WA_BLOCK)
        valid = jnp.logical_and(col >= lo, col < hi)
        outs = []
        for g in range(WA_KV_HEADS):
            heads = range(g * WA_REP, (g + 1) * WA_REP)
            q4 = jnp.concatenate([qb[:, h * HEAD_DIM:(h + 1) * HEAD_DIM] for h in heads], axis=0)
            kg = kslab[:, g * HEAD_DIM:(g + 1) * HEAD_DIM]
            vg = vslab[:, g * HEAD_DIM:(g + 1) * HEAD_DIM]
            s = _dot_nt(q4, kg) * scale + bias_ref[g]
            s = jnp.where(valid, s, NEG)
            sk = jnp.zeros((rq, 1), F32)
            for r, h in enumerate(heads):
                sk = jnp.where(head_of_row == r, sink_ref[h], sk)
            m = jnp.maximum(jnp.max(s, axis=-1, keepdims=True), sk)
            p = jnp.exp2(s - m)
            den = jnp.sum(p, axis=-1, keepdims=True) + jnp.exp2(sk - m)
            o4 = _dot(p.astype(BF16), vg) / den
            outs += [o4[r * WA_BLOCK:(r + 1) * WA_BLOCK] for r in range(WA_REP)]
        o_ref[rows(i), :] = jnp.concatenate(outs, axis=-1).astype(o_ref.dtype)
        return carry

    lax.fori_loop(0, nb, block, 0, unroll=2)


def window_attn(proj, sink, bias_mask, bsz, seq):
    hq = WA_HEADS * HEAD_DIM
    hkv = WA_KV_HEADS * HEAD_DIM
    q_blk = (3 * HY_WIDTH) // hq
    k_blk = (3 * HY_WIDTH + hq) // hkv
    bias2 = (bias_mask * LOG2E).reshape(WA_KV_HEADS, WA_REP * WA_BLOCK, 3 * WA_BLOCK)
    return pl.pallas_call(
        _window_attn_kernel,
        grid=(bsz,),
        in_specs=[pl.BlockSpec(memory_space=pltpu.SMEM),
                  pl.BlockSpec((seq, hq), lambda b: (b, q_blk)),
                  pl.BlockSpec((seq, hkv), lambda b: (b, k_blk)),
                  pl.BlockSpec((seq, hkv), lambda b: (b, k_blk + 1)),
                  pl.BlockSpec((WA_KV_HEADS, WA_REP * WA_BLOCK, 3 * WA_BLOCK), lambda b: (0, 0, 0))],
        out_specs=pl.BlockSpec((seq, hq), lambda b: (b, 0)),
        out_shape=jax.ShapeDtypeStruct((bsz * seq, hq), BF16),
        compiler_params=_cparams("parallel"),
        name="window_attn",
    )(sink.astype(F32) * LOG2E, proj, proj, proj, bias2)


def _out_proj_kernel(x_ref, a_ref, b_ref, w_ref, o_ref):
    acc = _dot(a_ref[...], w_ref[0:HALF, :]) + _dot(b_ref[...], w_ref[HALF:, :])
    o_ref[...] = x_ref[...] + acc


def out_proj_ab(x2d, y_a, y_b, w, tm=512):
    t = x2d.shape[0]
    tm = min(tm, t)
    return pl.pallas_call(
        _out_proj_kernel,
        grid=(t // tm,),
        in_specs=[pl.BlockSpec((tm, D_MODEL), lambda i: (i, 0)),
                  pl.BlockSpec((tm, HALF), lambda i: (i, 0)),
                  pl.BlockSpec((tm, HALF), lambda i: (i, 0)),
                  pl.BlockSpec((D_MODEL, D_MODEL), lambda i: (0, 0))],
        out_specs=pl.BlockSpec((tm, D_MODEL), lambda i: (i, 0)),
        out_shape=jax.ShapeDtypeStruct(x2d.shape, F32),
        compiler_params=_cparams("parallel"),
        name="out_proj_ab",
    )(x2d, y_a, y_b, w)


def s5_discretise(a_re, a_im, log_dt, b_re, b_im, c_re, c_im):
    lam = lax.complex(a_re.astype(F32), a_im.astype(F32))
    dt = jnp.exp(log_dt.astype(F32))[..., None]
    abar = jnp.exp(lam * dt)
    bmat = lax.complex(b_re.astype(F32), b_im.astype(F32))
    bbar = ((abar - 1.0) / lam)[..., None] * bmat
    cmat = lax.complex(c_re.astype(F32), c_im.astype(F32))
    nj, gl = S5_GROUPS // SUBLANES, SUBLANES
    eye = jnp.eye(gl, dtype=F32)
    a5 = jnp.stack([abar.real, abar.imag], axis=1).reshape(2, 2, nj, 1, gl * S5_STATE)
    a5 = jnp.transpose(a5, (0, 2, 1, 3, 4))
    a5 = jnp.broadcast_to(a5, (2, nj, 2, SUBLANES, gl * S5_STATE))

    def pack_b(x):
        x = x.reshape(2, nj, gl, S5_STATE, S5_GROUP)
        y = jnp.einsum('hg,djgpc->djhcgp', eye, x)
        return y.reshape(2, nj, gl * S5_GROUP, gl * S5_STATE)

    def pack_c(x):
        x = x.reshape(2, nj, gl, S5_GROUP, S5_STATE)
        y = jnp.einsum('hg,djgcp->djgphc', eye, x)
        return y.reshape(2, nj, gl * S5_STATE, gl * S5_GROUP)

    bm = jnp.concatenate([pack_b(bbar.real), pack_b(bbar.imag)], axis=-1).astype(BF16)
    cm = jnp.concatenate([pack_c(cmat.real), -pack_c(cmat.imag)], axis=-2).astype(BF16)
    return a5, bm, cm


def _s5_kernel(u_ref, a_ref, b_ref, c_ref, y_ref, buf_ref, h_ref, *, chunk):
    d = pl.program_id(1)
    nj = S5_GROUPS // SUBLANES
    sw = SUBLANES * S5_STATE
    rows = chunk * SUBLANES

    @pl.when(pl.program_id(2) == 0)
    def _():
        h_ref[...] = jnp.zeros_like(h_ref)

    u = pltpu.einshape("btc->(tb)c", u_ref[...]).astype(BF16)
    for j in range(nj):
        buf_ref[j] = _dot(u[:, j * LANES:(j + 1) * LANES], b_ref[0, j])

    for j0 in range(0, nj, 2):
        js = (j0, j0 + 1)

        def step(s, carry, js=js):
            t = jnp.where(d == 0, s, chunk - 1 - s)
            r0 = pl.multiple_of(t * SUBLANES, SUBLANES)
            new = []
            for n, j in enumerate(js):
                hr, hi = carry[2 * n], carry[2 * n + 1]
                ar = a_ref[0, j, 0]
                ai = a_ref[0, j, 1]
                br = buf_ref[j, pl.ds(r0, SUBLANES), 0:sw]
                bi = buf_ref[j, pl.ds(r0, SUBLANES), sw:2 * sw]
                nr = ar * hr - ai * hi + br
                ni = ar * hi + ai * hr + bi
                buf_ref[j, pl.ds(r0, SUBLANES), 0:sw] = nr
                buf_ref[j, pl.ds(r0, SUBLANES), sw:2 * sw] = ni
                new += [nr, ni]
            return tuple(new)

        init = tuple(h_ref[2 * j + k] for j in js for k in range(2))
        fin = lax.fori_loop(0, chunk, step, init, unroll=2)
        for n, j in enumerate(js):
            h_ref[2 * j] = fin[2 * n]
            h_ref[2 * j + 1] = fin[2 * n + 1]

    for j in range(nj):
        yj = _dot(buf_ref[j].astype(BF16), c_ref[0, j])
        y_ref[0, :, :, j * LANES:(j + 1) * LANES] = pltpu.einshape(
            "(tb)c->btc", yj, b=SUBLANES).astype(y_ref.dtype)


def s5_scan(u3, a5, bm, cm, bsz, seq, chunk=64):
    chunk = min(chunk, seq)
    nc = seq // chunk
    nj = S5_GROUPS // SUBLANES
    sw = SUBLANES * S5_STATE

    def tchunk(d, i):
        return i + d * (nc - 1 - 2 * i)

    return pl.pallas_call(
        functools.partial(_s5_kernel, chunk=chunk),
        grid=(bsz // SUBLANES, 2, nc),
        in_specs=[pl.BlockSpec((SUBLANES, chunk, HALF), lambda b, d, i: (b, tchunk(d, i), 0)),
                  pl.BlockSpec((1, nj, 2, SUBLANES, sw), lambda b, d, i: (d, 0, 0, 0, 0)),
                  pl.BlockSpec((1, nj, LANES, 2 * sw), lambda b, d, i: (d, 0, 0, 0)),
                  pl.BlockSpec((1, nj, 2 * sw, LANES), lambda b, d, i: (d, 0, 0, 0))],
        out_specs=pl.BlockSpec((1, SUBLANES, chunk, HALF), lambda b, d, i: (d, b, tchunk(d, i), 0)),
        out_shape=jax.ShapeDtypeStruct((2, bsz, seq, HALF), F32),
        scratch_shapes=[pltpu.VMEM((nj, chunk * SUBLANES, 2 * sw), F32),
                        pltpu.VMEM((2 * nj, SUBLANES, sw), F32)],
        compiler_params=_cparams("parallel", "arbitrary", "arbitrary"),
        name="s5_scan",
    )(u3, a5, bm, cm)


def _gelu_tanh(x):
    return 0.5 * x * (1.0 + jnp.tanh(math.sqrt(2.0 / math.pi) * (x + 0.044715 * (x * x * x))))


def _out_proj_cd_kernel(x_ref, u_ref, yf_ref, yb_ref, mla_ref, d_ref, gw_ref, gb_ref, w_ref, o_ref):
    y = u_ref[...] * d_ref[...] + yf_ref[0] + yb_ref[0]
    g = _gelu_tanh(y)
    z = _dot(g.astype(BF16), gw_ref[...]) + gb_ref[...]
    y_s5 = g * jax.nn.sigmoid(z)
    acc = _dot(y_s5.astype(BF16), w_ref[0:HALF, :]) + _dot(mla_ref[...], w_ref[HALF:, :])
    o_ref[...] = x_ref[...] + acc


def out_proj_cd(x2d, u2d, y2, y_mla, d, glu_w, glu_b, w, bsz, seq, tm=512):
    tm = min(tm, seq)
    nl = seq // tm
    row = lambda b, i: (b * nl + i, 0)
    const = lambda b, i: (0, 0)
    return pl.pallas_call(
        _out_proj_cd_kernel,
        grid=(bsz, nl),
        in_specs=[pl.BlockSpec((tm, D_MODEL), row),
                  pl.BlockSpec((tm, HALF), row),
                  pl.BlockSpec((1, tm, HALF), lambda b, i: (0, b * nl + i, 0)),
                  pl.BlockSpec((1, tm, HALF), lambda b, i: (1, b * nl + i, 0)),
                  pl.BlockSpec((tm, HALF), row),
                  pl.BlockSpec((1, HALF), const),
                  pl.BlockSpec((HALF, HALF), const),
                  pl.BlockSpec((1, HALF), const),
                  pl.BlockSpec((D_MODEL, D_MODEL), const)],
        out_specs=pl.BlockSpec((tm, D_MODEL), row),
        out_shape=jax.ShapeDtypeStruct(x2d.shape, F32),
        compiler_params=_cparams("parallel", "parallel"),
        name="out_proj_cd",
    )(x2d, u2d, y2, y2, y_mla, d.reshape(1, HALF), glu_w, glu_b.reshape(1, HALF), w)


MLA_HP = 128


def mla_weights(w_uq, w_ukv):
    rq = w_uq.shape[0]
    wq = w_uq.astype(F32).reshape(rq, MLA_HEADS, MLA_NOPE + MLA_ROPE)
    half = MLA_ROPE // 2
    x1, x2 = wq[..., MLA_NOPE:MLA_NOPE + half], wq[..., MLA_NOPE + half:]
    zpad = jnp.zeros((rq, MLA_HEADS, MLA_HP - MLA_NOPE - MLA_ROPE), F32)
    wq1 = jnp.concatenate([wq, zpad], axis=-1).reshape(rq, MLA_HEADS * MLA_HP)
    wq2 = jnp.concatenate([jnp.zeros((rq, MLA_HEADS, MLA_NOPE), F32), -x2, x1, zpad], axis=-1)
    wq2 = wq2.reshape(rq, MLA_HEADS * MLA_HP)
    rk = w_ukv.shape[0]
    wkv = w_ukv.astype(F32).reshape(rk, MLA_HEADS, MLA_NOPE + MLA_V)
    wk = jnp.concatenate([wkv[..., :MLA_NOPE], jnp.zeros((rk, MLA_HEADS, MLA_HP - MLA_NOPE), F32)], axis=-1)
    wk = wk.reshape(rk, MLA_HEADS * MLA_HP)
    wv = wkv[..., MLA_NOPE:].reshape(rk, MLA_HEADS * MLA_V)
    return wq1.astype(BF16), wq2.astype(BF16), wk.astype(BF16), wv.astype(BF16)


def rope_tables(seq):
    inv = 1.0 / (ROPE_THETA ** (jnp.arange(0, MLA_ROPE, 2, dtype=F32) / MLA_ROPE))
    ang = jnp.arange(seq, dtype=F32)[:, None] * inv[None, :]
    c, s = jnp.cos(ang), jnp.sin(ang)
    ones = jnp.ones((seq, MLA_NOPE), F32)
    zpad = jnp.zeros((seq, MLA_HP - MLA_NOPE - MLA_ROPE), F32)
    cos_t = jnp.concatenate([ones, c, c, zpad], axis=-1)
    sin_t = jnp.concatenate([0.0 * ones, s, s, zpad], axis=-1)
    return cos_t, sin_t


def _mla_prep_kernel(r_ref, qg_ref, kg_ref, wq1_ref, wq2_ref, wk_ref, wv_ref, cos_ref, sin_ref,
                     q_ref, k_ref, v_ref):
    scale = (MLA_NOPE + MLA_ROPE) ** -0.5 * LOG2E
    cq = _rms(r_ref[:, 0:MLA_Q_RANK].astype(F32), qg_ref[...]).astype(BF16)
    o1 = MLA_Q_RANK + MLA_KV_RANK
    ckv = _rms(r_ref[:, MLA_Q_RANK:o1].astype(F32), kg_ref[...]).astype(BF16)
    cos_t, sin_t = cos_ref[...], sin_ref[...]
    kr = r_ref[:, o1:o1 + LANES].astype(F32) * cos_t + r_ref[:, o1 + LANES:o1 + 2 * LANES].astype(F32) * sin_t
    v_ref[...] = _dot(ckv, wv_ref[...]).astype(v_ref.dtype)
    for h in range(MLA_HEADS):
        sl = slice(h * MLA_HP, (h + 1) * MLA_HP)
        qh = _dot(cq, wq1_ref[:, sl]) * cos_t + _dot(cq, wq2_ref[:, sl]) * sin_t
        q_ref[:, sl] = (qh * scale).astype(q_ref.dtype)
        k_ref[:, sl] = (_dot(ckv, wk_ref[:, sl]) + kr).astype(k_ref.dtype)


def mla_prep(rest, q_norm, kv_norm, wq1, wq2, wk, wv, cos_t, sin_t, bsz, seq, tm=512):
    tm = min(tm, seq)
    nl = seq // tm
    t = bsz * seq
    wr = rest.shape[1]
    row = lambda b, i: (b * nl + i, 0)
    const = lambda b, i: (0, 0)
    qk = MLA_HEADS * MLA_HP
    return pl.pallas_call(
        _mla_prep_kernel,
        grid=(bsz, nl),
        in_specs=[pl.BlockSpec((tm, wr), row),
                  pl.BlockSpec((1, MLA_Q_RANK), const),
                  pl.BlockSpec((1, MLA_KV_RANK), const),
                  pl.BlockSpec((MLA_Q_RANK, qk), const),
                  pl.BlockSpec((MLA_Q_RANK, qk), const),
                  pl.BlockSpec((MLA_KV_RANK, qk), const),
                  pl.BlockSpec((MLA_KV_RANK, MLA_HEADS * MLA_V), const),
                  pl.BlockSpec((tm, MLA_HP), lambda b, i: (i, 0)),
                  pl.BlockSpec((tm, MLA_HP), lambda b, i: (i, 0))],
        out_specs=[pl.BlockSpec((tm, qk), row), pl.BlockSpec((tm, qk), row),
                   pl.BlockSpec((tm, MLA_HEADS * MLA_V), row)],
        out_shape=[jax.ShapeDtypeStruct((t, qk), BF16), jax.ShapeDtypeStruct((t, qk), BF16),
                   jax.ShapeDtypeStruct((t, MLA_HEADS * MLA_V), BF16)],
        compiler_params=_cparams("parallel", "parallel"),
        name="mla_prep",
    )(rest, q_norm.reshape(1, -1), kv_norm.reshape(1, -1), wq1, wq2, wk, wv, cos_t, sin_t)


def _mla_attn_kernel(q_ref, k_ref, v_ref, o_ref, *, kb):
    nk = k_ref.shape[0] // kb
    outs = []
    for h in range(2):
        sl = slice(h * MLA_HP, (h + 1) * MLA_HP)
        q = q_ref[:, sl]
        m = den = acc = None
        for j in range(nk):
            ks = slice(j * kb, (j + 1) * kb)
            s = _dot_nt(q, k_ref[ks, sl])
            mj = jnp.max(s, axis=-1, keepdims=True)
            if j == 0:
                m = mj
                p = jnp.exp2(s - m)
                den = jnp.sum(p, axis=-1, keepdims=True)
                acc = _dot(p.astype(BF16), v_ref[ks, :])
            else:
                m_new = jnp.maximum(m, mj)
                alpha = jnp.exp2(m - m_new)
                p = jnp.exp2(s - m_new)
                den = alpha * den + jnp.sum(p, axis=-1, keepdims=True)
                acc = alpha * acc + _dot(p.astype(BF16), v_ref[ks, :])
                m = m_new
        outs.append(acc / den)
    lane = lax.broadcasted_iota(jnp.int32, outs[0].shape, 1)
    o_ref[...] = jnp.where(lane < MLA_V, outs[0], outs[1]).astype(o_ref.dtype)


def mla_attn(q, k, v, bsz, seq, tq=512, kb=256):
    tq = min(tq, seq)
    nq = seq // tq
    return pl.pallas_call(
        functools.partial(_mla_attn_kernel, kb=min(kb, seq)),
        grid=(bsz, MLA_HEADS // 2, nq),
        in_specs=[pl.BlockSpec((tq, 2 * MLA_HP), lambda b, p, i: (b * nq + i, p)),
                  pl.BlockSpec((seq, 2 * MLA_HP), lambda b, p, i: (b, p)),
                  pl.BlockSpec((seq, 2 * MLA_V), lambda b, p, i: (b, p))],
        out_specs=pl.BlockSpec((tq, 2 * MLA_V), lambda b, p, i: (b * nq + i, p)),
        out_shape=jax.ShapeDtypeStruct((bsz * seq, MLA_HEADS * MLA_V), BF16),
        compiler_params=_cparams("parallel", "parallel", "parallel"),
        name="mla_attn",
    )(q, k, v)


def _cross_router_kernel(x_ref, kv_ref, gc_ref, wq_ref, wo_ref, gf_ref, wr_ref, xo_ref, hn_ref, aff_ref, *, nsplit):
    rows = x_ref.shape[0] // nsplit
    for i in range(nsplit):
        sl = slice(i * rows, (i + 1) * rows)
        _cross_router_rows(x_ref.at[sl], kv_ref, gc_ref, wq_ref, wo_ref, gf_ref, wr_ref,
                           xo_ref.at[sl], hn_ref.at[sl], aff_ref.at[sl])


def _cross_router_rows(x_ref, kv_ref, gc_ref, wq_ref, wo_ref, gf_ref, wr_ref, xo_ref, hn_ref, aff_ref):
    x = x_ref[...]
    h = _rms(x, gc_ref[...]).astype(BF16)
    q = (_dot(h, wq_ref[...]) * (CA_HEAD_DIM ** -0.5 * LOG2E)).astype(BF16)
    hd = CA_HEADS * CA_HEAD_DIM
    outs = []
    for a in range(CA_HEADS):
        sl = slice(a * CA_HEAD_DIM, (a + 1) * CA_HEAD_DIM)
        s = _dot_nt(q[:, sl], kv_ref[:, sl])
        m = jnp.max(s, axis=-1, keepdims=True)
        p = jnp.exp2(s - m)
        den = jnp.sum(p, axis=-1, keepdims=True)
        outs.append((_dot(p.astype(BF16), kv_ref[:, hd + a * CA_HEAD_DIM:hd + (a + 1) * CA_HEAD_DIM]) / den))
    o = jnp.concatenate(outs, axis=-1).astype(BF16)
    xn = x + _dot(o, wo_ref[...])
    xo_ref[...] = xn
    hf = _rms(xn, gf_ref[...])
    hb = hf.astype(BF16)
    hn_ref[...] = hb
    lo = (hf - hb.astype(F32)).astype(BF16)
    hw = _dot(hb, wr_ref[...])
    logits = hw[:, 0:LANES] + (_dot(lo, wr_ref[:, 0:LANES]) + hw[:, LANES:])
    lane = lax.broadcasted_iota(jnp.int32, logits.shape, 1)
    logits = jnp.where(lane < N_EXPERTS, logits, NEG)
    m = jnp.max(logits, axis=-1, keepdims=True)
    e = jnp.exp(logits - m)
    aff = e / jnp.sum(e, axis=-1, keepdims=True)
    aff_ref[...] = aff[:, 0:N_EXPERTS]


def cross_router(x2d, kv, ln_cross, w_q, w_o, ln_ffn, w_router, bsz, seq, tq=1024, sub=256):
    tq = min(tq, seq)
    nsplit = max(tq // sub, 1)
    nq = seq // tq
    t = bsz * seq
    mem = kv.shape[0] // bsz
    hd = CA_HEADS * CA_HEAD_DIM
    wr = jnp.pad(w_router.astype(F32), ((0, 0), (0, LANES - N_EXPERTS)))
    wr_hi = wr.astype(BF16)
    wr2 = jnp.concatenate([wr_hi, (wr - wr_hi.astype(F32)).astype(BF16)], axis=1)
    row = lambda b, i: (b * nq + i, 0)
    const = lambda b, i: (0, 0)
    return pl.pallas_call(
        functools.partial(_cross_router_kernel, nsplit=nsplit),
        grid=(bsz, nq),
        in_specs=[pl.BlockSpec((tq, D_MODEL), row),
                  pl.BlockSpec((mem, 2 * hd), lambda b, i: (b, 0)),
                  pl.BlockSpec((1, D_MODEL), const),
                  pl.BlockSpec((D_MODEL, hd), const),
                  pl.BlockSpec((hd, D_MODEL), const),
                  pl.BlockSpec((1, D_MODEL), const),
                  pl.BlockSpec((D_MODEL, 2 * LANES), const)],
        out_specs=[pl.BlockSpec((tq, D_MODEL), row), pl.BlockSpec((tq, D_MODEL), row),
                   pl.BlockSpec((tq, N_EXPERTS), row)],
        out_shape=[jax.ShapeDtypeStruct((t, D_MODEL), F32), jax.ShapeDtypeStruct((t, D_MODEL), BF16),
                   jax.ShapeDtypeStruct((t, N_EXPERTS), F32)],
        compiler_params=_cparams("parallel", "parallel"),
        name="cross_router",
    )(x2d, kv, ln_cross.reshape(1, -1), w_q, w_o, ln_ffn.reshape(1, -1), wr2)


def _expert_ffn_kernel(x_ref, g_ref, wg_ref, wu_ref, wd_ref, o_ref, hid_ref, *, tf):
    x = x_ref[0]
    for c0 in range(0, D_EXPERT, tf):
        a = _dot(x, wg_ref[0, 0, :, c0:c0 + tf])
        u = _dot(x, wu_ref[0, 0, :, c0:c0 + tf])
        hid_ref[:, c0:c0 + tf] = (a * jax.nn.sigmoid(a) * u).astype(BF16)
    o_ref[0] = (_dot(hid_ref[...], wd_ref[0, 0]) * g_ref[0]).astype(o_ref.dtype)


def expert_ffn(xe, gate, w_gate, w_up, w_down, layer, tm=1024, tf=512):
    e, cap, _ = xe.shape
    tm = min(tm, cap)
    return pl.pallas_call(
        functools.partial(_expert_ffn_kernel, tf=tf),
        grid=(e, cap // tm),
        in_specs=[pl.BlockSpec((1, tm, D_MODEL), lambda e, m: (e, m, 0)),
                  pl.BlockSpec((1, tm, 1), lambda e, m: (e, m, 0)),
                  pl.BlockSpec((1, 1, D_MODEL, D_EXPERT), lambda e, m: (layer, e, 0, 0)),
                  pl.BlockSpec((1, 1, D_MODEL, D_EXPERT), lambda e, m: (layer, e, 0, 0)),
                  pl.BlockSpec((1, 1, D_EXPERT, D_MODEL), lambda e, m: (layer, e, 0, 0))],
        out_specs=pl.BlockSpec((1, tm, D_MODEL), lambda e, m: (e, m, 0)),
        out_shape=jax.ShapeDtypeStruct((e, cap, D_MODEL), BF16),
        scratch_shapes=[pltpu.VMEM((tm, D_EXPERT), BF16)],
        compiler_params=_cparams("parallel", "arbitrary"),
        name="expert_ffn",
    )(xe, gate, w_gate, w_up, w_down)


ROUTE_GROUP = SUBLANES


def _route_thr_kernel(a_ref, thr_ref, *, cap):
    bits = pltpu.bitcast(a_ref[...], jnp.int32)

    def body(i, lo):
        cand = lo | jnp.left_shift(jnp.int32(1), 30 - i)
        cnt = jnp.sum(jnp.where(bits >= cand, 1.0, 0.0), axis=1, keepdims=True)
        return jnp.where(cnt >= cap, cand, lo)

    thr_ref[...] = lax.fori_loop(0, 31, body, jnp.zeros(thr_ref.shape, jnp.int32))


def _prefix_rows(m, upper, lower):
    mb = m.astype(BF16)
    incl = _dot(mb, upper)
    tot = jnp.broadcast_to(incl[:, LANES - 1:LANES], incl.shape).astype(BF16)
    return incl - m + _dot(lower, tot)


def _route_mask_kernel(thr_ref, a_ref, upper_ref, lower_ref, sel_ref, pos_ref, *, cap):
    e = pl.program_id(0)
    thr = thr_ref[e]
    bits = pltpu.bitcast(a_ref[0], jnp.int32)
    gt = jnp.where(bits > thr, 1.0, 0.0)
    eq = jnp.where(bits == thr, 1.0, 0.0)
    need = cap - jnp.sum(jnp.sum(gt, axis=1, keepdims=True), axis=0, keepdims=True)
    eq_rank = _prefix_rows(eq, upper_ref[...], lower_ref[...])
    sel = gt + jnp.where(eq_rank < need, eq, 0.0)
    sel_ref[0] = sel
    pos_ref[0] = _prefix_rows(sel, upper_ref[...], lower_ref[...]).astype(jnp.int32)


def _route_compact_kernel(glo_ref, ghi_ref, a_ref, pos_ref, idx_ref, gate_ref):
    e = pl.program_id(0)
    nc = idx_ref.shape[1]
    gw = ROUTE_GROUP * LANES
    slot0 = lax.broadcasted_iota(jnp.int32, (LANES, LANES), 0)
    lane = lax.broadcasted_iota(jnp.int32, (1, gw), 1)
    zeros = jnp.zeros((2 * SUBLANES - 5, gw), F32)

    ngroups = pos_ref.shape[1] // ROUTE_GROUP
    never = jnp.int32(1 << 30)

    def terms(g, slot):
        r0 = pl.multiple_of(g * ROUTE_GROUP, ROUTE_GROUP)
        pos = pos_ref[0, pl.ds(r0, ROUTE_GROUP), :]
        aff = a_ref[0, pl.ds(r0, ROUTE_GROUP), :]
        hit = jnp.concatenate(
            [jnp.where(pos[j:j + 1] == slot, 1.0, 0.0) for j in range(ROUTE_GROUP)],
            axis=1).astype(BF16)
        arow = jnp.concatenate([aff[j:j + 1] for j in range(ROUTE_GROUP)], axis=1)
        tok = lane + g * gw
        g0 = arow.astype(BF16).astype(F32)
        r1 = arow - g0
        g1 = r1.astype(BF16).astype(F32)
        g2 = r1 - g1
        lhs = jnp.concatenate([(tok >> 8).astype(F32), (tok & 255).astype(F32), g0, g1, g2, zeros], axis=0)
        return lhs.astype(BF16), hit

    def token_id(acc):
        return (acc[0:1] * 256.0 + acc[1:2]).astype(jnp.int32)

    def affinity(acc):
        return (acc[2:3] + acc[3:4]) + acc[4:5]

    def first_two(c, carry):
        slot = slot0 + c * LANES
        g_first = glo_ref[e * nc + c]
        g_second = jnp.minimum(g_first + 1, ngroups - 1)
        lhs_a, hit_a = terms(g_first, slot)
        lhs_b, hit_b = terms(g_second, slot + jnp.where(g_first + 1 <= ghi_ref[e * nc + c], 0, never))
        acc = _dot_nt(jnp.concatenate([lhs_a, lhs_b], axis=1), jnp.concatenate([hit_a, hit_b], axis=1))
        idx_ref[0, pl.ds(c, 1), :] = token_id(acc)
        gate_ref[0, pl.ds(c, 1), :] = affinity(acc)
        return carry

    lax.fori_loop(0, nc, first_two, 0, unroll=4)

    def further(c, carry):
        g_first = glo_ref[e * nc + c]
        g_last = ghi_ref[e * nc + c]

        @pl.when(g_last >= g_first + 2)
        def _():
            slot = slot0 + c * LANES

            def group(g, acc):
                lhs, hit = terms(g, slot)
                return acc + _dot_nt(lhs, hit)

            acc = lax.fori_loop(g_first + 2, g_last + 1, group, jnp.zeros((2 * SUBLANES, LANES), F32))
            idx_ref[0, pl.ds(c, 1), :] += token_id(acc)
            gate_ref[0, pl.ds(c, 1), :] += affinity(acc)

        return carry

    lax.fori_loop(0, nc, further, 0)


def route_tokens(aff, cap):
    t, ne = aff.shape
    rows = t // LANES
    nc = cap // LANES
    aff_t = aff.T
    thr = pl.pallas_call(
        functools.partial(_route_thr_kernel, cap=cap),
        out_shape=jax.ShapeDtypeStruct((ne, 1), jnp.int32),
        compiler_params=pltpu.CompilerParams(vmem_limit_bytes=V7X_VMEM_LIMIT_BYTES),
        name="route_threshold",
    )(aff_t)
    aff3 = aff_t.reshape(ne, rows, LANES)
    ii = jnp.arange(LANES)
    upper = (ii[:, None] <= ii[None, :]).astype(BF16)
    rr = jnp.arange(rows)
    lower = (rr[None, :] < rr[:, None]).astype(BF16)
    blk = pl.BlockSpec((1, rows, LANES), lambda e: (e, 0, 0))
    sel, pos = pl.pallas_call(
        functools.partial(_route_mask_kernel, cap=cap),
        grid=(ne,),
        in_specs=[pl.BlockSpec(memory_space=pltpu.SMEM), blk,
                  pl.BlockSpec((LANES, LANES), lambda e: (0, 0)),
                  pl.BlockSpec((rows, rows), lambda e: (0, 0))],
        out_specs=[blk, blk],
        out_shape=[jax.ShapeDtypeStruct((ne, rows, LANES), F32), jax.ShapeDtypeStruct((ne, rows, LANES), jnp.int32)],
        compiler_params=_cparams("parallel"),
        name="route_mask",
    )(thr.reshape(ne), aff3, upper, lower)
    gsz = ROUTE_GROUP
    first = pos[:, ::gsz, 0]
    starts = jnp.arange(nc, dtype=jnp.int32) * LANES
    glo = jnp.sum(first[:, None, :] <= starts[None, :, None], axis=-1, dtype=jnp.int32) - 1
    ghi = jnp.sum(first[:, None, :] <= (starts + (LANES - 1))[None, :, None], axis=-1, dtype=jnp.int32) - 1
    posm = jnp.where(sel > 0, pos, -1)
    out_blk = pl.BlockSpec((1, nc, LANES), lambda e, *_: (e, 0, 0))
    grid_spec = pltpu.PrefetchScalarGridSpec(
        num_scalar_prefetch=2,
        grid=(ne,),
        in_specs=[pl.BlockSpec((1, rows, LANES), lambda e, *_: (e, 0, 0))] * 2,
        out_specs=[out_blk, out_blk],
    )
    idx, gate = pl.pallas_call(
        _route_compact_kernel,
        grid_spec=grid_spec,
        out_shape=[jax.ShapeDtypeStruct((ne, nc, LANES), jnp.int32), jax.ShapeDtypeStruct((ne, nc, LANES), F32)],
        compiler_params=_cparams("parallel"),
        name="route_compact",
    )(glo.reshape(-1), ghi.reshape(-1), aff3, posm)
    return idx.reshape(ne, cap), gate.reshape(ne, cap), pos.reshape(ne, t), posm.reshape(ne, t)


COMBINE_SUB = 512
COMBINE_ROWS = 96
COMBINE_XROWS = 128
COMBINE_ALIGN = 16


def _combine_kernel(wst_ref, nex_ref, x_ref, pos_ref, rel_ref, spread_ref, rowin_ref, g_ref, ye_hbm, o_ref,
                    ybuf, sem, xbuf, xsem, *, nsub, cap, final_norm):
    b = pl.program_id(0)
    nb = pl.num_programs(0)
    ne = N_EXPERTS
    w = COMBINE_ROWS
    xw = COMBINE_XROWS
    sub = COMBINE_SUB
    slot = b % 2
    lane = lax.broadcasted_iota(jnp.int32, (1, LANES), 1)
    row_in_window = rowin_ref[...]

    def window(bb, sl, u, e):
        st = pl.multiple_of(wst_ref[(bb * nsub + u) * ne + e], COMBINE_ALIGN)
        return pltpu.make_async_copy(ye_hbm.at[e, pl.ds(st, w), :], ybuf.at[sl, u, pl.ds(e * w, w), :],
                                     sem.at[sl, u, e])

    def start_all(bb, sl):
        for u in range(nsub):
            for e in range(ne):
                window(bb, sl, u, e).start()

    @pl.when(b == 0)
    def _():
        start_all(b, slot)

    @pl.when(b + 1 < nb)
    def _():
        start_all(b + 1, 1 - slot)

    for u in range(nsub):
        rows = slice(u * sub, (u + 1) * sub)
        base = (b * nsub + u) * ne
        spread = _dot(rel_ref[rows, :], spread_ref[...])
        onehot = jnp.where(spread == row_in_window, 1.0, 0.0).astype(BF16)
        for e in range(ne):
            window(b, slot, u, e).wait()
        o_ref[rows, :] = x_ref[rows, :] + _dot(onehot, ybuf[slot, u])

        for e in range(ne):
            def extra(k, carry, e=e, rows=rows, base=base):
                first = wst_ref[base + e] + w + (k - 1) * xw
                st = pl.multiple_of(jnp.minimum(first, cap - xw), COMBINE_ALIGN)
                cp = pltpu.make_async_copy(ye_hbm.at[e, pl.ds(st, xw), :], xbuf, xsem.at[0])
                cp.start()
                cp.wait()
                col = jnp.broadcast_to(pos_ref[rows, e:e + 1], (sub, xw))
                hit = jnp.logical_and(col - st == lane, col >= first)
                o_ref[rows, :] += _dot(jnp.where(hit, 1.0, 0.0).astype(BF16), xbuf[...])
                return carry

            lax.fori_loop(1, nex_ref[base + e] + 1, extra, 0)

        if final_norm:
            o_ref[rows, :] = _rms(o_ref[rows, :], g_ref[...])


def moe_combine(x2d, ye, pos, posm, norm_gain=None, nsub=1):
    t = x2d.shape[0]
    ne, cap = ye.shape[0], ye.shape[1]
    sub = COMBINE_SUB
    nsub = min(nsub, t // sub)
    tb = sub * nsub
    nb = t // tb
    w = COMBINE_ROWS
    lo = pos[:, ::sub]
    hi = jnp.concatenate([lo[:, 1:], jnp.full((ne, 1), cap, jnp.int32)], axis=1)
    wst = jnp.minimum((lo // COMBINE_ALIGN) * COMBINE_ALIGN, cap - COMBINE_XROWS)
    nex = jnp.maximum(hi - (wst + w) + (COMBINE_XROWS - 1), 0) // COMBINE_XROWS
    posm = posm.T
    rel = posm - jnp.repeat(wst.T, sub, axis=0)
    rel = jnp.where(posm >= 0, jnp.minimum(rel, 2 * LANES - 1), -1).astype(BF16)
    stacked = jnp.arange(ne * w, dtype=jnp.int32)
    spread = stacked[None, :] // w == jnp.arange(ne, dtype=jnp.int32)[:, None]
    rowin = (stacked % w).astype(F32).reshape(1, ne * w)
    gain = jnp.ones((1, D_MODEL), F32) if norm_gain is None else norm_gain.astype(F32).reshape(1, D_MODEL)
    grid_spec = pltpu.PrefetchScalarGridSpec(
        num_scalar_prefetch=2,
        grid=(nb,),
        in_specs=[pl.BlockSpec((tb, D_MODEL), lambda b, *_: (b, 0)),
                  pl.BlockSpec((tb, ne), lambda b, *_: (b, 0)),
                  pl.BlockSpec((tb, ne), lambda b, *_: (b, 0)),
                  pl.BlockSpec((ne, ne * w), lambda b, *_: (0, 0)),
                  pl.BlockSpec((1, ne * w), lambda b, *_: (0, 0)),
                  pl.BlockSpec((1, D_MODEL), lambda b, *_: (0, 0)),
                  pl.BlockSpec(memory_space=pl.ANY)],
        out_specs=pl.BlockSpec((tb, D_MODEL), lambda b, *_: (b, 0)),
        scratch_shapes=[pltpu.VMEM((2, nsub, ne * w, D_MODEL), BF16),
                        pltpu.SemaphoreType.DMA((2, nsub, ne)),
                        pltpu.VMEM((COMBINE_XROWS, D_MODEL), BF16),
                        pltpu.SemaphoreType.DMA((1,))],
    )
    return pl.pallas_call(
        functools.partial(_combine_kernel, nsub=nsub, cap=cap, final_norm=norm_gain is not None),
        grid_spec=grid_spec,
        out_shape=jax.ShapeDtypeStruct(x2d.shape, F32),
        compiler_params=_cparams("arbitrary"),
        name="moe_combine",
    )(wst.T.reshape(-1), nex.T.reshape(-1), x2d, posm, rel, spread.astype(BF16), rowin, gain, ye)


def mixer_ab(x2d, e, p, shared, bsz, seq):
    proj = norm_proj(x2d, p['ln_mix_l'], p['w_in_ab'][e], bsz, seq, [(0, AB_IN, BF16, False)])[0]
    vx, x0 = hyena_pre(proj, p['hy_conv_w'][e], p['hy_conv_b'][e], bsz, seq)
    kp, kq, kp2 = shared['hy_spec'][e]
    y_hy = hyena_conv(vx, x0, shared['dft_fwd'], shared['dft_inv'], kp, kq, kp2, p['hy_d'][e], bsz, seq)
    y_wa = window_attn(proj, p['attn_sink'][e], shared['wa_bias'], bsz, seq)
    return out_proj_ab(x2d, y_hy, y_wa, p['w_out_ab'][e])


def mixer_cd(x2d, o, p, shared, bsz, seq):
    u2d, rest = norm_proj(x2d, p['ln_mix_l'], shared['w_in_cd'][o], bsz, seq,
                          [(0, HALF, F32, False), (HALF, CD_PAD - HALF, BF16, False)])
    a5, bm, cm = shared['s5'][o]
    y_dirs = s5_scan(u2d.reshape(bsz, seq, HALF), a5, bm, cm, bsz, seq)
    wq1, wq2, wk, wv = shared['mla_w'][o]
    q, k, v = mla_prep(rest, p['mla_q_norm'][o], p['mla_kv_norm'][o], wq1, wq2, wk, wv,
                       shared['rope_cos'], shared['rope_sin'], bsz, seq)
    y_mla = mla_attn(q, k, v, bsz, seq)
    return out_proj_cd(x2d, u2d, y_dirs.reshape(2, bsz * seq, HALF), y_mla, p['s5_d'][o],
                       p['s5_glu_w'][o], p['s5_glu_b'][o], p['w_out_cd'][o], bsz, seq)


def ec_moe(x2d, hn, aff, w_gate, w_up, w_down, layer, norm_gain=None):
    t = x2d.shape[0]
    cap = EC_CAPACITY_FACTOR * t // N_EXPERTS
    idx, gate, pos, posm = route_tokens(aff, cap)
    xe = hn[idx]
    ye = expert_ffn(xe, gate[..., None], w_gate, w_up, w_down, layer)
    return moe_combine(x2d, ye, pos, posm, norm_gain)


def prepare_shared(p, seq):
    sh = {}
    sh['wa_bias'] = window_bias_mask(p['rel_bias'])
    sh['dft_fwd'], sh['dft_inv'] = dft_matrices(seq)
    sh['rope_cos'], sh['rope_sin'] = rope_tables(seq)
    specs = []
    for e in range(p['w_in_ab'].shape[0]):
        h = hyena_filters(seq, p['hy_filt_w1'][e], p['hy_filt_b1'][e], p['hy_filt_w2'][e], p['hy_filt_b2'][e],
                          p['hy_filt_w3'][e], p['hy_filt_freq'][e])
        h_fwd, h_bwd = h[:, :HY_WIDTH], h[:, HY_WIDTH:]
        k = jnp.concatenate([h_fwd, jnp.zeros_like(h_fwd[:1]), h_bwd[:0:-1]], axis=0)
        kf = kernel_spectrum(sh['dft_fwd'], k)
        k_r, k_s = kf[:seq], kf[seq:]
        specs.append((k_r, k_s.at[0].set(0.0), k_r.at[0].set(k_s[0])))
    sh['hy_spec'] = specs
    s5, mla_w, w_in_cd = [], [], []
    for o in range(p['w_in_cd'].shape[0]):
        s5.append(s5_discretise(p['s5_a_re'][o], p['s5_a_im'][o], p['s5_log_dt'][o], p['s5_b_re'][o],
                                p['s5_b_im'][o], p['s5_c_re'][o], p['s5_c_im'][o]))
        mla_w.append(mla_weights(p['mla_w_uq'][o], p['mla_w_ukv'][o]))
        w = p['w_in_cd'][o].astype(F32)
        o2 = HALF + MLA_Q_RANK + MLA_KV_RANK
        kr = w[:, o2:o2 + MLA_ROPE]
        half = MLA_ROPE // 2
        kr_rot = jnp.concatenate([-kr[:, half:], kr[:, :half]], axis=1)
        z64 = jnp.zeros((D_MODEL, MLA_NOPE), F32)
        z32 = jnp.zeros((D_MODEL, MLA_HP - MLA_NOPE - MLA_ROPE), F32)
        w_in_cd.append(jnp.concatenate([w[:, :o2], z64, kr, z32, z64, kr_rot, z32], axis=1).astype(BF16))
    sh['s5'], sh['mla_w'], sh['w_in_cd'] = s5, mla_w, w_in_cd
    return sh


def run_trunk(x, mem, p, shared):
    bsz, seq, _ = x.shape
    x2d = x.reshape(bsz * seq, D_MODEL)
    mem2d = mem.reshape(bsz * mem.shape[1], D_MODEL)
    for layer in range(DEPTH):
        pl_ = dict(p, ln_mix_l=p['ln_mix'][layer])
        if layer % 2 == 0:
            x2d = mixer_ab(x2d, layer // 2, pl_, shared, bsz, seq)
        else:
            x2d = mixer_cd(x2d, layer // 2, pl_, shared, bsz, seq)
        kv = norm_proj(mem2d, p['ln_mem'][layer], p['ca_w_kv'][layer], bsz, mem.shape[1],
                       [(0, 2 * CA_HEADS * CA_HEAD_DIM, BF16, False)])[0]
        x2d, hn, aff = cross_router(x2d, kv, p['ln_cross'][layer], p['ca_w_q'][layer], p['ca_w_o'][layer],
                                    p['ln_ffn'][layer], p['moe_w_router'][layer], bsz, seq)
        x2d = ec_moe(x2d, hn, aff, p['moe_w_gate'], p['moe_w_up'], p['moe_w_down'], layer,
                     norm_gain=p['ln_final'] if layer == DEPTH - 1 else None)
    return x2d.reshape(bsz, seq, D_MODEL)


_BF16_WEIGHTS = ('w_in_ab', 'w_out_ab', 'w_out_cd', 's5_glu_w', 'ca_w_q', 'ca_w_kv', 'ca_w_o',
                 'moe_w_gate', 'moe_w_up', 'moe_w_down')


def kernel(x_prompt, x_sample, mem_prompt, mem_sample, ln_mix, ln_cross, ln_mem, ln_ffn, ln_final, rel_bias, w_in_ab, w_out_ab, hy_conv_w, hy_conv_b, hy_filt_w1, hy_filt_b1, hy_filt_w2, hy_filt_b2, hy_filt_w3, hy_filt_freq, hy_d, attn_sink, w_in_cd, w_out_cd, s5_a_re, s5_a_im, s5_log_dt, s5_b_re, s5_b_im, s5_c_re, s5_c_im, s5_d, s5_glu_w, s5_glu_b, mla_q_norm, mla_w_uq, mla_kv_norm, mla_w_ukv, ca_w_q, ca_w_kv, ca_w_o, moe_w_router, moe_w_gate, moe_w_up, moe_w_down):
    p = dict(ln_mix=ln_mix, ln_cross=ln_cross, ln_mem=ln_mem, ln_ffn=ln_ffn, ln_final=ln_final,
             rel_bias=rel_bias, w_in_ab=w_in_ab, w_out_ab=w_out_ab, hy_conv_w=hy_conv_w,
             hy_conv_b=hy_conv_b, hy_filt_w1=hy_filt_w1, hy_filt_b1=hy_filt_b1,
             hy_filt_w2=hy_filt_w2, hy_filt_b2=hy_filt_b2, hy_filt_w3=hy_filt_w3,
             hy_filt_freq=hy_filt_freq, hy_d=hy_d, attn_sink=attn_sink, w_in_cd=w_in_cd,
             w_out_cd=w_out_cd, s5_a_re=s5_a_re, s5_a_im=s5_a_im, s5_log_dt=s5_log_dt,
             s5_b_re=s5_b_re, s5_b_im=s5_b_im, s5_c_re=s5_c_re, s5_c_im=s5_c_im, s5_d=s5_d,
             s5_glu_w=s5_glu_w, s5_glu_b=s5_glu_b, mla_q_norm=mla_q_norm, mla_w_uq=mla_w_uq,
             mla_kv_norm=mla_kv_norm, mla_w_ukv=mla_w_ukv, ca_w_q=ca_w_q, ca_w_kv=ca_w_kv,
             ca_w_o=ca_w_o, moe_w_router=moe_w_router, moe_w_gate=moe_w_gate,
             moe_w_up=moe_w_up, moe_w_down=moe_w_down)
    assert x_prompt.shape[1] == x_sample.shape[1]
    shared = prepare_shared(p, x_prompt.shape[1])
    for name in _BF16_WEIGHTS:
        p[name] = p[name].astype(BF16)
    y_prompt = run_trunk(x_prompt, mem_prompt, p, shared)
    y_sample = run_trunk(x_sample, mem_sample, p, shared)
    return (y_prompt, y_sample)
```

```python
import functools
import math

import jax
import jax.numpy as jnp
import numpy as np
from jax import lax
from jax.experimental import pallas as pl
from jax.experimental.pallas import tpu as pltpu

D_MODEL = 1024
DEPTH = 4
HALF = 512
HEAD_DIM = 64
EPS = 1e-6
NEG = -1e30

HY_WIDTH = HALF
HY_EMB = 33
HY_BANDS = (HY_EMB - 1) // 2
HY_FILT_HIDDEN = 64
HY_DECAY_TARGET = 1e-2
HY_FAST = 0.3
HY_SLOW = 1.5
HY_MIN_DECAY = math.log(HY_DECAY_TARGET) / HY_SLOW
HY_MAX_DECAY = math.log(HY_DECAY_TARGET) / HY_FAST
HY_SHIFT = 0.05

WA_HEADS = 8
WA_KV_HEADS = 2
WA_REP = 4
WA_WINDOW = 128
WA_BLOCK = 128
REL_BUCKETS = 32
REL_MAX_DIST = 128

S5_GROUP = 16
S5_GROUPS = 32
S5_STATE = 64

MLA_HEADS = 8
MLA_NOPE = 64
MLA_ROPE = 32
MLA_V = 64
MLA_Q_RANK = 256
MLA_KV_RANK = 128
ROPE_THETA = 10000.0

CA_HEADS = 4
CA_HEAD_DIM = 128

N_EXPERTS = 16
EC_CAPACITY_FACTOR = 2
D_EXPERT = 2048

AB_IN = 3 * HY_WIDTH + (WA_HEADS + 2 * WA_KV_HEADS) * HEAD_DIM
CD_PAD = 1152

V7X_VMEM_LIMIT_BYTES = 56 * 1024 * 1024
LANES = 128
SUBLANES = 8
LOG2E = math.log2(math.e)

BF16 = jnp.bfloat16
F32 = jnp.float32


def _cparams(*sem):
    return pltpu.CompilerParams(dimension_semantics=sem, vmem_limit_bytes=V7X_VMEM_LIMIT_BYTES)


def _dot(a, b):
    return jnp.dot(a, b, preferred_element_type=F32)


def _dot_nt(a, b):
    return lax.dot_general(a, b, (((1,), (1,)), ((), ())), preferred_element_type=F32)


def _rms(xf, g):
    return xf * lax.rsqrt(jnp.mean(xf * xf, axis=-1, keepdims=True) + EPS) * g


def _norm_proj_kernel(x_ref, g_ref, w_ref, *out_refs, splits):
    hn = _rms(x_ref[...].astype(F32), g_ref[...]).astype(BF16)
    for o_ref, (start, width) in zip(out_refs, splits):
        for c0 in range(0, width, 512):
            cw = min(512, width - c0)
            o_ref[:, c0:c0 + cw] = _dot(hn, w_ref[:, start + c0:start + c0 + cw]).astype(o_ref.dtype)


def norm_proj(x2d, gain, w, bsz, seq, outs, tm=512):
    tm = min(tm, seq)
    nl = seq // tm
    n = w.shape[1]
    out_shapes, out_specs, splits = [], [], []
    for start, width, dtype, time_major in outs:
        splits.append((start, width))
        if time_major:
            out_shapes.append(jax.ShapeDtypeStruct((seq, bsz * width), dtype))
            out_specs.append(pl.BlockSpec((tm, width), lambda b, i: (i, b)))
        else:
            out_shapes.append(jax.ShapeDtypeStruct((bsz * seq, width), dtype))
            out_specs.append(pl.BlockSpec((tm, width), lambda b, i, nl=nl: (b * nl + i, 0)))
    return pl.pallas_call(
        functools.partial(_norm_proj_kernel, splits=tuple(splits)),
        grid=(bsz, nl),
        in_specs=[pl.BlockSpec((tm, D_MODEL), lambda b, i, nl=nl: (b * nl + i, 0)),
                  pl.BlockSpec((1, D_MODEL), lambda b, i: (0, 0)),
                  pl.BlockSpec((D_MODEL, n), lambda b, i: (0, 0))],
        out_specs=out_specs,
        out_shape=out_shapes,
        compiler_params=_cparams("parallel", "parallel"),
        name="norm_proj",
    )(x2d, gain.reshape(1, D_MODEL), w)


def _hyena_filter_kernel(z_ref, w1_ref, b1_ref, w2_ref, b2_ref, w3_ref, fr_ref, win_ref, o_ref):
    hp = lax.Precision.HIGHEST
    fr = fr_ref[...]
    h = jnp.sin(fr * (jnp.dot(z_ref[...], w1_ref[...], precision=hp, preferred_element_type=F32) + b1_ref[...]))
    h = jnp.sin(fr * (jnp.dot(h, w2_ref[...], precision=hp, preferred_element_type=F32) + b2_ref[...]))
    h = jnp.dot(h, w3_ref[...], precision=hp, preferred_element_type=F32)
    o_ref[...] = h * win_ref[...]


def hyena_filters(seq, w1, b1, w2, b2, w3, freq):
    t = jnp.linspace(0.0, 1.0, seq, dtype=F32)[:, None]
    ang = 2.0 * math.pi * jnp.arange(seq, dtype=F32)[:, None] / seq
    bands = jnp.linspace(1e-4, HY_BANDS - 1, HY_BANDS, dtype=F32)[None, :]
    z = jnp.concatenate([t, jnp.cos(bands * ang), -jnp.sin(bands * ang)], axis=-1)
    zp = jnp.pad(z, ((0, 0), (0, HY_FILT_HIDDEN - HY_EMB)))
    w1p = jnp.pad(w1.astype(F32), ((0, HY_FILT_HIDDEN - HY_EMB), (0, 0)))
    deltas = jnp.abs(jnp.linspace(HY_MIN_DECAY, HY_MAX_DECAY, HY_WIDTH, dtype=F32))
    window = jnp.exp(-t * deltas[None, :]) + HY_SHIFT
    win2 = jnp.concatenate([window, window], axis=-1)
    tl = min(512, seq)
    hh = HY_FILT_HIDDEN
    full = lambda r, c: pl.BlockSpec((r, c), lambda i: (0, 0))
    return pl.pallas_call(
        _hyena_filter_kernel,
        grid=(seq // tl,),
        in_specs=[pl.BlockSpec((tl, hh), lambda i: (i, 0)), full(hh, hh), full(1, hh), full(hh, hh), full(1, hh),
                  full(hh, 2 * HY_WIDTH), full(1, hh), pl.BlockSpec((tl, 2 * HY_WIDTH), lambda i: (i, 0))],
        out_specs=pl.BlockSpec((tl, 2 * HY_WIDTH), lambda i: (i, 0)),
        out_shape=jax.ShapeDtypeStruct((seq, 2 * HY_WIDTH), F32),
        compiler_params=_cparams("parallel"),
        name="hyena_filter",
    )(zp, w1p, b1.reshape(1, hh), w2, b2.reshape(1, hh), w3, freq.reshape(1, hh), win2)


def dft_matrices(seq):
    n = 2 * seq
    r = jnp.arange(seq, dtype=jnp.int32)[:, None]
    t = jnp.arange(n, dtype=jnp.int32)[None, :]
    ang = ((r * t) % n).astype(F32) * (2.0 * math.pi / n)
    c, s = jnp.cos(ang), jnp.sin(ang)
    nyq = jnp.where(t % 2 == 0, 1.0, -1.0).astype(F32)
    fwd = jnp.concatenate([c, jnp.where(r == 0, nyq, -s)], axis=0)
    ct = c[:, :seq].T
    st = s[:, :seq].T
    r_row = r.T
    inv_r = jnp.where(r_row == 0, 1.0, 2.0 * ct) / n
    inv_s = jnp.where(r_row == 0, nyq[:, :seq].T, -2.0 * st) / n
    inv = jnp.concatenate([inv_r, inv_s], axis=1)
    return fwd.astype(BF16), inv.astype(BF16)


def _kernel_dft_kernel(a_ref, khi_ref, klo_ref, o_ref):
    o_ref[...] = _dot(a_ref[...], khi_ref[...]) + _dot(a_ref[...], klo_ref[...])


def kernel_spectrum(fwd, k):
    n = fwd.shape[0]
    khi = k.astype(BF16)
    klo = (k - khi.astype(F32)).astype(BF16)
    tf = min(256, n)
    return pl.pallas_call(
        _kernel_dft_kernel,
        grid=(n // tf,),
        in_specs=[pl.BlockSpec((tf, n), lambda i: (i, 0)),
                  pl.BlockSpec((n, HY_WIDTH), lambda i: (0, 0)),
                  pl.BlockSpec((n, HY_WIDTH), lambda i: (0, 0))],
        out_specs=pl.BlockSpec((tf, HY_WIDTH), lambda i: (i, 0)),
        out_shape=jax.ShapeDtypeStruct((n, HY_WIDTH), F32),
        compiler_params=_cparams("parallel"),
        name="hyena_kernel_dft",
    )(fwd, khi, klo)


def _shift_down(u):
    rows = lax.broadcasted_iota(jnp.int32, u.shape, 0)
    return jnp.where(rows == 0, 0.0, pltpu.roll(u, 1, 0))


def _shift_up(u):
    n = u.shape[0]
    rows = lax.broadcasted_iota(jnp.int32, u.shape, 0)
    return jnp.where(rows == n - 1, 0.0, pltpu.roll(u, n - 1, 0))


def _hyena_pre_kernel(u_ref, w_ref, b_ref, vx_ref, x0_ref):
    def conv(c0):
        u = u_ref[:, c0:c0 + LANES].astype(F32)
        w = w_ref[:, c0:c0 + LANES]
        return _shift_down(u) * w[0:1] + u * w[1:2] + _shift_up(u) * w[2:3] + b_ref[:, c0:c0 + LANES]

    for c in range(0, HY_WIDTH, LANES):
        x0_ref[:, c:c + LANES] = conv(c).astype(x0_ref.dtype)
        vx_ref[:, c:c + LANES] = (conv(2 * HY_WIDTH + c) * conv(HY_WIDTH + c)).astype(vx_ref.dtype)


def hyena_pre(proj, conv_w, conv_b, bsz, seq):
    w3 = 3 * HY_WIDTH
    out = jax.ShapeDtypeStruct((bsz * seq, HY_WIDTH), BF16)
    return pl.pallas_call(
        _hyena_pre_kernel,
        grid=(bsz,),
        in_specs=[pl.BlockSpec((seq, w3), lambda b: (b, 0)),
                  pl.BlockSpec((3, w3), lambda b: (0, 0)),
                  pl.BlockSpec((1, w3), lambda b: (0, 0))],
        out_specs=[pl.BlockSpec((seq, HY_WIDTH), lambda b: (b, 0))] * 2,
        out_shape=[out, out],
        compiler_params=_cparams("parallel"),
        name="hyena_pre",
    )(proj, conv_w, conv_b.reshape(1, w3))


def _hyena_conv_kernel(vx_ref, x0_ref, ar_ref, as_ref, inv_ref, kp_ref, kq_ref, kp2_ref, d_ref,
                       o_ref, z_ref, *, nf, tf):
    step = pl.program_id(1)
    half = nf * tf

    @pl.when(step < nf)
    def _():
        vx = vx_ref[...]
        r = _dot(ar_ref[...], vx)
        s = _dot(as_ref[...], vx)
        kq = kq_ref[...]
        f0 = pl.multiple_of(step * tf, tf)
        z_ref[pl.ds(f0, tf), :] = (r * kp_ref[...] - s * kq).astype(BF16)
        z_ref[pl.ds(half + f0, tf), :] = (r * kq + s * kp2_ref[...]).astype(BF16)

    @pl.when(step >= nf)
    def _():
        t0 = pl.multiple_of((step - nf) * tf, tf)
        y = _dot(inv_ref[...], z_ref[...]) + vx_ref[pl.ds(t0, tf), :].astype(F32) * d_ref[...]
        o_ref[pl.ds(t0, tf), :] = (y * x0_ref[pl.ds(t0, tf), :].astype(F32)).astype(o_ref.dtype)


def hyena_conv(vx, x0, fwd, inv, kp, kq, kp2, d, bsz, seq):
    tf = min(512, seq)
    nf = seq // tf
    w = HY_WIDTH
    fwd_tile = lambda b, s: (jnp.minimum(s, nf - 1), 0)
    return pl.pallas_call(
        functools.partial(_hyena_conv_kernel, nf=nf, tf=tf),
        grid=(bsz, 2 * nf),
        in_specs=[pl.BlockSpec((seq, w), lambda b, s: (b, 0)),
                  pl.BlockSpec((seq, w), lambda b, s: (b, 0)),
                  pl.BlockSpec((tf, seq), fwd_tile),
                  pl.BlockSpec((tf, seq), lambda b, s: (nf + jnp.minimum(s, nf - 1), 0)),
                  pl.BlockSpec((tf, 2 * seq), lambda b, s: (jnp.maximum(s - nf, 0), 0)),
                  pl.BlockSpec((tf, w), fwd_tile),
                  pl.BlockSpec((tf, w), fwd_tile),
                  pl.BlockSpec((tf, w), fwd_tile),
                  pl.BlockSpec((1, w), lambda b, s: (0, 0))],
        out_specs=pl.BlockSpec((seq, w), lambda b, s: (b, 0)),
        out_shape=jax.ShapeDtypeStruct((bsz * seq, w), BF16),
        scratch_shapes=[pltpu.VMEM((2 * seq, w), BF16)],
        compiler_params=_cparams("parallel", "arbitrary"),
        name="hyena_conv",
    )(vx, x0, fwd, fwd, inv, kp, kq, kp2, d.reshape(1, w))


def _rel_bucket(rel):
    nb = REL_BUCKETS // 2
    max_exact = nb // 2
    ret = (rel > 0).astype(jnp.int32) * nb
    n = jnp.abs(rel)
    nf = jnp.maximum(n, 1).astype(F32)
    large = max_exact + (jnp.log(nf / max_exact) / math.log(REL_MAX_DIST / max_exact)
                         * (nb - max_exact)).astype(jnp.int32)
    large = jnp.minimum(large, nb - 1)
    return ret + jnp.where(n < max_exact, n, large)


def window_bias_mask(rel_bias):
    j = jnp.arange(WA_BLOCK, dtype=jnp.int32)[:, None]
    s = jnp.arange(3 * WA_BLOCK, dtype=jnp.int32)[None, :]
    rel = (s - WA_BLOCK) - j
    bias = jnp.transpose(rel_bias.astype(F32)[_rel_bucket(rel)], (2, 0, 1))
    band = jnp.abs(rel) <= WA_WINDOW
    return jnp.where(band[None], bias, NEG)


def _window_attn_kernel(sink_ref, q_ref, k_ref, v_ref, bias_ref, o_ref):
    nb = q_ref.shape[0] // WA_BLOCK
    scale = HEAD_DIM ** -0.5 * LOG2E
    rq = WA_REP * WA_BLOCK
    col = lax.broadcasted_iota(jnp.int32, (rq, 3 * WA_BLOCK), 1)
    head_of_row = lax.broadcasted_iota(jnp.int32, (rq, 1), 0) // WA_BLOCK

    def block(i, carry):
        ip = jnp.maximum(i - 1, 0)
        inx = jnp.minimum(i + 1, nb - 1)
        rows = lambda j: pl.ds(pl.multiple_of(j * WA_BLOCK, WA_BLOCK), WA_BLOCK)
        qb = q_ref[rows(i), :]
        kslab = jnp.concatenate([k_ref[rows(ip), :], k_ref[rows(i), :], k_ref[rows(inx), :]], axis=0)
        vslab = jnp.concatenate([v_ref[rows(ip), :], v_ref[rows(i), :], v_ref[rows(inx), :]], axis=0)
        lo = jnp.where(i > 0, 0, WA_BLOCK)
        hi = jnp.where(i < nb - 1, 3 * WA_BLOCK, 2 * WA_BLOCK)
        valid = jnp.logical_and(col >= lo, col < hi)
        outs = []
        for g in range(WA_KV_HEADS):
            heads = range(g * WA_REP, (g + 1) * WA_REP)
            q4 = jnp.concatenate([qb[:, h * HEAD_DIM:(h + 1) * HEAD_DIM] for h in heads], axis=0)
            kg = kslab[:, g * HEAD_DIM:(g + 1) * HEAD_DIM]
            vg = vslab[:, g * HEAD_DIM:(g + 1) * HEAD_DIM]
            s = _dot_nt(q4, kg) * scale + bias_ref[g]
            s = jnp.where(valid, s, NEG)
            sk = jnp.zeros((rq, 1), F32)
            for r, h in enumerate(heads):
                sk = jnp.where(head_of_row == r, sink_ref[h], sk)
            m = jnp.maximum(jnp.max(s, axis=-1, keepdims=True), sk)
            p = jnp.exp2(s - m)
            den = jnp.sum(p, axis=-1, keepdims=True) + jnp.exp2(sk - m)
            o4 = _dot(p.astype(BF16), vg) / den
            outs += [o4[r * WA_BLOCK:(r + 1) * WA_BLOCK] for r in range(WA_REP)]
        o_ref[rows(i), :] = jnp.concatenate(outs, axis=-1).astype(o_ref.dtype)
        return carry

    lax.fori_loop(0, nb, block, 0, unroll=2)


def window_attn(proj, sink, bias_mask, bsz, seq):
    hq = WA_HEADS * HEAD_DIM
    hkv = WA_KV_HEADS * HEAD_DIM
    q_blk = (3 * HY_WIDTH) // hq
    k_blk = (3 * HY_WIDTH + hq) // hkv
    bias2 = (bias_mask * LOG2E).reshape(WA_KV_HEADS, WA_REP * WA_BLOCK, 3 * WA_BLOCK)
    return pl.pallas_call(
        _window_attn_kernel,
        grid=(bsz,),
        in_specs=[pl.BlockSpec(memory_space=pltpu.SMEM),
                  pl.BlockSpec((seq, hq), lambda b: (b, q_blk)),
                  pl.BlockSpec((seq, hkv), lambda b: (b, k_blk)),
                  pl.BlockSpec((seq, hkv), lambda b: (b, k_blk + 1)),
                  pl.BlockSpec((WA_KV_HEADS, WA_REP * WA_BLOCK, 3 * WA_BLOCK), lambda b: (0, 0, 0))],
        out_specs=pl.BlockSpec((seq, hq), lambda b: (b, 0)),
        out_shape=jax.ShapeDtypeStruct((bsz * seq, hq), BF16),
        compiler_params=_cparams("parallel"),
        name="window_attn",
    )(sink.astype(F32) * LOG2E, proj, proj, proj, bias2)


def _out_proj_kernel(x_ref, a_ref, b_ref, w_ref, o_ref):
    acc = _dot(a_ref[...], w_ref[0:HALF, :]) + _dot(b_ref[...], w_ref[HALF:, :])
    o_ref[...] = x_ref[...] + acc


def out_proj_ab(x2d, y_a, y_b, w, tm=512):
    t = x2d.shape[0]
    tm = min(tm, t)
    return pl.pallas_call(
        _out_proj_kernel,
        grid=(t // tm,),
        in_specs=[pl.BlockSpec((tm, D_MODEL), lambda i: (i, 0)),
                  pl.BlockSpec((tm, HALF), lambda i: (i, 0)),
                  pl.BlockSpec((tm, HALF), lambda i: (i, 0)),
                  pl.BlockSpec((D_MODEL, D_MODEL), lambda i: (0, 0))],
        out_specs=pl.BlockSpec((tm, D_MODEL), lambda i: (i, 0)),
        out_shape=jax.ShapeDtypeStruct(x2d.shape, F32),
        compiler_params=_cparams("parallel"),
        name="out_proj_ab",
    )(x2d, y_a, y_b, w)


def s5_discretise(a_re, a_im, log_dt, b_re, b_im, c_re, c_im):
    lam = lax.complex(a_re.astype(F32), a_im.astype(F32))
    dt = jnp.exp(log_dt.astype(F32))[..., None]
    abar = jnp.exp(lam * dt)
    bmat = lax.complex(b_re.astype(F32), b_im.astype(F32))
    bbar = ((abar - 1.0) / lam)[..., None] * bmat
    cmat = lax.complex(c_re.astype(F32), c_im.astype(F32))
    nj, gl = S5_GROUPS // SUBLANES, SUBLANES
    eye = jnp.eye(gl, dtype=F32)
    a5 = jnp.stack([abar.real, abar.imag], axis=1).reshape(2, 2, nj, 1, gl * S5_STATE)
    a5 = jnp.transpose(a5, (0, 2, 1, 3, 4))
    a5 = jnp.broadcast_to(a5, (2, nj, 2, SUBLANES, gl * S5_STATE))

    def pack_b(x):
        x = x.reshape(2, nj, gl, S5_STATE, S5_GROUP)
        y = jnp.einsum('hg,djgpc->djhcgp', eye, x)
        return y.reshape(2, nj, gl * S5_GROUP, gl * S5_STATE)

    def pack_c(x):
        x = x.reshape(2, nj, gl, S5_GROUP, S5_STATE)
        y = jnp.einsum('hg,djgcp->djgphc', eye, x)
        return y.reshape(2, nj, gl * S5_STATE, gl * S5_GROUP)

    bm = jnp.concatenate([pack_b(bbar.real), pack_b(bbar.imag)], axis=-1).astype(BF16)
    cm = jnp.concatenate([pack_c(cmat.real), -pack_c(cmat.imag)], axis=-2).astype(BF16)
    return a5, bm, cm


def _s5_kernel(u_ref, a_ref, b_ref, c_ref, y_ref, buf_ref, h_ref, *, chunk):
    d = pl.program_id(1)
    nj = S5_GROUPS // SUBLANES
    sw = SUBLANES * S5_STATE
    rows = chunk * SUBLANES

    @pl.when(pl.program_id(2) == 0)
    def _():
        h_ref[...] = jnp.zeros_like(h_ref)

    u = pltpu.einshape("btc->(tb)c", u_ref[...]).astype(BF16)
    for j in range(nj):
        buf_ref[j] = _dot(u[:, j * LANES:(j + 1) * LANES], b_ref[0, j])

    for j0 in range(0, nj, 2):
        js = (j0, j0 + 1)

        def step(s, carry, js=js):
            t = jnp.where(d == 0, s, chunk - 1 - s)
            r0 = pl.multiple_of(t * SUBLANES, SUBLANES)
            new = []
            for n, j in enumerate(js):
                hr, hi = carry[2 * n], carry[2 * n + 1]
                ar = a_ref[0, j, 0]
                ai = a_ref[0, j, 1]
                br = buf_ref[j, pl.ds(r0, SUBLANES), 0:sw]
                bi = buf_ref[j, pl.ds(r0, SUBLANES), sw:2 * sw]
                nr = ar * hr - ai * hi + br
                ni = ar * hi + ai * hr + bi
                buf_ref[j, pl.ds(r0, SUBLANES), 0:sw] = nr
                buf_ref[j, pl.ds(r0, SUBLANES), sw:2 * sw] = ni
                new += [nr, ni]
            return tuple(new)

        init = tuple(h_ref[2 * j + k] for j in js for k in range(2))
        fin = lax.fori_loop(0, chunk, step, init, unroll=2)
        for n, j in enumerate(js):
            h_ref[2 * j] = fin[2 * n]
            h_ref[2 * j + 1] = fin[2 * n + 1]

    for j in range(nj):
        yj = _dot(buf_ref[j].astype(BF16), c_ref[0, j])
        y_ref[0, :, :, j * LANES:(j + 1) * LANES] = pltpu.einshape(
            "(tb)c->btc", yj, b=SUBLANES).astype(y_ref.dtype)


def s5_scan(u3, a5, bm, cm, bsz, seq, chunk=64):
    chunk = min(chunk, seq)
    nc = seq // chunk
    nj = S5_GROUPS // SUBLANES
    sw = SUBLANES * S5_STATE

    def tchunk(d, i):
        return i + d * (nc - 1 - 2 * i)

    return pl.pallas_call(
        functools.partial(_s5_kernel, chunk=chunk),
        grid=(bsz // SUBLANES, 2, nc),
        in_specs=[pl.BlockSpec((SUBLANES, chunk, HALF), lambda b, d, i: (b, tchunk(d, i), 0)),
                  pl.BlockSpec((1, nj, 2, SUBLANES, sw), lambda b, d, i: (d, 0, 0, 0, 0)),
                  pl.BlockSpec((1, nj, LANES, 2 * sw), lambda b, d, i: (d, 0, 0, 0)),
                  pl.BlockSpec((1, nj, 2 * sw, LANES), lambda b, d, i: (d, 0, 0, 0))],
        out_specs=pl.BlockSpec((1, SUBLANES, chunk, HALF), lambda b, d, i: (d, b, tchunk(d, i), 0)),
        out_shape=jax.ShapeDtypeStruct((2, bsz, seq, HALF), F32),
        scratch_shapes=[pltpu.VMEM((nj, chunk * SUBLANES, 2 * sw), F32),
                        pltpu.VMEM((2 * nj, SUBLANES, sw), F32)],
        compiler_params=_cparams("parallel", "arbitrary", "arbitrary"),
        name="s5_scan",
    )(u3, a5, bm, cm)


def _gelu_tanh(x):
    return 0.5 * x * (1.0 + jnp.tanh(math.sqrt(2.0 / math.pi) * (x + 0.044715 * (x * x * x))))


def _out_proj_cd_kernel(x_ref, u_ref, yf_ref, yb_ref, mla_ref, d_ref, gw_ref, gb_ref, w_ref, o_ref):
    y = u_ref[...] * d_ref[...] + yf_ref[0] + yb_ref[0]
    g = _gelu_tanh(y)
    z = _dot(g.astype(BF16), gw_ref[...]) + gb_ref[...]
    y_s5 = g * jax.nn.sigmoid(z)
    acc = _dot(y_s5.astype(BF16), w_ref[0:HALF, :]) + _dot(mla_ref[...], w_ref[HALF:, :])
    o_ref[...] = x_ref[...] + acc


def out_proj_cd(x2d, u2d, y2, y_mla, d, glu_w, glu_b, w, bsz, seq, tm=512):
    tm = min(tm, seq)
    nl = seq // tm
    row = lambda b, i: (b * nl + i, 0)
    const = lambda b, i: (0, 0)
    return pl.pallas_call(
        _out_proj_cd_kernel,
        grid=(bsz, nl),
        in_specs=[pl.BlockSpec((tm, D_MODEL), row),
                  pl.BlockSpec((tm, HALF), row),
                  pl.BlockSpec((1, tm, HALF), lambda b, i: (0, b * nl + i, 0)),
                  pl.BlockSpec((1, tm, HALF), lambda b, i: (1, b * nl + i, 0)),
                  pl.BlockSpec((tm, HALF), row),
                  pl.BlockSpec((1, HALF), const),
                  pl.BlockSpec((HALF, HALF), const),
                  pl.BlockSpec((1, HALF), const),
                  pl.BlockSpec((D_MODEL, D_MODEL), const)],
        out_specs=pl.BlockSpec((tm, D_MODEL), row),
        out_shape=jax.ShapeDtypeStruct(x2d.shape, F32),
        compiler_params=_cparams("parallel", "parallel"),
        name="out_proj_cd",
    )(x2d, u2d, y2, y2, y_mla, d.reshape(1, HALF), glu_w, glu_b.reshape(1, HALF), w)


MLA_HP = 128


def mla_weights(w_uq, w_ukv):
    rq = w_uq.shape[0]
    wq = w_uq.astype(F32).reshape(rq, MLA_HEADS, MLA_NOPE + MLA_ROPE)
    half = MLA_ROPE // 2
    x1, x2 = wq[..., MLA_NOPE:MLA_NOPE + half], wq[..., MLA_NOPE + half:]
    zpad = jnp.zeros((rq, MLA_HEADS, MLA_HP - MLA_NOPE - MLA_ROPE), F32)
    wq1 = jnp.concatenate([wq, zpad], axis=-1).reshape(rq, MLA_HEADS * MLA_HP)
    wq2 = jnp.concatenate([jnp.zeros((rq, MLA_HEADS, MLA_NOPE), F32), -x2, x1, zpad], axis=-1)
    wq2 = wq2.reshape(rq, MLA_HEADS * MLA_HP)
    rk = w_ukv.shape[0]
    wkv = w_ukv.astype(F32).reshape(rk, MLA_HEADS, MLA_NOPE + MLA_V)
    wk = jnp.concatenate([wkv[..., :MLA_NOPE], jnp.zeros((rk, MLA_HEADS, MLA_HP - MLA_NOPE), F32)], axis=-1)
    wk = wk.reshape(rk, MLA_HEADS * MLA_HP)
    wv = wkv[..., MLA_NOPE:].reshape(rk, MLA_HEADS * MLA_V)
    return wq1.astype(BF16), wq2.astype(BF16), wk.astype(BF16), wv.astype(BF16)


def rope_tables(seq):
    inv = 1.0 / (ROPE_THETA ** (jnp.arange(0, MLA_ROPE, 2, dtype=F32) / MLA_ROPE))
    ang = jnp.arange(seq, dtype=F32)[:, None] * inv[None, :]
    c, s = jnp.cos(ang), jnp.sin(ang)
    ones = jnp.ones((seq, MLA_NOPE), F32)
    zpad = jnp.zeros((seq, MLA_HP - MLA_NOPE - MLA_ROPE), F32)
    cos_t = jnp.concatenate([ones, c, c, zpad], axis=-1)
    sin_t = jnp.concatenate([0.0 * ones, s, s, zpad], axis=-1)
    return cos_t, sin_t


def _mla_prep_kernel(r_ref, qg_ref, kg_ref, wq1_ref, wq2_ref, wk_ref, wv_ref, cos_ref, sin_ref,
                     q_ref, k_ref, v_ref):
    scale = (MLA_NOPE + MLA_ROPE) ** -0.5 * LOG2E
    cq = _rms(r_ref[:, 0:MLA_Q_RANK].astype(F32), qg_ref[...]).astype(BF16)
    o1 = MLA_Q_RANK + MLA_KV_RANK
    ckv = _rms(r_ref[:, MLA_Q_RANK:o1].astype(F32), kg_ref[...]).astype(BF16)
    cos_t, sin_t = cos_ref[...], sin_ref[...]
    kr = r_ref[:, o1:o1 + LANES].astype(F32) * cos_t + r_ref[:, o1 + LANES:o1 + 2 * LANES].astype(F32) * sin_t
    v_ref[...] = _dot(ckv, wv_ref[...]).astype(v_ref.dtype)
    for h in range(MLA_HEADS):
        sl = slice(h * MLA_HP, (h + 1) * MLA_HP)
        qh = _dot(cq, wq1_ref[:, sl]) * cos_t + _dot(cq, wq2_ref[:, sl]) * sin_t
        q_ref[:, sl] = (qh * scale).astype(q_ref.dtype)
        k_ref[:, sl] = (_dot(ckv, wk_ref[:, sl]) + kr).astype(k_ref.dtype)


def mla_prep(rest, q_norm, kv_norm, wq1, wq2, wk, wv, cos_t, sin_t, bsz, seq, tm=512):
    tm = min(tm, seq)
    nl = seq // tm
    t = bsz * seq
    wr = rest.shape[1]
    row = lambda b, i: (b * nl + i, 0)
    const = lambda b, i: (0, 0)
    qk = MLA_HEADS * MLA_HP
    return pl.pallas_call(
        _mla_prep_kernel,
        grid=(bsz, nl),
        in_specs=[pl.BlockSpec((tm, wr), row),
                  pl.BlockSpec((1, MLA_Q_RANK), const),
                  pl.BlockSpec((1, MLA_KV_RANK), const),
                  pl.BlockSpec((MLA_Q_RANK, qk), const),
                  pl.BlockSpec((MLA_Q_RANK, qk), const),
                  pl.BlockSpec((MLA_KV_RANK, qk), const),
                  pl.BlockSpec((MLA_KV_RANK, MLA_HEADS * MLA_V), const),
                  pl.BlockSpec((tm, MLA_HP), lambda b, i: (i, 0)),
                  pl.BlockSpec((tm, MLA_HP), lambda b, i: (i, 0))],
        out_specs=[pl.BlockSpec((tm, qk), row), pl.BlockSpec((tm, qk), row),
                   pl.BlockSpec((tm, MLA_HEADS * MLA_V), row)],
        out_shape=[jax.ShapeDtypeStruct((t, qk), BF16), jax.ShapeDtypeStruct((t, qk), BF16),
                   jax.ShapeDtypeStruct((t, MLA_HEADS * MLA_V), BF16)],
        compiler_params=_cparams("parallel", "parallel"),
        name="mla_prep",
    )(rest, q_norm.reshape(1, -1), kv_norm.reshape(1, -1), wq1, wq2, wk, wv, cos_t, sin_t)


def _mla_attn_kernel(q_ref, k_ref, v_ref, o_ref, *, kb):
    nk = k_ref.shape[0] // kb
    outs = []
    for h in range(2):
        sl = slice(h * MLA_HP, (h + 1) * MLA_HP)
        q = q_ref[:, sl]
        m = den = acc = None
        for j in range(nk):
            ks = slice(j * kb, (j + 1) * kb)
            s = _dot_nt(q, k_ref[ks, sl])
            mj = jnp.max(s, axis=-1, keepdims=True)
            if j == 0:
                m = mj
                p = jnp.exp2(s - m)
                den = jnp.sum(p, axis=-1, keepdims=True)
                acc = _dot(p.astype(BF16), v_ref[ks, :])
            else:
                m_new = jnp.maximum(m, mj)
                alpha = jnp.exp2(m - m_new)
                p = jnp.exp2(s - m_new)
                den = alpha * den + jnp.sum(p, axis=-1, keepdims=True)
                acc = alpha * acc + _dot(p.astype(BF16), v_ref[ks, :])
                m = m_new
        outs.append(acc / den)
    lane = lax.broadcasted_iota(jnp.int32, outs[0].shape, 1)
    o_ref[...] = jnp.where(lane < MLA_V, outs[0], outs[1]).astype(o_ref.dtype)


def mla_attn(q, k, v, bsz, seq, tq=512, kb=256):
    tq = min(tq, seq)
    nq = seq // tq
    return pl.pallas_call(
        functools.partial(_mla_attn_kernel, kb=min(kb, seq)),
        grid=(bsz, MLA_HEADS // 2, nq),
        in_specs=[pl.BlockSpec((tq, 2 * MLA_HP), lambda b, p, i: (b * nq + i, p)),
                  pl.BlockSpec((seq, 2 * MLA_HP), lambda b, p, i: (b, p)),
                  pl.BlockSpec((seq, 2 * MLA_V), lambda b, p, i: (b, p))],
        out_specs=pl.BlockSpec((tq, 2 * MLA_V), lambda b, p, i: (b * nq + i, p)),
        out_shape=jax.ShapeDtypeStruct((bsz * seq, MLA_HEADS * MLA_V), BF16),
        compiler_params=_cparams("parallel", "parallel", "parallel"),
        name="mla_attn",
    )(q, k, v)


def _cross_router_kernel(x_ref, kv_ref, gc_ref, wq_ref, wo_ref, gf_ref, wr_ref, xo_ref, hn_ref, aff_ref, *, nsplit):
    rows = x_ref.shape[0] // nsplit
    for i in range(nsplit):
        sl = slice(i * rows, (i + 1) * rows)
        _cross_router_rows(x_ref.at[sl], kv_ref, gc_ref, wq_ref, wo_ref, gf_ref, wr_ref,
                           xo_ref.at[sl], hn_ref.at[sl], aff_ref.at[sl])


def _cross_router_rows(x_ref, kv_ref, gc_ref, wq_ref, wo_ref, gf_ref, wr_ref, xo_ref, hn_ref, aff_ref):
    x = x_ref[...]
    h = _rms(x, gc_ref[...]).astype(BF16)
    q = (_dot(h, wq_ref[...]) * (CA_HEAD_DIM ** -0.5 * LOG2E)).astype(BF16)
    hd = CA_HEADS * CA_HEAD_DIM
    outs = []
    for a in range(CA_HEADS):
        sl = slice(a * CA_HEAD_DIM, (a + 1) * CA_HEAD_DIM)
        s = _dot_nt(q[:, sl], kv_ref[:, sl])
        m = jnp.max(s, axis=-1, keepdims=True)
        p = jnp.exp2(s - m)
        den = jnp.sum(p, axis=-1, keepdims=True)
        outs.append((_dot(p.astype(BF16), kv_ref[:, hd + a * CA_HEAD_DIM:hd + (a + 1) * CA_HEAD_DIM]) / den))
    o = jnp.concatenate(outs, axis=-1).astype(BF16)
    xn = x + _dot(o, wo_ref[...])
    xo_ref[...] = xn
    hf = _rms(xn, gf_ref[...])
    hb = hf.astype(BF16)
    hn_ref[...] = hb
    lo = (hf - hb.astype(F32)).astype(BF16)
    hw = _dot(hb, wr_ref[...])
    logits = hw[:, 0:LANES] + (_dot(lo, wr_ref[:, 0:LANES]) + hw[:, LANES:])
    lane = lax.broadcasted_iota(jnp.int32, logits.shape, 1)
    logits = jnp.where(lane < N_EXPERTS, logits, NEG)
    m = jnp.max(logits, axis=-1, keepdims=True)
    e = jnp.exp(logits - m)
    aff = e / jnp.sum(e, axis=-1, keepdims=True)
    aff_ref[...] = aff[:, 0:N_EXPERTS]


def cross_router(x2d, kv, ln_cross, w_q, w_o, ln_ffn, w_router, bsz, seq, tq=1024, sub=256):
    tq = min(tq, seq)
    nsplit = max(tq // sub, 1)
    nq = seq // tq
    t = bsz * seq
    mem = kv.shape[0] // bsz
    hd = CA_HEADS * CA_HEAD_DIM
    wr = jnp.pad(w_router.astype(F32), ((0, 0), (0, LANES - N_EXPERTS)))
    wr_hi = wr.astype(BF16)
    wr2 = jnp.concatenate([wr_hi, (wr - wr_hi.astype(F32)).astype(BF16)], axis=1)
    row = lambda b, i: (b * nq + i, 0)
    const = lambda b, i: (0, 0)
    return pl.pallas_call(
        functools.partial(_cross_router_kernel, nsplit=nsplit),
        grid=(bsz, nq),
        in_specs=[pl.BlockSpec((tq, D_MODEL), row),
                  pl.BlockSpec((mem, 2 * hd), lambda b, i: (b, 0)),
                  pl.BlockSpec((1, D_MODEL), const),
                  pl.BlockSpec((D_MODEL, hd), const),
                  pl.BlockSpec((hd, D_MODEL), const),
                  pl.BlockSpec((1, D_MODEL), const),
                  pl.BlockSpec((D_MODEL, 2 * LANES), const)],
        out_specs=[pl.BlockSpec((tq, D_MODEL), row), pl.BlockSpec((tq, D_MODEL), row),
                   pl.BlockSpec((tq, N_EXPERTS), row)],
        out_shape=[jax.ShapeDtypeStruct((t, D_MODEL), F32), jax.ShapeDtypeStruct((t, D_MODEL), BF16),
                   jax.ShapeDtypeStruct((t, N_EXPERTS), F32)],
        compiler_params=_cparams("parallel", "parallel"),
        name="cross_router",
    )(x2d, kv, ln_cross.reshape(1, -1), w_q, w_o, ln_ffn.reshape(1, -1), wr2)


def _expert_ffn_kernel(x_ref, g_ref, wg_ref, wu_ref, wd_ref, o_ref, hid_ref, *, tf):
    x = x_ref[0]
    for c0 in range(0, D_EXPERT, tf):
        a = _dot(x, wg_ref[0, 0, :, c0:c0 + tf])
        u = _dot(x, wu_ref[0, 0, :, c0:c0 + tf])
        hid_ref[:, c0:c0 + tf] = (a * jax.nn.sigmoid(a) * u).astype(BF16)
    o_ref[0] = (_dot(hid_ref[...], wd_ref[0, 0]) * g_ref[0]).astype(o_ref.dtype)


def expert_ffn(xe, gate, w_gate, w_up, w_down, layer, tm=1024, tf=512):
    e, cap, _ = xe.shape
    tm = min(tm, cap)
    return pl.pallas_call(
        functools.partial(_expert_ffn_kernel, tf=tf),
        grid=(e, cap // tm),
        in_specs=[pl.BlockSpec((1, tm, D_MODEL), lambda e, m: (e, m, 0)),
                  pl.BlockSpec((1, tm, 1), lambda e, m: (e, m, 0)),
                  pl.BlockSpec((1, 1, D_MODEL, D_EXPERT), lambda e, m: (layer, e, 0, 0)),
                  pl.BlockSpec((1, 1, D_MODEL, D_EXPERT), lambda e, m: (layer, e, 0, 0)),
                  pl.BlockSpec((1, 1, D_EXPERT, D_MODEL), lambda e, m: (layer, e, 0, 0))],
        out_specs=pl.BlockSpec((1, tm, D_MODEL), lambda e, m: (e, m, 0)),
        out_shape=jax.ShapeDtypeStruct((e, cap, D_MODEL), BF16),
        scratch_shapes=[pltpu.VMEM((tm, D_EXPERT), BF16)],
        compiler_params=_cparams("parallel", "arbitrary"),
        name="expert_ffn",
    )(xe, gate, w_gate, w_up, w_down)


ROUTE_GROUP = SUBLANES


def _route_thr_kernel(a_ref, thr_ref, *, cap):
    bits = pltpu.bitcast(a_ref[...], jnp.int32)

    def body(i, lo):
        cand = lo | jnp.left_shift(jnp.int32(1), 30 - i)
        cnt = jnp.sum(jnp.where(bits >= cand, 1.0, 0.0), axis=1, keepdims=True)
        return jnp.where(cnt >= cap, cand, lo)

    thr_ref[...] = lax.fori_loop(0, 31, body, jnp.zeros(thr_ref.shape, jnp.int32))


def _prefix_rows(m, upper, lower):
    mb = m.astype(BF16)
    incl = _dot(mb, upper)
    tot = jnp.broadcast_to(incl[:, LANES - 1:LANES], incl.shape).astype(BF16)
    return incl - m + _dot(lower, tot)


def _route_mask_kernel(thr_ref, a_ref, upper_ref, lower_ref, sel_ref, pos_ref, *, cap):
    e = pl.program_id(0)
    thr = thr_ref[e]
    bits = pltpu.bitcast(a_ref[0], jnp.int32)
    gt = jnp.where(bits > thr, 1.0, 0.0)
    eq = jnp.where(bits == thr, 1.0, 0.0)
    need = cap - jnp.sum(jnp.sum(gt, axis=1, keepdims=True), axis=0, keepdims=True)
    eq_rank = _prefix_rows(eq, upper_ref[...], lower_ref[...])
    sel = gt + jnp.where(eq_rank < need, eq, 0.0)
    sel_ref[0] = sel
    pos_ref[0] = _prefix_rows(sel, upper_ref[...], lower_ref[...]).astype(jnp.int32)


def _route_compact_kernel(glo_ref, ghi_ref, a_ref, pos_ref, idx_ref, gate_ref):
    e = pl.program_id(0)
    nc = idx_ref.shape[1]
    gw = ROUTE_GROUP * LANES
    slot0 = lax.broadcasted_iota(jnp.int32, (LANES, LANES), 0)
    lane = lax.broadcasted_iota(jnp.int32, (1, gw), 1)
    zeros = jnp.zeros((2 * SUBLANES - 5, gw), F32)

    ngroups = pos_ref.shape[1] // ROUTE_GROUP
    never = jnp.int32(1 << 30)

    def terms(g, slot):
        r0 = pl.multiple_of(g * ROUTE_GROUP, ROUTE_GROUP)
        pos = pos_ref[0, pl.ds(r0, ROUTE_GROUP), :]
        aff = a_ref[0, pl.ds(r0, ROUTE_GROUP), :]
        hit = jnp.concatenate(
            [jnp.where(pos[j:j + 1] == slot, 1.0, 0.0) for j in range(ROUTE_GROUP)],
            axis=1).astype(BF16)
        arow = jnp.concatenate([aff[j:j + 1] for j in range(ROUTE_GROUP)], axis=1)
        tok = lane + g * gw
        g0 = arow.astype(BF16).astype(F32)
        r1 = arow - g0
        g1 = r1.astype(BF16).astype(F32)
        g2 = r1 - g1
        lhs = jnp.concatenate([(tok >> 8).astype(F32), (tok & 255).astype(F32), g0, g1, g2, zeros], axis=0)
        return lhs.astype(BF16), hit

    def token_id(acc):
        return (acc[0:1] * 256.0 + acc[1:2]).astype(jnp.int32)

    def affinity(acc):
        return (acc[2:3] + acc[3:4]) + acc[4:5]

    def first_two(c, carry):
        slot = slot0 + c * LANES
        g_first = glo_ref[e * nc + c]
        g_second = jnp.minimum(g_first + 1, ngroups - 1)
        lhs_a, hit_a = terms(g_first, slot)
        lhs_b, hit_b = terms(g_second, slot + jnp.where(g_first + 1 <= ghi_ref[e * nc + c], 0, never))
        acc = _dot_nt(jnp.concatenate([lhs_a, lhs_b], axis=1), jnp.concatenate([hit_a, hit_b], axis=1))
        idx_ref[0, pl.ds(c, 1), :] = token_id(acc)
        gate_ref[0, pl.ds(c, 1), :] = affinity(acc)
        return carry

    lax.fori_loop(0, nc, first_two, 0, unroll=4)

    def further(c, carry):
        g_first = glo_ref[e * nc + c]
        g_last = ghi_ref[e * nc + c]

        @pl.when(g_last >= g_first + 2)
        def _():
            slot = slot0 + c * LANES

            def group(g, acc):
                lhs, hit = terms(g, slot)
                return acc + _dot_nt(lhs, hit)

            acc = lax.fori_loop(g_first + 2, g_last + 1, group, jnp.zeros((2 * SUBLANES, LANES), F32))
            idx_ref[0, pl.ds(c, 1), :] += token_id(acc)
            gate_ref[0, pl.ds(c, 1), :] += affinity(acc)

        return carry

    lax.fori_loop(0, nc, further, 0)


def route_tokens(aff, cap):
    t, ne = aff.shape
    rows = t // LANES
    nc = cap // LANES
    aff_t = aff.T
    thr = pl.pallas_call(
        functools.partial(_route_thr_kernel, cap=cap),
        out_shape=jax.ShapeDtypeStruct((ne, 1), jnp.int32),
        compiler_params=pltpu.CompilerParams(vmem_limit_bytes=V7X_VMEM_LIMIT_BYTES),
        name="route_threshold",
    )(aff_t)
    aff3 = aff_t.reshape(ne, rows, LANES)
    ii = jnp.arange(LANES)
    upper = (ii[:, None] <= ii[None, :]).astype(BF16)
    rr = jnp.arange(rows)
    lower = (rr[None, :] < rr[:, None]).astype(BF16)
    blk = pl.BlockSpec((1, rows, LANES), lambda e: (e, 0, 0))
    sel, pos = pl.pallas_call(
        functools.partial(_route_mask_kernel, cap=cap),
        grid=(ne,),
        in_specs=[pl.BlockSpec(memory_space=pltpu.SMEM), blk,
                  pl.BlockSpec((LANES, LANES), lambda e: (0, 0)),
                  pl.BlockSpec((rows, rows), lambda e: (0, 0))],
        out_specs=[blk, blk],
        out_shape=[jax.ShapeDtypeStruct((ne, rows, LANES), F32), jax.ShapeDtypeStruct((ne, rows, LANES), jnp.int32)],
        compiler_params=_cparams("parallel"),
        name="route_mask",
    )(thr.reshape(ne), aff3, upper, lower)
    gsz = ROUTE_GROUP
    first = pos[:, ::gsz, 0]
    starts = jnp.arange(nc, dtype=jnp.int32) * LANES
    glo = jnp.sum(first[:, None, :] <= starts[None, :, None], axis=-1, dtype=jnp.int32) - 1
    ghi = jnp.sum(first[:, None, :] <= (starts + (LANES - 1))[None, :, None], axis=-1, dtype=jnp.int32) - 1
    posm = jnp.where(sel > 0, pos, -1)
    out_blk = pl.BlockSpec((1, nc, LANES), lambda e, *_: (e, 0, 0))
    grid_spec = pltpu.PrefetchScalarGridSpec(
        num_scalar_prefetch=2,
        grid=(ne,),
        in_specs=[pl.BlockSpec((1, rows, LANES), lambda e, *_: (e, 0, 0))] * 2,
        out_specs=[out_blk, out_blk],
    )
    idx, gate = pl.pallas_call(
        _route_compact_kernel,
        grid_spec=grid_spec,
        out_shape=[jax.ShapeDtypeStruct((ne, nc, LANES), jnp.int32), jax.ShapeDtypeStruct((ne, nc, LANES), F32)],
        compiler_params=_cparams("parallel"),
        name="route_compact",
    )(glo.reshape(-1), ghi.reshape(-1), aff3, posm)
    return idx.reshape(ne, cap), gate.reshape(ne, cap), pos.reshape(ne, t), posm.reshape(ne, t)


COMBINE_SUB = 512
COMBINE_ROWS = 96
COMBINE_XROWS = 128
COMBINE_ALIGN = 16


def _combine_kernel(wst_ref, nex_ref, x_hbm, pos_ref, rel_ref, spread_ref, rowin_ref, g_ref, ye_hbm, o_ref,
                    ybuf, sem, xbuf, xsem, xring, xrsem, *, nsub, cap, final_norm):
    b = pl.program_id(0)
    nb = pl.num_programs(0)
    ne = N_EXPERTS
    w = COMBINE_ROWS
    xw = COMBINE_XROWS
    sub = COMBINE_SUB
    slot = b % 2
    lane = lax.broadcasted_iota(jnp.int32, (1, LANES), 1)
    row_in_window = rowin_ref[...]

    def window(bb, sl, u, e):
        st = pl.multiple_of(wst_ref[(bb * nsub + u) * ne + e], COMBINE_ALIGN)
        return pltpu.make_async_copy(ye_hbm.at[e, pl.ds(st, w), :], ybuf.at[sl, u, pl.ds(e * w, w), :],
                                     sem.at[sl, u, e])

    def start_all(bb, sl):
        for u in range(nsub):
            for e in range(ne):
                window(bb, sl, u, e).start()

    tb = sub * nsub
    ring = xring.shape[0]

    def x_block(bb):
        src = x_hbm.at[pl.ds(pl.multiple_of(bb * tb, tb), tb), :]
        return pltpu.make_async_copy(src, xring.at[bb % ring], xrsem.at[bb % ring])

    @pl.when(b == 0)
    def _():
        start_all(b, slot)
        x_block(b).start()

    @pl.when(jnp.logical_and(b == 0, nb > 1))
    def _():
        x_block(b + 1).start()

    @pl.when(b + 1 < nb)
    def _():
        start_all(b + 1, 1 - slot)

    @pl.when(b + 2 < nb)
    def _():
        x_block(b + 2).start()

    x_block(b).wait()
    x_ref = xring.at[b % ring]

    for u in range(nsub):
        rows = slice(u * sub, (u + 1) * sub)
        base = (b * nsub + u) * ne
        spread = _dot(rel_ref[rows, :], spread_ref[...])
        onehot = jnp.where(spread == row_in_window, 1.0, 0.0).astype(BF16)
        for e in range(ne):
            window(b, slot, u, e).wait()
        o_ref[rows, :] = x_ref[rows, :] + _dot(onehot, ybuf[slot, u])

        for e in range(ne):
            def extra(k, carry, e=e, rows=rows, base=base):
                first = wst_ref[base + e] + w + (k - 1) * xw
                st = pl.multiple_of(jnp.minimum(first, cap - xw), COMBINE_ALIGN)
                cp = pltpu.make_async_copy(ye_hbm.at[e, pl.ds(st, xw), :], xbuf, xsem.at[0])
                cp.start()
                cp.wait()
                col = jnp.broadcast_to(pos_ref[rows, e:e + 1], (sub, xw))
                hit = jnp.logical_and(col - st == lane, col >= first)
                o_ref[rows, :] += _dot(jnp.where(hit, 1.0, 0.0).astype(BF16), xbuf[...])
                return carry

            lax.fori_loop(1, nex_ref[base + e] + 1, extra, 0)

        if final_norm:
            o_ref[rows, :] = _rms(o_ref[rows, :], g_ref[...])


def moe_combine(x2d, ye, pos, posm, norm_gain=None, nsub=1):
    t = x2d.shape[0]
    ne, cap = ye.shape[0], ye.shape[1]
    sub = COMBINE_SUB
    nsub = min(nsub, t // sub)
    tb = sub * nsub
    nb = t // tb
    w = COMBINE_ROWS
    lo = pos[:, ::sub]
    hi = jnp.concatenate([lo[:, 1:], jnp.full((ne, 1), cap, jnp.int32)], axis=1)
    wst = jnp.minimum((lo // COMBINE_ALIGN) * COMBINE_ALIGN, cap - COMBINE_XROWS)
    nex = jnp.maximum(hi - (wst + w) + (COMBINE_XROWS - 1), 0) // COMBINE_XROWS
    posm = posm.T
    rel = posm - jnp.repeat(wst.T, sub, axis=0)
    rel = jnp.where(posm >= 0, jnp.minimum(rel, 2 * LANES - 1), -1).astype(BF16)
    stacked = jnp.arange(ne * w, dtype=jnp.int32)
    spread = stacked[None, :] // w == jnp.arange(ne, dtype=jnp.int32)[:, None]
    rowin = (stacked % w).astype(F32).reshape(1, ne * w)
    gain = jnp.ones((1, D_MODEL), F32) if norm_gain is None else norm_gain.astype(F32).reshape(1, D_MODEL)
    grid_spec = pltpu.PrefetchScalarGridSpec(
        num_scalar_prefetch=2,
        grid=(nb,),
        in_specs=[pl.BlockSpec(memory_space=pl.ANY),
                  pl.BlockSpec((tb, ne), lambda b, *_: (b, 0)),
                  pl.BlockSpec((tb, ne), lambda b, *_: (b, 0)),
                  pl.BlockSpec((ne, ne * w), lambda b, *_: (0, 0)),
                  pl.BlockSpec((1, ne * w), lambda b, *_: (0, 0)),
                  pl.BlockSpec((1, D_MODEL), lambda b, *_: (0, 0)),
                  pl.BlockSpec(memory_space=pl.ANY)],
        out_specs=pl.BlockSpec((tb, D_MODEL), lambda b, *_: (b, 0)),
        scratch_shapes=[pltpu.VMEM((2, nsub, ne * w, D_MODEL), BF16),
                        pltpu.SemaphoreType.DMA((2, nsub, ne)),
                        pltpu.VMEM((COMBINE_XROWS, D_MODEL), BF16),
                        pltpu.SemaphoreType.DMA((1,)),
                        pltpu.VMEM((3, tb, D_MODEL), F32),
                        pltpu.SemaphoreType.DMA((3,))],
    )
    return pl.pallas_call(
        functools.partial(_combine_kernel, nsub=nsub, cap=cap, final_norm=norm_gain is not None),
        grid_spec=grid_spec,
        out_shape=jax.ShapeDtypeStruct(x2d.shape, F32),
        compiler_params=_cparams("arbitrary"),
        name="moe_combine",
    )(wst.T.reshape(-1), nex.T.reshape(-1), x2d, posm, rel, spread.astype(BF16), rowin, gain, ye)


def mixer_ab(x2d, e, p, shared, bsz, seq):
    proj = norm_proj(x2d, p['ln_mix_l'], p['w_in_ab'][e], bsz, seq, [(0, AB_IN, BF16, False)])[0]
    vx, x0 = hyena_pre(proj, p['hy_conv_w'][e], p['hy_conv_b'][e], bsz, seq)
    kp, kq, kp2 = shared['hy_spec'][e]
    y_hy = hyena_conv(vx, x0, shared['dft_fwd'], shared['dft_inv'], kp, kq, kp2, p['hy_d'][e], bsz, seq)
    y_wa = window_attn(proj, p['attn_sink'][e], shared['wa_bias'], bsz, seq)
    return out_proj_ab(x2d, y_hy, y_wa, p['w_out_ab'][e])


def mixer_cd(x2d, o, p, shared, bsz, seq):
    u2d, rest = norm_proj(x2d, p['ln_mix_l'], shared['w_in_cd'][o], bsz, seq,
                          [(0, HALF, F32, False), (HALF, CD_PAD - HALF, BF16, False)])
    a5, bm, cm = shared['s5'][o]
    y_dirs = s5_scan(u2d.reshape(bsz, seq, HALF), a5, bm, cm, bsz, seq)
    wq1, wq2, wk, wv = shared['mla_w'][o]
    q, k, v = mla_prep(rest, p['mla_q_norm'][o], p['mla_kv_norm'][o], wq1, wq2, wk, wv,
                       shared['rope_cos'], shared['rope_sin'], bsz, seq)
    y_mla = mla_attn(q, k, v, bsz, seq)
    return out_proj_cd(x2d, u2d, y_dirs.reshape(2, bsz * seq, HALF), y_mla, p['s5_d'][o],
                       p['s5_glu_w'][o], p['s5_glu_b'][o], p['w_out_cd'][o], bsz, seq)


def ec_moe(x2d, hn, aff, w_gate, w_up, w_down, layer, norm_gain=None):
    t = x2d.shape[0]
    cap = EC_CAPACITY_FACTOR * t // N_EXPERTS
    idx, gate, pos, posm = route_tokens(aff, cap)
    xe = hn[idx]
    ye = expert_ffn(xe, gate[..., None], w_gate, w_up, w_down, layer)
    return moe_combine(x2d, ye, pos, posm, norm_gain)


def prepare_shared(p, seq):
    sh = {}
    sh['wa_bias'] = window_bias_mask(p['rel_bias'])
    sh['dft_fwd'], sh['dft_inv'] = dft_matrices(seq)
    sh['rope_cos'], sh['rope_sin'] = rope_tables(seq)
    specs = []
    for e in range(p['w_in_ab'].shape[0]):
        h = hyena_filters(seq, p['hy_filt_w1'][e], p['hy_filt_b1'][e], p['hy_filt_w2'][e], p['hy_filt_b2'][e],
                          p['hy_filt_w3'][e], p['hy_filt_freq'][e])
        h_fwd, h_bwd = h[:, :HY_WIDTH], h[:, HY_WIDTH:]
        k = jnp.concatenate([h_fwd, jnp.zeros_like(h_fwd[:1]), h_bwd[:0:-1]], axis=0)
        kf = kernel_spectrum(sh['dft_fwd'], k)
        k_r, k_s = kf[:seq], kf[seq:]
        specs.append((k_r, k_s.at[0].set(0.0), k_r.at[0].set(k_s[0])))
    sh['hy_spec'] = specs
    s5, mla_w, w_in_cd = [], [], []
    for o in range(p['w_in_cd'].shape[0]):
        s5.append(s5_discretise(p['s5_a_re'][o], p['s5_a_im'][o], p['s5_log_dt'][o], p['s5_b_re'][o],
                                p['s5_b_im'][o], p['s5_c_re'][o], p['s5_c_im'][o]))
        mla_w.append(mla_weights(p['mla_w_uq'][o], p['mla_w_ukv'][o]))
        w = p['w_in_cd'][o].astype(F32)
        o2 = HALF + MLA_Q_RANK + MLA_KV_RANK
        kr = w[:, o2:o2 + MLA_ROPE]
        half = MLA_ROPE // 2
        kr_rot = jnp.concatenate([-kr[:, half:], kr[:, :half]], axis=1)
        z64 = jnp.zeros((D_MODEL, MLA_NOPE), F32)
        z32 = jnp.zeros((D_MODEL, MLA_HP - MLA_NOPE - MLA_ROPE), F32)
        w_in_cd.append(jnp.concatenate([w[:, :o2], z64, kr, z32, z64, kr_rot, z32], axis=1).astype(BF16))
    sh['s5'], sh['mla_w'], sh['w_in_cd'] = s5, mla_w, w_in_cd
    return sh


def run_trunk(x, mem, p, shared):
    bsz, seq, _ = x.shape
    x2d = x.reshape(bsz * seq, D_MODEL)
    mem2d = mem.reshape(bsz * mem.shape[1], D_MODEL)
    for layer in range(DEPTH):
        pl_ = dict(p, ln_mix_l=p['ln_mix'][layer])
        if layer % 2 == 0:
            x2d = mixer_ab(x2d, layer // 2, pl_, shared, bsz, seq)
        else:
            x2d = mixer_cd(x2d, layer // 2, pl_, shared, bsz, seq)
        kv = norm_proj(mem2d, p['ln_mem'][layer], p['ca_w_kv'][layer], bsz, mem.shape[1],
                       [(0, 2 * CA_HEADS * CA_HEAD_DIM, BF16, False)])[0]
        x2d, hn, aff = cross_router(x2d, kv, p['ln_cross'][layer], p['ca_w_q'][layer], p['ca_w_o'][layer],
                                    p['ln_ffn'][layer], p['moe_w_router'][layer], bsz, seq)
        x2d = ec_moe(x2d, hn, aff, p['moe_w_gate'], p['moe_w_up'], p['moe_w_down'], layer,
                     norm_gain=p['ln_final'] if layer == DEPTH - 1 else None)
    return x2d.reshape(bsz, seq, D_MODEL)


_BF16_WEIGHTS = ('w_in_ab', 'w_out_ab', 'w_out_cd', 's5_glu_w', 'ca_w_q', 'ca_w_kv', 'ca_w_o',
                 'moe_w_gate', 'moe_w_up', 'moe_w_down')


def kernel(x_prompt, x_sample, mem_prompt, mem_sample, ln_mix, ln_cross, ln_mem, ln_ffn, ln_final, rel_bias, w_in_ab, w_out_ab, hy_conv_w, hy_conv_b, hy_filt_w1, hy_filt_b1, hy_filt_w2, hy_filt_b2, hy_filt_w3, hy_filt_freq, hy_d, attn_sink, w_in_cd, w_out_cd, s5_a_re, s5_a_im, s5_log_dt, s5_b_re, s5_b_im, s5_c_re, s5_c_im, s5_d, s5_glu_w, s5_glu_b, mla_q_norm, mla_w_uq, mla_kv_norm, mla_w_ukv, ca_w_q, ca_w_kv, ca_w_o, moe_w_router, moe_w_gate, moe_w_up, moe_w_down):
    p = dict(ln_mix=ln_mix, ln_cross=ln_cross, ln_mem=ln_mem, ln_ffn=ln_ffn, ln_final=ln_final,
             rel_bias=rel_bias, w_in_ab=w_in_ab, w_out_ab=w_out_ab, hy_conv_w=hy_conv_w,
             hy_conv_b=hy_conv_b, hy_filt_w1=hy_filt_w1, hy_filt_b1=hy_filt_b1,
             hy_filt_w2=hy_filt_w2, hy_filt_b2=hy_filt_b2, hy_filt_w3=hy_filt_w3,
             hy_filt_freq=hy_filt_freq, hy_d=hy_d, attn_sink=attn_sink, w_in_cd=w_in_cd,
             w_out_cd=w_out_cd, s5_a_re=s5_a_re, s5_a_im=s5_a_im, s5_log_dt=s5_log_dt,
             s5_b_re=s5_b_re, s5_b_im=s5_b_im, s5_c_re=s5_c_re, s5_c_im=s5_c_im, s5_d=s5_d,
             s5_glu_w=s5_glu_w, s5_glu_b=s5_glu_b, mla_q_norm=mla_q_norm, mla_w_uq=mla_w_uq,
             mla_kv_norm=mla_kv_norm, mla_w_ukv=mla_w_ukv, ca_w_q=ca_w_q, ca_w_kv=ca_w_kv,
             ca_w_o=ca_w_o, moe_w_router=moe_w_router, moe_w_gate=moe_w_gate,
             moe_w_up=moe_w_up, moe_w_down=moe_w_down)
    assert x_prompt.shape[1] == x_sample.shape[1]
    shared = prepare_shared(p, x_prompt.shape[1])
    for name in _BF16_WEIGHTS:
        p[name] = p[name].astype(BF16)
    y_prompt = run_trunk(x_prompt, mem_prompt, p, shared)
    y_sample = run_trunk(x_sample, mem_sample, p, shared)
    return (y_prompt, y_sample)
```
